```python
import math
import jax
import jax.numpy as jnp
from jax import lax
import numpy as np

D_MODEL = 1024
BATCH = 32
SEQ = 256
DEPTH = 2
DEC_BATCH = 4
DEC_SEQ = 1024
PAST_LEN = 256

GRID_W = 64
N_EVEN = (DEPTH + 1) // 2
N_ODD = DEPTH // 2
ROPE_THETA = 10000.0
Q_BLOCK = 128
EPS = 1e-6
DN_ALPHA = (2 * DEPTH) ** 0.25
DN_BETA = (8 * DEPTH) ** -0.25

H_A = 4
D_A = 64
H_B = 8
B_NOPE = 64
B_ROPE = 32
B_V = 64
B_QRANK = 256
B_KVRANK = 128
H_C = 8
G_C = 2
D_C = 64
H_D = 8
D_D = 64
NA_ROWS = 8
NA_COLS = 16
PEER_HEADS = 8
PEER_KEYS = 128
PEER_EXPERTS = PEER_KEYS * PEER_KEYS
PEER_DKEY = 256
PEER_TOPK = 16
PEER_CHUNK = 128

EVEN_SIZES = (H_A * 2 * D_A, H_A * 2 * D_A, H_A * 2 * D_A, B_QRANK, B_KVRANK, B_ROPE)
ODD_SIZES = (H_C * D_C, G_C * D_C, G_C * D_C, H_D * D_D, H_D * D_D, H_D * D_D)
EVEN_IN = 3 * H_A * 2 * D_A + B_QRANK + B_KVRANK + B_ROPE
ODD_IN = H_C * D_C + 2 * G_C * D_C + 3 * H_D * D_D
EVEN_MIX = H_A * 2 * D_A + H_B * B_V
ODD_MIX = H_C * D_C + H_D * D_D

kernel_name = 'hybrid_diffusion_prefix_step'


def split_points(sizes):
    pts, acc = [], 0
    for sz in sizes[:-1]:
        acc += sz
        pts.append(acc)
    return pts


def layer_norm(x, g, b):
    xf = x.astype(jnp.float32)
    mu = jnp.mean(xf, axis=-1, keepdims=True)
    var = jnp.mean(jnp.square(xf - mu), axis=-1, keepdims=True)
    return ((xf - mu) * lax.rsqrt(var + EPS) * g.astype(jnp.float32) + b.astype(jnp.float32)).astype(x.dtype)


def rms_norm(x, g):
    xf = x.astype(jnp.float32)
    return (xf * lax.rsqrt(jnp.mean(xf * xf, axis=-1, keepdims=True) + EPS) * g.astype(jnp.float32)).astype(x.dtype)


def softmax_f32(s):
    return jax.nn.softmax(s.astype(jnp.float32), axis=-1)


def grid_angles(n_tok, rot_dim):
    t = jnp.arange(n_tok)
    row = (t // GRID_W).astype(jnp.float32)
    col = (t % GRID_W).astype(jnp.float32)
    n_freq = rot_dim // 4
    inv = ROPE_THETA ** (-jnp.arange(n_freq, dtype=jnp.float32) / n_freq)
    return jnp.concatenate([row[:, None] * inv, col[:, None] * inv], axis=-1)


def apply_rope(x, ang):
    half = x.shape[-1] // 2
    shp = (1, x.shape[1]) + (1,) * (x.ndim - 3) + (half,)
    cos = jnp.cos(ang).reshape(shp).astype(x.dtype)
    sin = jnp.sin(ang).reshape(shp).astype(x.dtype)
    x1, x2 = x[..., :half], x[..., half:]
    return jnp.concatenate([x1 * cos - x2 * sin, x1 * sin + x2 * cos], axis=-1)


def sweep_query_blocks(fn, *qs):
    b, s = qs[0].shape[:2]
    nb = s // Q_BLOCK

    def split(a):
        return jnp.moveaxis(a.reshape((b, nb, Q_BLOCK) + a.shape[2:]), 1, 0)

    out = lax.map(lambda blk: fn(*blk), tuple(split(q) for q in qs))
    return jnp.moveaxis(out, 0, 1).reshape((b, s) + out.shape[3:])


def modulation(cvec, w_mod, b_mod):
    m = jax.nn.silu(cvec) @ w_mod + b_mod
    return jnp.split(m[:, None, :], 6, axis=-1)


def diff_attention(q, k, v, lam):
    s = jnp.einsum('bqhmd,bkhmd->bhmqk', q, k) * (D_A ** -0.5)
    p = softmax_f32(s)
    p = p[:, :, 0] - lam * p[:, :, 1]
    return jnp.einsum('bhqk,bkhe->bqhe', p.astype(v.dtype), v)


def latent_attention(q_nope, q_rope, k_nope, k_rope, v):
    s = jnp.einsum('bqhd,bkhd->bhqk', q_nope, k_nope) + jnp.einsum('bqhd,bkd->bhqk', q_rope, k_rope)
    p = softmax_f32(s * ((B_NOPE + B_ROPE) ** -0.5)).astype(v.dtype)
    return jnp.einsum('bhqk,bkhd->bqhd', p, v)


def gqa_attention(q, k, v):
    b, nq = q.shape[:2]
    qg = q.reshape(b, nq, G_C, H_C // G_C, D_C)
    s = jnp.einsum('bqgrd,bkgd->bgrqk', qg, k) * (D_C ** -0.5)
    p = softmax_f32(s).astype(v.dtype)
    return jnp.einsum('bgrqk,bkgd->bqgrd', p, v).reshape(b, nq, H_C, D_C)


def mha_attention(q, k, v, scale):
    s = jnp.einsum('bqhd,bkhd->bhqk', q, k) * scale
    p = softmax_f32(s).astype(v.dtype)
    return jnp.einsum('bhqk,bkhd->bqhd', p, v)


def neighbourhood_attention(q, k, v, ck, cv, rpb):
    b, s, h, d = q.shape
    rows = s // GRID_W
    wr = min(NA_ROWS, rows)
    wc = min(NA_COLS, GRID_W)
    r = jnp.arange(rows)
    col = jnp.arange(GRID_W)
    r0 = jnp.clip(r - wr // 2, 0, rows - wr)
    c0 = jnp.clip(col - wc // 2, 0, GRID_W - wc)
    key_rows = r0[:, None] + jnp.arange(wr)
    qg = q.reshape(b, rows, GRID_W, h, d)
    kg = k.reshape(b, rows, GRID_W, h, d)[:, key_rows]
    vg = v.reshape(b, rows, GRID_W, h, d)[:, key_rows]
    scale = d ** -0.5
    s_loc = jnp.einsum('brqhd,brjkhd->brhqjk', qg, kg).astype(jnp.float32) * scale
    dr = key_rows - r[:, None]
    dc = jnp.clip(col[None, :] - col[:, None], -(NA_COLS - 1), NA_COLS - 1)
    bias = rpb[:, dr[:, None, :, None] + (NA_ROWS - 1), dc[None, :, None, :] + (NA_COLS - 1)]
    s_loc = s_loc + jnp.moveaxis(bias, 0, 1).astype(jnp.float32)[None]
    col_ok = (col[None, :] >= c0[:, None]) & (col[None, :] < c0[:, None] + wc)
    s_loc = jnp.where(col_ok[None, None, None, :, None, :], s_loc, -jnp.inf)
    s_ctx = jnp.einsum('brqhd,blhd->brhql', qg, ck).astype(jnp.float32) * scale
    n_loc = wr * GRID_W
    p = softmax_f32(jnp.concatenate([s_loc.reshape(b, rows, h, GRID_W, n_loc), s_ctx], axis=-1))
    p = p.astype(v.dtype)
    p_loc = p[..., :n_loc].reshape(b, rows, h, GRID_W, wr, GRID_W)
    out = (jnp.einsum('brhqjk,brjkhd->brqhd', p_loc, vg)
           + jnp.einsum('brhql,blhd->brqhd', p[..., n_loc:], cv))
    return out.reshape(b, s, h, d)


def even_mixer(h, mix, lam_init, cache):
    b, s, _ = h.shape
    a_q, a_k, a_v, b_cq, b_ckv, b_kr = jnp.split(h @ mix['w_in'], split_points(EVEN_SIZES), axis=-1)
    a_q = a_q.reshape(b, s, H_A, 2, D_A)
    a_k = a_k.reshape(b, s, H_A, 2, D_A)
    a_v = a_v.reshape(b, s, H_A, 2 * D_A)
    bq = (rms_norm(b_cq, mix['g_cq']) @ mix['w_uq']).reshape(b, s, H_B, B_NOPE + B_ROPE)
    bq_nope, bq_rope = bq[..., :B_NOPE], bq[..., B_NOPE:]
    ckv = rms_norm(b_ckv, mix['g_ckv'])
    if cache is None:
        new_state = (a_k, a_v, ckv, b_kr)
        k_a, v_a, ckv_all, kr_all = a_k, a_v, ckv, b_kr
    else:
        ang_a = grid_angles(s, D_A)
        ang_b = grid_angles(s, B_ROPE)
        a_q = apply_rope(a_q, ang_a)
        bq_rope = apply_rope(bq_rope, ang_b)
        ck_a, cv_a, c_ckv, c_kr = cache
        k_a = jnp.concatenate([apply_rope(a_k, ang_a), ck_a], axis=1)
        v_a = jnp.concatenate([a_v, cv_a], axis=1)
        ckv_all = jnp.concatenate([ckv, c_ckv], axis=1)
        kr_all = jnp.concatenate([apply_rope(b_kr, ang_b), c_kr], axis=1)
        new_state = None
    n_k = ckv_all.shape[1]
    kv = (ckv_all @ mix['w_ukv']).reshape(b, n_k, H_B, B_NOPE + B_V)
    k_nope, v_b = kv[..., :B_NOPE], kv[..., B_NOPE:]
    f32 = jnp.float32
    lam = (jnp.exp(jnp.sum(mix['lam_q1'].astype(f32) * mix['lam_k1'].astype(f32)))
           - jnp.exp(jnp.sum(mix['lam_q2'].astype(f32) * mix['lam_k2'].astype(f32))) + lam_init)
    o_a = sweep_query_blocks(lambda q: diff_attention(q, k_a, v_a, lam), a_q)
    o_a = rms_norm(o_a, mix['g_sub']) * (1.0 - lam_init)
    o_b = sweep_query_blocks(lambda qn, qr: latent_attention(qn, qr, k_nope, kr_all, v_b), bq_nope, bq_rope)
    o = jnp.concatenate([o_a.reshape(b, s, -1), o_b.reshape(b, s, -1)], axis=-1)
    return o @ mix['w_out'], new_state


def odd_mixer(h, mix, cache):
    b, s, _ = h.shape
    c_q, c_k, c_v, d_q, d_k, d_v = jnp.split(h @ mix['w_in'], split_points(ODD_SIZES), axis=-1)
    c_q = rms_norm(c_q.reshape(b, s, H_C, D_C), mix['g_qn'])
    c_k = rms_norm(c_k.reshape(b, s, G_C, D_C), mix['g_kn'])
    c_v = c_v.reshape(b, s, G_C, D_C)
    d_q = d_q.reshape(b, s, H_D, D_D)
    d_k = d_k.reshape(b, s, H_D, D_D)
    d_v = d_v.reshape(b, s, H_D, D_D)
    if cache is None:
        new_state = (c_k, c_v, d_k, d_v)
        o_c = sweep_query_blocks(lambda q: gqa_attention(q, c_k, c_v), c_q)
        o_d = sweep_query_blocks(lambda q: mha_attention(q, d_k, d_v, D_D ** -0.5), d_q)
    else:
        ck_c, cv_c, ck_d, cv_d = cache
        ang = grid_angles(s, D_C)
        k_c = jnp.concatenate([apply_rope(c_k, ang), ck_c], axis=1)
        v_c = jnp.concatenate([c_v, cv_c], axis=1)
        o_c = sweep_query_blocks(lambda q: gqa_attention(q, k_c, v_c), apply_rope(c_q, ang))
        o_d = neighbourhood_attention(d_q, d_k, d_v, ck_d, cv_d, mix['rpb'])
        new_state = None
    o = jnp.concatenate([o_c.reshape(b, s, -1), o_d.reshape(b, s, -1)], axis=-1)
    return o @ mix['w_out'], new_state


def peer_ffn(h, w_q, sub_k1, sub_k2, u_tab, v_tab):
    b, s, dm = h.shape
    n_tok = b * s
    xt = h.reshape(n_tok, dm)
    q = (xt @ w_q).reshape(n_tok, PEER_HEADS, PEER_DKEY)
    half = PEER_DKEY // 2
    s1 = jnp.einsum('thd,nd->thn', q[..., :half], sub_k1).astype(jnp.float32)
    s2 = jnp.einsum('thd,nd->thn', q[..., half:], sub_k2).astype(jnp.float32)
    v1, i1 = lax.top_k(s1, PEER_TOPK)
    v2, i2 = lax.top_k(s2, PEER_TOPK)
    cand_s = (v1[..., :, None] + v2[..., None, :]).reshape(n_tok, PEER_HEADS, PEER_TOPK * PEER_TOPK)
    cand_i = (i1[..., :, None] * PEER_KEYS + i2[..., None, :]).reshape(n_tok, PEER_HEADS, PEER_TOPK * PEER_TOPK)
    top_s, pos = lax.top_k(cand_s, PEER_TOPK)
    idx = jnp.take_along_axis(cand_i, pos, axis=-1)
    g = jax.nn.softmax(top_s, axis=-1).astype(h.dtype)

    def chunk(args):
        xc, ic, gc = args
        u = jnp.take(u_tab, ic, axis=0)
        act = jax.nn.gelu(jnp.einsum('chkd,cd->chk', u, xc), approximate=False)
        return jnp.einsum('chk,chkd->cd', gc * act, jnp.take(v_tab, ic, axis=0))

    nch = n_tok // PEER_CHUNK
    out = lax.map(chunk, (xt.reshape(nch, PEER_CHUNK, dm),
                          idx.reshape(nch, PEER_CHUNK, PEER_HEADS, PEER_TOPK),
                          g.reshape(nch, PEER_CHUNK, PEER_HEADS, PEER_TOPK)))
    return out.reshape(b, s, dm)


def trunk_layer(x, cvec, l, mix, norm, peer, cache):
    w_m, b_m, g1, b1, g2, b2 = norm
    sh1, sc1, gt1, sh2, sc2, gt2 = modulation(cvec, w_m, b_m)
    h = x * (1.0 + sc1) + sh1
    if l % 2 == 0:
        y, new_state = even_mixer(h, mix, 0.8 - 0.6 * math.exp(-0.3 * l), cache)
    else:
        y, new_state = odd_mixer(h, mix, cache)
    x = layer_norm(DN_ALPHA * x + gt1 * y, g1, b1)
    h = x * (1.0 + sc2) + sh2
    x = layer_norm(DN_ALPHA * x + gt2 * peer_ffn(h, *peer), g2, b2)
    return x, new_state


def setup_inputs(seed: int = 0) -> dict:
    key = jax.random.key(seed)
    ks = iter(jax.random.split(key, 48))
    d = D_MODEL

    def nrm(shape, scale):
        return jax.random.normal(next(ks), shape, jnp.float32) * scale

    def gain(shape):
        return 1.0 + nrm(shape, 0.02)

    return {
        'x_prompt': nrm((BATCH, SEQ, d), 1.0),
        'x_sample': nrm((DEC_BATCH, DEC_SEQ, d), 1.0),
        'cache_a_k': nrm((DEC_BATCH, N_EVEN, PAST_LEN, H_A, 2, D_A), 1.0),
        'cache_a_v': nrm((DEC_BATCH, N_EVEN, PAST_LEN, H_A, 2 * D_A), 1.0),
        'cache_b_ckv': nrm((DEC_BATCH, N_EVEN, PAST_LEN, B_KVRANK), 1.0),
        'cache_b_kr': nrm((DEC_BATCH, N_EVEN, PAST_LEN, B_ROPE), 1.0),
        'cache_c_k': nrm((DEC_BATCH, N_ODD, PAST_LEN, G_C, D_C), 1.0),
        'cache_c_v': nrm((DEC_BATCH, N_ODD, PAST_LEN, G_C, D_C), 1.0),
        'cache_d_k': nrm((DEC_BATCH, N_ODD, PAST_LEN, H_D, D_D), 1.0),
        'cache_d_v': nrm((DEC_BATCH, N_ODD, PAST_LEN, H_D, D_D), 1.0),
        'c': nrm((DEC_BATCH, d), 1.0),
        'c_ctx': nrm((d,), 1.0),
        'w_mod': nrm((DEPTH, d, 6 * d), 0.5 * d ** -0.5),
        'b_mod': nrm((DEPTH, 6 * d), 0.02),
        'ln1_g': gain((DEPTH, d)),
        'ln1_b': nrm((DEPTH, d), 0.02),
        'ln2_g': gain((DEPTH, d)),
        'ln2_b': nrm((DEPTH, d), 0.02),
        'ev_w_in': nrm((N_EVEN, d, EVEN_IN), d ** -0.5),
        'ev_lam_q1': nrm((N_EVEN, D_A), 0.1),
        'ev_lam_k1': nrm((N_EVEN, D_A), 0.1),
        'ev_lam_q2': nrm((N_EVEN, D_A), 0.1),
        'ev_lam_k2': nrm((N_EVEN, D_A), 0.1),
        'ev_g_sub': gain((N_EVEN, 2 * D_A)),
        'ev_g_cq': gain((N_EVEN, B_QRANK)),
        'ev_w_uq': nrm((N_EVEN, B_QRANK, H_B * (B_NOPE + B_ROPE)), B_QRANK ** -0.5),
        'ev_g_ckv': gain((N_EVEN, B_KVRANK)),
        'ev_w_ukv': nrm((N_EVEN, B_KVRANK, H_B * (B_NOPE + B_V)), B_KVRANK ** -0.5),
        'ev_w_out': nrm((N_EVEN, EVEN_MIX, d), DN_BETA * EVEN_MIX ** -0.5),
        'od_w_in': nrm((N_ODD, d, ODD_IN), d ** -0.5),
        'od_g_qn': gain((N_ODD, D_C)),
        'od_g_kn': gain((N_ODD, D_C)),
        'od_rpb': nrm((N_ODD, H_D, 2 * NA_ROWS - 1, 2 * NA_COLS - 1), 0.1),
        'od_w_out': nrm((N_ODD, ODD_MIX, d), DN_BETA * ODD_MIX ** -0.5),
        'pk_w_q': nrm((DEPTH, d, PEER_HEADS * PEER_DKEY), d ** -0.5),
        'pk_k1': nrm((DEPTH, PEER_KEYS, PEER_DKEY // 2), (PEER_DKEY // 2) ** -0.5),
        'pk_k2': nrm((DEPTH, PEER_KEYS, PEER_DKEY // 2), (PEER_DKEY // 2) ** -0.5),
        'pk_u': nrm((DEPTH, PEER_EXPERTS, d), d ** -0.5),
        'pk_v': nrm((DEPTH, PEER_EXPERTS, d), DN_BETA),
    }


def reference(x_prompt, x_sample, cache_a_k, cache_a_v, cache_b_ckv, cache_b_kr,
              cache_c_k, cache_c_v, cache_d_k, cache_d_v, c, c_ctx,
              w_mod, b_mod, ln1_g, ln1_b, ln2_g, ln2_b,
              ev_w_in, ev_lam_q1, ev_lam_k1, ev_lam_q2, ev_lam_k2, ev_g_sub,
              ev_g_cq, ev_w_uq, ev_g_ckv, ev_w_ukv, ev_w_out,
              od_w_in, od_g_qn, od_g_kn, od_rpb, od_w_out,
              pk_w_q, pk_k1, pk_k2, pk_u, pk_v):
    def params_for(l):
        j = l // 2
        if l % 2 == 0:
            mix = {'w_in': ev_w_in[j], 'lam_q1': ev_lam_q1[j], 'lam_k1': ev_lam_k1[j],
                   'lam_q2': ev_lam_q2[j], 'lam_k2': ev_lam_k2[j], 'g_sub': ev_g_sub[j],
                   'g_cq': ev_g_cq[j], 'w_uq': ev_w_uq[j], 'g_ckv': ev_g_ckv[j],
                   'w_ukv': ev_w_ukv[j], 'w_out': ev_w_out[j]}
        else:
            mix = {'w_in': od_w_in[j], 'g_qn': od_g_qn[j], 'g_kn': od_g_kn[j],
                   'rpb': od_rpb[j], 'w_out': od_w_out[j]}
        norm = (w_mod[l], b_mod[l], ln1_g[l], ln1_b[l], ln2_g[l], ln2_b[l])
        peer = (pk_w_q[l], pk_k1[l], pk_k2[l], pk_u[l], pk_v[l])
        return mix, norm, peer

    y = x_prompt
    even_states, odd_states = [], []
    for l in range(DEPTH):
        mix, norm, peer = params_for(l)
        y, st = trunk_layer(y, c_ctx[None, :], l, mix, norm, peer, None)
        (even_states if l % 2 == 0 else odd_states).append(st)

    z = x_sample
    for l in range(DEPTH):
        j = l // 2
        if l % 2 == 0:
            cache = (cache_a_k[:, j], cache_a_v[:, j], cache_b_ckv[:, j], cache_b_kr[:, j])
        else:
            cache = (cache_c_k[:, j], cache_c_v[:, j], cache_d_k[:, j], cache_d_v[:, j])
        mix, norm, peer = params_for(l)
        z, _ = trunk_layer(z, c, l, mix, norm, peer, cache)

    state_a_k = jnp.stack([st[0] for st in even_states], axis=1)
    state_a_v = jnp.stack([st[1] for st in even_states], axis=1)
    state_b_ckv = jnp.stack([st[2] for st in even_states], axis=1)
    state_b_kr = jnp.stack([st[3] for st in even_states], axis=1)
    state_c_k = jnp.stack([st[0] for st in odd_states], axis=1)
    state_c_v = jnp.stack([st[1] for st in odd_states], axis=1)
    state_d_k = jnp.stack([st[2] for st in odd_states], axis=1)
    state_d_v = jnp.stack([st[3] for st in odd_states], axis=1)
    return (y, z, state_a_k, state_a_v, state_b_ckv, state_b_kr, state_c_k, state_c_v, state_d_k, state_d_v)
```

```python
import functools
import math

import jax
import jax.numpy as jnp
from jax import lax
from jax.experimental import pallas as pl
from jax.experimental.pallas import tpu as pltpu

F32 = jnp.float32
BF16 = jnp.bfloat16

D = 1024
N_CTX_B = 32
CTX_S = 256
N_LAT_B = 4
LAT_S = 1024
PAST = 256
T_CTX = N_CTX_B * CTX_S
T_LAT = N_LAT_B * LAT_S
T_ALL = T_CTX + T_LAT
DEPTH = 2
GRID_W = 64
GRID_ROWS = LAT_S // GRID_W
THETA = 10000.0
EPS = 1e-6
ALPHA = (2 * DEPTH) ** 0.25

H_A, D_A = 4, 64
H_B, B_NOPE, B_ROPE, B_V, B_QRANK, B_KVRANK = 8, 64, 32, 64, 256, 128
H_C, G_C, D_C = 8, 2, 64
H_D, D_D = 8, 64
NA_ROWS, NA_COLS = 8, 16
P_HEADS, P_KEYS, P_DKEY, P_TOPK = 8, 128, 256, 16
P_EXPERTS = P_KEYS * P_KEYS

EVEN_IN_PAD = 2048
ODD_IN = 2304

TM = 256
N_TILES = T_ALL // TM
N_CTX_TILES = T_CTX // TM
TILES_PER_LAT = LAT_S // TM
TQ = 256
PEER_TT = 512
PEER_EB = 1024
PEER_NSEL = P_TOPK + 1
VMEM_LIMIT = 56 * 1024 * 1024

NEG_INF = float("-inf")


def _dot(a, b):
    return jnp.dot(a, b, preferred_element_type=F32)


def _dot_nt(a, b):
    return lax.dot_general(a, b, (((1,), (1,)), ((), ())), preferred_element_type=F32)


def _softmax(s):
    m = jnp.max(s, axis=-1, keepdims=True)
    e = jnp.exp(s - m)
    return e / jnp.sum(e, axis=-1, keepdims=True)


def _layer_norm(x, g, b):
    mu = jnp.mean(x, axis=-1, keepdims=True)
    xc = x - mu
    var = jnp.mean(xc * xc, axis=-1, keepdims=True)
    return xc * lax.rsqrt(var + EPS) * g + b


def _rms_norm(x, g):
    return x * lax.rsqrt(jnp.mean(x * x, axis=-1, keepdims=True) + EPS) * g


def _chunk_rms_norm(x, g, ones_bd, chunk):
    x2 = x * x
    hi = x2.astype(BF16)
    lo = (x2 - hi.astype(F32)).astype(BF16)
    ms = (_dot(hi, ones_bd) + _dot(lo, ones_bd)) * (1.0 / chunk)
    return x * lax.rsqrt(ms + EPS) * g


def _rope(x, cos_t, sin_t, chunk):
    n = x.shape[-1]
    half = chunk // 2
    lane = lax.broadcasted_iota(jnp.int32, x.shape, x.ndim - 1)
    swapped = jnp.where((lane % chunk) < half,
                        pltpu.roll(x, n - half, x.ndim - 1),
                        pltpu.roll(x, half, x.ndim - 1))
    return x * cos_t + swapped * sin_t


def _params(sem):
    return pltpu.CompilerParams(dimension_semantics=sem, vmem_limit_bytes=VMEM_LIMIT)


def _full(shape):
    zeros = (0,) * len(shape)
    return pl.BlockSpec(shape, lambda *_: zeros)


def _mod_row(i):
    return jnp.where(i < N_CTX_TILES, 0, 1 + (i - N_CTX_TILES) // TILES_PER_LAT)


def _rope_blk(i):
    return jnp.where(i < N_CTX_TILES, 0, 1 + (i - N_CTX_TILES) % TILES_PER_LAT)


def _mod_kernel(c_ref, w_ref, b_ref, o_ref):
    c = c_ref[...]
    a = c * (1.0 / (1.0 + jnp.exp(-c)))
    o_ref[0] = _dot(a.astype(BF16), w_ref[0].astype(BF16)) + b_ref[0]


def _modulation(cvecs, w_mod, b_mod):
    nb = 6
    bn = 6 * D // nb
    return pl.pallas_call(
        _mod_kernel,
        grid=(DEPTH, nb),
        in_specs=[pl.BlockSpec((8, D), lambda l, j: (0, 0)),
                  pl.BlockSpec((1, D, bn), lambda l, j: (l, 0, j)),
                  pl.BlockSpec((1, 1, bn), lambda l, j: (l, 0, j))],
        out_specs=pl.BlockSpec((1, 8, bn), lambda l, j: (l, 0, j)),
        out_shape=jax.ShapeDtypeStruct((DEPTH, 8, 6 * D), F32),
        compiler_params=_params(("arbitrary", "arbitrary")),
        name="modulation",
    )(cvecs, w_mod, b_mod.reshape(DEPTH, 1, 6 * D))


def _proj_even_kernel(x_ref, mod_ref, rt_ref, win_ref, gcq_ref, wuq_ref, gckv_ref, wukv_ref,
                      qa_ref, ka_ref, va_ref, bq_ref, ckv_ref, kr_ref, kvb_ref):
    x = x_ref[...]
    mod = mod_ref[0, 0]
    h = x * (1.0 + mod[1:2]) + mod[0:1]
    p = _dot(h.astype(BF16), win_ref[...])
    rt = rt_ref[0]
    cos_a, sin_a = rt[:, 0:512], rt[:, 512:1024]
    cos_b, sin_b = rt[:, 1024:1280], rt[:, 1280:1536]
    qa_ref[...] = _rope(p[:, 0:512], cos_a, sin_a, D_A)
    ka_ref[...] = _rope(p[:, 512:1024], cos_a, sin_a, D_A)
    va_ref[...] = p[:, 1024:1536]
    cq = _rms_norm(p[:, 1536:1792], gcq_ref[...])
    bq = _dot(cq.astype(BF16), wuq_ref[...])
    bq_ref[:, 0:512] = bq[:, 0:512]
    bq_ref[:, 512:768] = _rope(bq[:, 512:768], cos_b, sin_b, B_ROPE)
    ckv = _rms_norm(p[:, 1792:1920], gckv_ref[...])
    ckv_ref[...] = ckv
    kvb_ref[...] = _dot(ckv.astype(BF16), wukv_ref[...])
    kr = _rope(p[:, 1920:2048], cos_b[:, 0:128], sin_b[:, 0:128], B_ROPE)
    kr_ref[...] = kr[:, 0:B_ROPE]


def _proj_even(x, mods, l, rt, w_in, g_cq, w_uq, g_ckv, w_ukv):
    tok = lambda w: pl.BlockSpec((TM, w), lambda i: (i, 0))
    widths = (512, 512, 512, 768, 128, B_ROPE, 1024)
    return pl.pallas_call(
        _proj_even_kernel,
        grid=(N_TILES,),
        in_specs=[tok(D),
                  pl.BlockSpec((1, 1, 6, D), lambda i: (l, _mod_row(i), 0, 0)),
                  pl.BlockSpec((1, TM, 1536), lambda i: (_rope_blk(i), 0, 0)),
                  _full((D, EVEN_IN_PAD)), _full((1, B_QRANK)), _full((B_QRANK, 768)),
                  _full((1, B_KVRANK)), _full((B_KVRANK, 1024))],
        out_specs=[tok(w) for w in widths],
        out_shape=[jax.ShapeDtypeStruct((T_ALL, w), F32) for w in widths],
        compiler_params=_params(("arbitrary",)),
        name="proj_even",
    )(x, mods, rt, w_in, g_cq, w_uq, g_ckv, w_ukv)


def _proj_odd_kernel(x_ref, mod_ref, rt_ref, win_ref, gq_ref, gk_ref, bd_ref,
                     cq_ref, ck_ref, cv_ref, dq_ref, dk_ref, dv_ref):
    x = x_ref[...]
    mod = mod_ref[0, 0]
    h = x * (1.0 + mod[1:2]) + mod[0:1]
    p = _dot(h.astype(BF16), win_ref[...])
    rt = rt_ref[0]
    cos_c, sin_c = rt[:, 0:512], rt[:, 512:1024]
    bd = bd_ref[...]
    cq = _chunk_rms_norm(p[:, 0:512], gq_ref[...], bd, D_C)
    cq_ref[...] = _rope(cq, cos_c, sin_c, D_C)
    ck = _chunk_rms_norm(p[:, 512:640], gk_ref[...], bd[0:128, 0:128], D_C)
    ck_ref[...] = _rope(ck, cos_c[:, 0:128], sin_c[:, 0:128], D_C)
    cv_ref[...] = p[:, 640:768]
    dq_ref[...] = p[:, 768:1280]
    dk_ref[...] = p[:, 1280:1792]
    dv_ref[...] = p[:, 1792:2304]


def _proj_odd(x, mods, l, rt, w_in, g_qn, g_kn, ones_bd):
    tok = lambda w: pl.BlockSpec((TM, w), lambda i: (i, 0))
    widths = (512, 128, 128, 512, 512, 512)
    return pl.pallas_call(
        _proj_odd_kernel,
        grid=(N_TILES,),
        in_specs=[tok(D),
                  pl.BlockSpec((1, 1, 6, D), lambda i: (l, _mod_row(i), 0, 0)),
                  pl.BlockSpec((1, TM, 1024), lambda i: (_rope_blk(i), 0, 0)),
                  _full((D, ODD_IN)), _full((1, 512)), _full((1, 128)), _full((512, 512))],
        out_specs=[tok(w) for w in widths],
        out_shape=[jax.ShapeDtypeStruct((T_ALL, w), F32) for w in widths],
        compiler_params=_params(("arbitrary",)),
        name="proj_odd",
    )(x, mods, rt, w_in, g_qn, g_kn, ones_bd)


def _matmul_kernel(x_ref, w_ref, o_ref):
    o_ref[...] = _dot(x_ref[...].astype(BF16), w_ref[...])


def _matmul(x, w):
    m, k = x.shape
    n = w.shape[1]
    return pl.pallas_call(
        _matmul_kernel,
        grid=(m // TM,),
        in_specs=[pl.BlockSpec((TM, k), lambda i: (i, 0)), _full((k, n))],
        out_specs=pl.BlockSpec((TM, n), lambda i: (i, 0)),
        out_shape=jax.ShapeDtypeStruct((m, n), F32),
        compiler_params=_params(("arbitrary",)),
        name="cache_kv_up",
    )(x, w)


def _attn_even_kernel(lq1_ref, lk1_ref, lq2_ref, lk2_ref, gsub_ref,
                      qa_ref, bq_ref, ka_ref, va_ref, kvb_ref, kr_ref,
                      oa_ref, ob_ref, *, lam_init):
    lam = (jnp.exp(jnp.sum(lq1_ref[...] * lk1_ref[...], axis=-1, keepdims=True))
           - jnp.exp(jnp.sum(lq2_ref[...] * lk2_ref[...], axis=-1, keepdims=True)) + lam_init)
    gsub = gsub_ref[...]
    for hd in range(H_A):
        ps = []
        for m in range(2):
            c0 = hd * 2 * D_A + m * D_A
            q = qa_ref[:, c0:c0 + D_A].astype(BF16)
            k = ka_ref[0, :, c0:c0 + D_A].astype(BF16)
            ps.append(_softmax(_dot_nt(q, k) * (D_A ** -0.5)))
        p = ps[0] - lam * ps[1]
        v = va_ref[0, :, hd * 2 * D_A:(hd + 1) * 2 * D_A].astype(BF16)
        o = _dot(p.astype(BF16), v)
        oa_ref[:, hd * 2 * D_A:(hd + 1) * 2 * D_A] = _rms_norm(o, gsub) * (1.0 - lam_init)
    kr = kr_ref[0].astype(BF16)
    for hd in range(H_B):
        qn = bq_ref[:, hd * B_NOPE:(hd + 1) * B_NOPE].astype(BF16)
        qr = bq_ref[:, 512 + hd * B_ROPE:512 + (hd + 1) * B_ROPE].astype(BF16)
        kn = kvb_ref[0, :, hd * B_NOPE:(hd + 1) * B_NOPE].astype(BF16)
        s = _dot_nt(qn, kn) + _dot_nt(qr, kr)
        p = _softmax(s * ((B_NOPE + B_ROPE) ** -0.5))
        v = kvb_ref[0, :, 512 + hd * B_V:512 + (hd + 1) * B_V].astype(BF16)
        ob_ref[:, hd * B_V:(hd + 1) * B_V] = _dot(p.astype(BF16), v)


def _attn_even(lam_vecs, g_sub, lam_init, qa, bq, ka, va, kvb, kr, nb, nq):
    nk = ka.shape[1]
    nqb = nq // TQ
    qspec = lambda w: pl.BlockSpec((TQ, w), lambda b, j: (b * nqb + j, 0))
    kspec = lambda w: pl.BlockSpec((1, nk, w), lambda b, j: (b, 0, 0))
    return pl.pallas_call(
        functools.partial(_attn_even_kernel, lam_init=lam_init),
        grid=(nb, nqb),
        in_specs=[_full((1, D_A))] * 4 + [_full((1, 2 * D_A)),
                  qspec(512), qspec(768), kspec(512), kspec(512), kspec(1024), kspec(B_ROPE)],
        out_specs=[qspec(512), qspec(512)],
        out_shape=[jax.ShapeDtypeStruct((nb * nq, 512), F32)] * 2,
        compiler_params=_params(("arbitrary", "arbitrary")),
        name="attn_even",
    )(*lam_vecs, g_sub, qa, bq, ka, va, kvb, kr)


def _gqa_kernel(cq_ref, ck_ref, cv_ref, o_ref):
    for hd in range(H_C):
        g = hd // (H_C // G_C)
        q = cq_ref[:, hd * D_C:(hd + 1) * D_C].astype(BF16)
        k = ck_ref[0, :, g * D_C:(g + 1) * D_C].astype(BF16)
        p = _softmax(_dot_nt(q, k) * (D_C ** -0.5))
        v = cv_ref[0, :, g * D_C:(g + 1) * D_C].astype(BF16)
        o_ref[:, hd * D_C:(hd + 1) * D_C] = _dot(p.astype(BF16), v)


def _gqa(cq, ck, cv, nb, nq):
    nk = ck.shape[1]
    nqb = nq // TQ
    return pl.pallas_call(
        _gqa_kernel,
        grid=(nb, nqb),
        in_specs=[pl.BlockSpec((TQ, 512), lambda b, j: (b * nqb + j, 0)),
                  pl.BlockSpec((1, nk, 128), lambda b, j: (b, 0, 0)),
                  pl.BlockSpec((1, nk, 128), lambda b, j: (b, 0, 0))],
        out_specs=pl.BlockSpec((TQ, 512), lambda b, j: (b * nqb + j, 0)),
        out_shape=jax.ShapeDtypeStruct((nb * nq, 512), F32),
        compiler_params=_params(("arbitrary", "arbitrary")),
        name="attn_gqa",
    )(cq, ck, cv)


def _mha_kernel(q_ref, k_ref, v_ref, o_ref):
    for hd in range(H_D):
        sl = slice(hd * D_D, (hd + 1) * D_D)
        q = q_ref[:, sl].astype(BF16)
        k = k_ref[0, :, sl].astype(BF16)
        p = _softmax(_dot_nt(q, k) * (D_D ** -0.5))
        o_ref[:, sl] = _dot(p.astype(BF16), v_ref[0, :, sl].astype(BF16))


def _mha(q, k, v, nb, nq):
    nk = k.shape[1]
    nqb = nq // TQ
    return pl.pallas_call(
        _mha_kernel,
        grid=(nb, nqb),
        in_specs=[pl.BlockSpec((TQ, 512), lambda b, j: (b * nqb + j, 0)),
                  pl.BlockSpec((1, nk, 512), lambda b, j: (b, 0, 0)),
                  pl.BlockSpec((1, nk, 512), lambda b, j: (b, 0, 0))],
        out_specs=pl.BlockSpec((TQ, 512), lambda b, j: (b * nqb + j, 0)),
        out_shape=jax.ShapeDtypeStruct((nb * nq, 512), F32),
        compiler_params=_params(("arbitrary", "arbitrary")),
        name="attn_mha",
    )(q, k, v)


def _na_bias_kernel(rpb_ref, o_ref):
    hd = pl.program_id(0)
    qc = lax.broadcasted_iota(jnp.int32, (GRID_W, GRID_W), 0)
    kc = lax.broadcasted_iota(jnp.int32, (GRID_W, GRID_W), 1)
    dc = jnp.clip(kc - qc, -(NA_COLS - 1), NA_COLS - 1) + (NA_COLS - 1)
    c0 = jnp.clip(qc - NA_COLS // 2, 0, GRID_W - NA_COLS)
    n_dc = 2 * NA_COLS - 1
    n_dr = 2 * NA_ROWS - 1
    for dr in range(n_dr):
        acc = jnp.zeros((GRID_W, GRID_W), F32)
        for j in range(n_dc):
            acc = jnp.where(dc == j, rpb_ref[(hd * n_dr + dr) * n_dc + j], acc)
        o_ref[0, dr] = jnp.where(kc >= c0, jnp.where(kc < c0 + NA_COLS, acc, NEG_INF), NEG_INF)


def _na_bias(rpb):
    n_dr = 2 * NA_ROWS - 1
    return pl.pallas_call(
        _na_bias_kernel,
        grid=(H_D,),
        in_specs=[pl.BlockSpec(memory_space=pltpu.SMEM)],
        out_specs=pl.BlockSpec((1, n_dr, GRID_W, GRID_W), lambda h: (h, 0, 0, 0)),
        out_shape=jax.ShapeDtypeStruct((H_D, n_dr, GRID_W, GRID_W), F32),
        compiler_params=_params(("arbitrary",)),
        name="na_bias",
    )(rpb.reshape(-1))


def _na_window_start(r):
    return jnp.clip(r - NA_ROWS // 2, 0, GRID_ROWS - NA_ROWS)


def _na_kernel(q_ref, k_ref, v_ref, ck_ref, cv_ref, bias_ref, o_ref):
    r = pl.program_id(1)
    start = pl.multiple_of(_na_window_start(r) * GRID_W, GRID_W)
    n_loc = NA_ROWS * GRID_W
    scale = D_D ** -0.5
    for hd in range(H_D):
        sl = slice(hd * D_D, (hd + 1) * D_D)
        q = q_ref[:, sl].astype(BF16)
        kw = k_ref[0, pl.ds(start, n_loc), sl].astype(BF16)
        vw = v_ref[0, pl.ds(start, n_loc), sl].astype(BF16)
        s_loc = _dot_nt(q, kw) * scale + bias_ref[0, hd]
        s_ctx = _dot_nt(q, ck_ref[0, :, sl].astype(BF16)) * scale
        m = jnp.maximum(jnp.max(s_loc, axis=-1, keepdims=True), jnp.max(s_ctx, axis=-1, keepdims=True))
        e_loc = jnp.exp(s_loc - m)
        e_ctx = jnp.exp(s_ctx - m)
        den = jnp.sum(e_loc, axis=-1, keepdims=True) + jnp.sum(e_ctx, axis=-1, keepdims=True)
        o = (_dot((e_loc / den).astype(BF16), vw)
             + _dot((e_ctx / den).astype(BF16), cv_ref[0, :, sl].astype(BF16)))
        o_ref[:, sl] = o


def _neighbourhood(q, k, v, ck, cv, bias_win):
    n_loc = NA_ROWS * GRID_W
    return pl.pallas_call(
        _na_kernel,
        grid=(N_LAT_B, GRID_ROWS),
        in_specs=[pl.BlockSpec((GRID_W, 512), lambda b, r: (b * GRID_ROWS + r, 0)),
                  pl.BlockSpec((1, LAT_S, 512), lambda b, r: (b, 0, 0)),
                  pl.BlockSpec((1, LAT_S, 512), lambda b, r: (b, 0, 0)),
                  pl.BlockSpec((1, PAST, 512), lambda b, r: (b, 0, 0)),
                  pl.BlockSpec((1, PAST, 512), lambda b, r: (b, 0, 0)),
                  pl.BlockSpec((1, H_D, GRID_W, n_loc),
                               lambda b, r: (_na_window_start(r) - r + NA_ROWS - 1, 0, 0, 0))],
        out_specs=pl.BlockSpec((GRID_W, 512), lambda b, r: (b * GRID_ROWS + r, 0)),
        out_shape=jax.ShapeDtypeStruct((T_LAT, 512), F32),
        compiler_params=_params(("arbitrary", "arbitrary")),
        name="attn_neighbourhood",
    )(q, k, v, ck, cv, bias_win)


def _post_kernel(x_ref, mod_ref, o1_ref, o2_ref, wo1_ref, wo2_ref, g1_ref, b1_ref,
                 wq_ref, k1_ref, k2_ref, x1_ref, h2_ref, s1_ref, s2_ref):
    mod = mod_ref[0, 0]
    y = _dot(o1_ref[...].astype(BF16), wo1_ref[...]) + _dot(o2_ref[...].astype(BF16), wo2_ref[...])
    x1 = _layer_norm(ALPHA * x_ref[...] + mod[2:3] * y, g1_ref[...], b1_ref[...])
    x1_ref[...] = x1
    h2 = (x1 * (1.0 + mod[4:5]) + mod[3:4]).astype(BF16)
    h2_ref[...] = h2
    q = _dot(h2, wq_ref[...]).astype(BF16)
    half = P_DKEY // 2
    k1 = k1_ref[...]
    k2 = k2_ref[...]
    for hd in range(P_HEADS):
        s1_ref[hd] = _dot_nt(k1, q[:, hd * P_DKEY:hd * P_DKEY + half])
        s2_ref[hd] = _dot_nt(k2, q[:, hd * P_DKEY + half:(hd + 1) * P_DKEY])


def _post(x, mods, l, o1, o2, wo1, wo2, g1, b1, wq, k1, k2):
    tok = lambda w: pl.BlockSpec((TM, w), lambda i: (i, 0))
    sspec = pl.BlockSpec((P_HEADS, P_KEYS, TM), lambda i: (0, 0, i))
    return pl.pallas_call(
        _post_kernel,
        grid=(N_TILES,),
        in_specs=[tok(D),
                  pl.BlockSpec((1, 1, 6, D), lambda i: (l, _mod_row(i), 0, 0)),
                  tok(512), tok(512), _full((512, D)), _full((512, D)), _full((1, D)), _full((1, D)),
                  _full((D, P_HEADS * P_DKEY)), _full((P_KEYS, P_DKEY // 2)), _full((P_KEYS, P_DKEY // 2))],
        out_specs=[tok(D), tok(D), sspec, sspec],
        out_shape=[jax.ShapeDtypeStruct((T_ALL, D), F32),
                   jax.ShapeDtypeStruct((T_ALL, D), BF16),
                   jax.ShapeDtypeStruct((P_HEADS, P_KEYS, T_ALL), F32),
                   jax.ShapeDtypeStruct((P_HEADS, P_KEYS, T_ALL), F32)],
        compiler_params=_params(("arbitrary",)),
        name="post_mixer",
    )(x, mods, o1, o2, wo1, wo2, g1, b1, wq, k1, k2)


def _top_values(x, n):
    vals = []
    for _ in range(n):
        m = jnp.max(x, axis=0, keepdims=True)
        vals.append(m)
        x = jnp.where(x == m, NEG_INF, x)
    return vals


PEER_PAIRS = [(a, b) for a in range(PEER_NSEL) for b in range(PEER_NSEL) if (a + 1) * (b + 1) <= PEER_NSEL]
PEER_NCAND = -(-len(PEER_PAIRS) // 8) * 8


def _peer_select(s1_ref, s2_ref, th_scr, e1_scr, e2_scr, cand_scr):
    n_pad = PEER_NCAND - len(PEER_PAIRS)
    cand_scr[len(PEER_PAIRS):, :] = jnp.full((n_pad, cand_scr.shape[1]), NEG_INF, F32)

    def body(hd, carry):
        s1 = s1_ref[hd]
        s2 = s2_ref[hd]
        v1 = _top_values(s1, PEER_NSEL)
        v2 = _top_values(s2, PEER_NSEL)
        for k, (a, b) in enumerate(PEER_PAIRS):
            cand_scr[k:k + 1, :] = v1[a] + v2[b]
        cs = _top_values(cand_scr[...], PEER_NSEL)
        tau = 0.5 * (cs[P_TOPK - 1] + cs[P_TOPK])
        z = jnp.exp(cs[0] - cs[0])
        for j in range(1, P_TOPK):
            z = z + jnp.exp(cs[j] - cs[0])
        th_scr[hd] = tau - s1
        e1_scr[hd] = jnp.exp(s1 - v1[0])
        e2_scr[hd] = jnp.exp(s2 - v2[0]) / z
        return carry

    lax.fori_loop(0, P_HEADS, body, 0)


def _gelu(a):
    return 0.5 * a * (1.0 + lax.erf(a * (2.0 ** -0.5)))


PEER_LW = 256


def _peer_kernel(h2_ref, s1_ref, s2_ref, u_ref, vt_ref, x1_ref, mod_ref, g2_ref, b2_ref,
                 o_ref, th_scr, e1_scr, e2_scr, cand_scr, act_scr, c_scr, acc_scr):
    eb = pl.program_id(1)
    rows_per_step = PEER_EB // P_KEYS

    @pl.when(eb == 0)
    def _():
        _peer_select(s1_ref, s2_ref, th_scr, e1_scr, e2_scr, cand_scr)
        acc_scr[...] = jnp.zeros_like(acc_scr)

    act_scr[...] = _dot_nt(u_ref[...], h2_ref[...])

    def body(j, carry):
        i1 = eb * rows_per_step + j
        row0 = pl.multiple_of(j * P_KEYS, P_KEYS)
        for lc in range(PEER_TT // PEER_LW):
            ls = slice(lc * PEER_LW, (lc + 1) * PEER_LW)
            w = jnp.zeros((P_KEYS, PEER_LW), F32)
            for hd in range(P_HEADS):
                th = th_scr[hd, pl.ds(i1, 1), ls]
                e1 = e1_scr[hd, pl.ds(i1, 1), ls]
                w = w + jnp.where(s2_ref[hd, :, ls] >= th, e2_scr[hd, :, ls], 0.0) * e1
            a = act_scr[pl.ds(row0, P_KEYS), ls]
            c_scr[pl.ds(row0, P_KEYS), ls] = (w * _gelu(a)).astype(BF16)
        return carry

    lax.fori_loop(0, rows_per_step, body, 0)
    acc_scr[...] += _dot(vt_ref[...], c_scr[...])

    @pl.when(eb == pl.num_programs(1) - 1)
    def _():
        mod = mod_ref[0, 0]
        peer = acc_scr[...].T
        o_ref[...] = _layer_norm(ALPHA * x1_ref[...] + mod[5:6] * peer, g2_ref[...], b2_ref[...])


def _peer(h2, s1, s2, u_bf, vt_bf, x1, mods, l, g2, b2):
    tiles_ctx = T_CTX // PEER_TT
    tiles_per_lat = LAT_S // PEER_TT
    mod_row = lambda i: jnp.where(i < tiles_ctx, 0, 1 + (i - tiles_ctx) // tiles_per_lat)
    tok = lambda w: pl.BlockSpec((PEER_TT, w), lambda i, e: (i, 0))
    sspec = pl.BlockSpec((P_HEADS, P_KEYS, PEER_TT), lambda i, e: (0, 0, i))
    sel = pltpu.VMEM((P_HEADS, P_KEYS, PEER_TT), F32)
    return pl.pallas_call(
        _peer_kernel,
        grid=(T_ALL // PEER_TT, P_EXPERTS // PEER_EB),
        in_specs=[tok(D), sspec, sspec,
                  pl.BlockSpec((PEER_EB, D), lambda i, e: (e, 0)),
                  pl.BlockSpec((D, PEER_EB), lambda i, e: (0, e)),
                  tok(D),
                  pl.BlockSpec((1, 1, 6, D), lambda i, e: (l, mod_row(i), 0, 0)),
                  pl.BlockSpec((1, D), lambda i, e: (0, 0)),
                  pl.BlockSpec((1, D), lambda i, e: (0, 0))],
        out_specs=tok(D),
        out_shape=jax.ShapeDtypeStruct((T_ALL, D), F32),
        scratch_shapes=[sel, sel, sel,
                        pltpu.VMEM((PEER_NCAND, PEER_TT), F32),
                        pltpu.VMEM((PEER_EB, PEER_TT), F32),
                        pltpu.VMEM((PEER_EB, PEER_TT), BF16),
                        pltpu.VMEM((D, PEER_TT), F32)],
        compiler_params=_params(("arbitrary", "arbitrary")),
        name="peer",
    )(h2, s1, s2, u_bf, vt_bf, x1, mods, g2, b2)


def _grid_angles(rot_dim):
    t = jnp.arange(LAT_S)
    row = (t // GRID_W).astype(F32)
    col = (t % GRID_W).astype(F32)
    n_freq = rot_dim // 4
    inv = THETA ** (-jnp.arange(n_freq, dtype=F32) / n_freq)
    return jnp.concatenate([row[:, None] * inv, col[:, None] * inv], axis=-1)


def _rope_tables(rot_dim, width):
    ang = _grid_angles(rot_dim)
    cos = jnp.cos(ang)
    sin = jnp.sin(ang)
    cos_t = jnp.tile(jnp.concatenate([cos, cos], axis=-1), (1, width // rot_dim))
    sin_t = jnp.tile(jnp.concatenate([-sin, sin], axis=-1), (1, width // rot_dim))
    ident_c = jnp.ones((1, TM, width), F32)
    ident_s = jnp.zeros((1, TM, width), F32)
    cos_t = jnp.concatenate([ident_c, cos_t.reshape(TILES_PER_LAT, TM, width)], axis=0)
    sin_t = jnp.concatenate([ident_s, sin_t.reshape(TILES_PER_LAT, TM, width)], axis=0)
    return cos_t, sin_t


def _lat_keys(own, cache):
    w = own.shape[-1]
    return jnp.concatenate([own.reshape(N_LAT_B, LAT_S, w), cache.reshape(N_LAT_B, PAST, w)], axis=1)


def kernel(x_prompt, x_sample, cache_a_k, cache_a_v, cache_b_ckv, cache_b_kr, cache_c_k, cache_c_v, cache_d_k, cache_d_v, c, c_ctx, w_mod, b_mod, ln1_g, ln1_b, ln2_g, ln2_b, ev_w_in, ev_lam_q1, ev_lam_k1, ev_lam_q2, ev_lam_k2, ev_g_sub, ev_g_cq, ev_w_uq, ev_g_ckv, ev_w_ukv, ev_w_out, od_w_in, od_g_qn, od_g_kn, od_rpb, od_w_out, pk_w_q, pk_k1, pk_k2, pk_u, pk_v):
    x = jnp.concatenate([x_prompt.reshape(T_CTX, D), x_sample.reshape(T_LAT, D)], axis=0)
    cvecs = jnp.concatenate([c_ctx[None, :], c, jnp.zeros((8 - 1 - N_LAT_B, D), F32)], axis=0)
    mods = _modulation(cvecs, w_mod, b_mod).reshape(DEPTH, 8, 6, D)

    cos_a, sin_a = _rope_tables(D_A, 512)
    cos_b, sin_b = _rope_tables(B_ROPE, 256)
    rt_even = jnp.concatenate([cos_a, sin_a, cos_b, sin_b], axis=-1)
    rt_odd = jnp.concatenate(_rope_tables(D_C, 512), axis=-1)
    ones_bd = jnp.kron(jnp.eye(512 // D_C, dtype=F32), jnp.ones((D_C, D_C), F32)).astype(BF16)

    states = {}
    for l in range(DEPTH):
        j = l // 2
        if l % 2 == 0:
            lam_init = 0.8 - 0.6 * math.exp(-0.3 * l)
            w_in = jnp.pad(ev_w_in[j], ((0, 0), (0, EVEN_IN_PAD - ev_w_in.shape[-1]))).astype(BF16)
            wuq = ev_w_uq[j].reshape(B_QRANK, H_B, B_NOPE + B_ROPE)
            wuq = jnp.concatenate([wuq[:, :, :B_NOPE].reshape(B_QRANK, -1),
                                   wuq[:, :, B_NOPE:].reshape(B_QRANK, -1)], axis=1).astype(BF16)
            wukv = ev_w_ukv[j].reshape(B_KVRANK, H_B, B_NOPE + B_V)
            wukv = jnp.concatenate([wukv[:, :, :B_NOPE].reshape(B_KVRANK, -1),
                                    wukv[:, :, B_NOPE:].reshape(B_KVRANK, -1)], axis=1).astype(BF16)
            qa, ka, va, bq, ckv, kr, kvb = _proj_even(
                x, mods, l, rt_even, w_in, ev_g_cq[j][None, :], wuq, ev_g_ckv[j][None, :], wukv)
            states["a_k"], states["a_v"] = ka[:T_CTX], va[:T_CTX]
            states["b_ckv"], states["b_kr"] = ckv[:T_CTX], kr[:T_CTX]
            lam_vecs = [v[j][None, :] for v in (ev_lam_q1, ev_lam_k1, ev_lam_q2, ev_lam_k2)]
            g_sub = ev_g_sub[j][None, :]
            ctx3 = lambda a: a[:T_CTX].reshape(N_CTX_B, CTX_S, a.shape[-1])
            oa_c, ob_c = _attn_even(lam_vecs, g_sub, lam_init, qa[:T_CTX], bq[:T_CTX],
                                    ctx3(ka), ctx3(va), ctx3(kvb), ctx3(kr), N_CTX_B, CTX_S)
            cache_kvb = _matmul(cache_b_ckv[:, j].reshape(N_LAT_B * PAST, B_KVRANK), wukv)
            oa_l, ob_l = _attn_even(lam_vecs, g_sub, lam_init, qa[T_CTX:], bq[T_CTX:],
                                    _lat_keys(ka[T_CTX:], cache_a_k[:, j]),
                                    _lat_keys(va[T_CTX:], cache_a_v[:, j]),
                                    _lat_keys(kvb[T_CTX:], cache_kvb),
                                    _lat_keys(kr[T_CTX:], cache_b_kr[:, j]), N_LAT_B, LAT_S)
            o1 = jnp.concatenate([oa_c, oa_l], axis=0)
            o2 = jnp.concatenate([ob_c, ob_l], axis=0)
            w_out = ev_w_out[j]
        else:
            w_in = od_w_in[j].astype(BF16)
            g_qn = jnp.tile(od_g_qn[j], H_C)[None, :]
            g_kn = jnp.tile(od_g_kn[j], G_C)[None, :]
            cq, ck, cv, dq, dk, dv = _proj_odd(x, mods, l, rt_odd, w_in, g_qn, g_kn, ones_bd)
            states["c_k"], states["c_v"] = ck[:T_CTX], cv[:T_CTX]
            states["d_k"], states["d_v"] = dk[:T_CTX], dv[:T_CTX]
            ctx3 = lambda a: a[:T_CTX].reshape(N_CTX_B, CTX_S, a.shape[-1])
            oc_c = _gqa(cq[:T_CTX], ctx3(ck), ctx3(cv), N_CTX_B, CTX_S)
            od_c = _mha(dq[:T_CTX], ctx3(dk), ctx3(dv), N_CTX_B, CTX_S)
            oc_l = _gqa(cq[T_CTX:], _lat_keys(ck[T_CTX:], cache_c_k[:, j]),
                        _lat_keys(cv[T_CTX:], cache_c_v[:, j]), N_LAT_B, LAT_S)
            bias = _na_bias(od_rpb[j])
            offs = jnp.arange(NA_ROWS + 1)[:, None] + jnp.arange(NA_ROWS)[None, :]
            bias_win = jnp.transpose(bias[:, offs], (1, 0, 3, 2, 4)).reshape(
                NA_ROWS + 1, H_D, GRID_W, NA_ROWS * GRID_W)
            od_l = _neighbourhood(dq[T_CTX:], dk[T_CTX:].reshape(N_LAT_B, LAT_S, 512),
                                  dv[T_CTX:].reshape(N_LAT_B, LAT_S, 512),
                                  cache_d_k[:, j].reshape(N_LAT_B, PAST, 512),
                                  cache_d_v[:, j].reshape(N_LAT_B, PAST, 512), bias_win)
            o1 = jnp.concatenate([oc_c, oc_l], axis=0)
            o2 = jnp.concatenate([od_c, od_l], axis=0)
            w_out = od_w_out[j]

        x1, h2, s1, s2 = _post(x, mods, l, o1, o2, w_out[:512].astype(BF16), w_out[512:].astype(BF16),
                               ln1_g[l][None, :], ln1_b[l][None, :], pk_w_q[l].astype(BF16),
                               pk_k1[l].astype(BF16), pk_k2[l].astype(BF16))
        x = _peer(h2, s1, s2, pk_u[l].astype(BF16), pk_v[l].T.astype(BF16), x1, mods, l,
                  ln2_g[l][None, :], ln2_b[l][None, :])

    y = x[:T_CTX].reshape(N_CTX_B, CTX_S, D)
    z = x[T_CTX:].reshape(N_LAT_B, LAT_S, D)
    n_even = (DEPTH + 1) // 2
    n_odd = DEPTH // 2
    return (y, z,
            states["a_k"].reshape(N_CTX_B, n_even, CTX_S, H_A, 2, D_A),
            states["a_v"].reshape(N_CTX_B, n_even, CTX_S, H_A, 2 * D_A),
            states["b_ckv"].reshape(N_CTX_B, n_even, CTX_S, B_KVRANK),
            states["b_kr"].reshape(N_CTX_B, n_even, CTX_S, B_ROPE),
            states["c_k"].reshape(N_CTX_B, n_odd, CTX_S, G_C, D_C),
            states["c_v"].reshape(N_CTX_B, n_odd, CTX_S, G_C, D_C),
            states["d_k"].reshape(N_CTX_B, n_odd, CTX_S, H_D, D_D),
            states["d_v"].reshape(N_CTX_B, n_odd, CTX_S, H_D, D_D))
```

```python
import functools
import math

import jax
import jax.numpy as jnp
from jax import lax
from jax.experimental import pallas as pl
from jax.experimental.pallas import tpu as pltpu

F32 = jnp.float32
BF16 = jnp.bfloat16

D = 1024
N_CTX_B = 32
CTX_S = 256
N_LAT_B = 4
LAT_S = 1024
PAST = 256
T_CTX = N_CTX_B * CTX_S
T_LAT = N_LAT_B * LAT_S
T_ALL = T_CTX + T_LAT
DEPTH = 2
GRID_W = 64
GRID_ROWS = LAT_S // GRID_W
THETA = 10000.0
EPS = 1e-6
ALPHA = (2 * DEPTH) ** 0.25

H_A, D_A = 4, 64
H_B, B_NOPE, B_ROPE, B_V, B_QRANK, B_KVRANK = 8, 64, 32, 64, 256, 128
H_C, G_C, D_C = 8, 2, 64
H_D, D_D = 8, 64
NA_ROWS, NA_COLS = 8, 16
P_HEADS, P_KEYS, P_DKEY, P_TOPK = 8, 128, 256, 16
P_EXPERTS = P_KEYS * P_KEYS

EVEN_IN_PAD = 2048
ODD_IN = 2304

TM = 256
N_TILES = T_ALL // TM
N_CTX_TILES = T_CTX // TM
TILES_PER_LAT = LAT_S // TM
TQ = 256
PEER_TT = 512
PEER_EB = 1024
PEER_NSEL = P_TOPK + 1
VMEM_LIMIT = 56 * 1024 * 1024

NEG_INF = float("-inf")


def _dot(a, b):
    return jnp.dot(a, b, preferred_element_type=F32)


def _dot_nt(a, b):
    return lax.dot_general(a, b, (((1,), (1,)), ((), ())), preferred_element_type=F32)


def _softmax(s):
    m = jnp.max(s, axis=-1, keepdims=True)
    e = jnp.exp(s - m)
    return e / jnp.sum(e, axis=-1, keepdims=True)


def _layer_norm(x, g, b):
    mu = jnp.mean(x, axis=-1, keepdims=True)
    xc = x - mu
    var = jnp.mean(xc * xc, axis=-1, keepdims=True)
    return xc * lax.rsqrt(var + EPS) * g + b


def _rms_norm(x, g):
    return x * lax.rsqrt(jnp.mean(x * x, axis=-1, keepdims=True) + EPS) * g


def _chunk_rms_norm(x, g, ones_bd, chunk):
    x2 = x * x
    hi = x2.astype(BF16)
    lo = (x2 - hi.astype(F32)).astype(BF16)
    ms = (_dot(hi, ones_bd) + _dot(lo, ones_bd)) * (1.0 / chunk)
    return x * lax.rsqrt(ms + EPS) * g


def _rope(x, cos_t, sin_t, chunk):
    n = x.shape[-1]
    half = chunk // 2
    lane = lax.broadcasted_iota(jnp.int32, x.shape, x.ndim - 1)
    swapped = jnp.where((lane % chunk) < half,
                        pltpu.roll(x, n - half, x.ndim - 1),
                        pltpu.roll(x, half, x.ndim - 1))
    return x * cos_t + swapped * sin_t


def _params(sem):
    return pltpu.CompilerParams(dimension_semantics=sem, vmem_limit_bytes=VMEM_LIMIT)


def _full(shape):
    zeros = (0,) * len(shape)
    return pl.BlockSpec(shape, lambda *_: zeros)


def _mod_row(i):
    return jnp.where(i < N_CTX_TILES, 0, 1 + (i - N_CTX_TILES) // TILES_PER_LAT)


def _rope_blk(i):
    return jnp.where(i < N_CTX_TILES, 0, 1 + (i - N_CTX_TILES) % TILES_PER_LAT)


def _mod_kernel(c_ref, w_ref, b_ref, o_ref):
    c = c_ref[...]
    a = c * (1.0 / (1.0 + jnp.exp(-c)))
    o_ref[0] = _dot(a.astype(BF16), w_ref[0].astype(BF16)) + b_ref[0]


def _modulation(cvecs, w_mod, b_mod):
    nb = 6
    bn = 6 * D // nb
    return pl.pallas_call(
        _mod_kernel,
        grid=(DEPTH, nb),
        in_specs=[pl.BlockSpec((8, D), lambda l, j: (0, 0)),
                  pl.BlockSpec((1, D, bn), lambda l, j: (l, 0, j)),
                  pl.BlockSpec((1, 1, bn), lambda l, j: (l, 0, j))],
        out_specs=pl.BlockSpec((1, 8, bn), lambda l, j: (l, 0, j)),
        out_shape=jax.ShapeDtypeStruct((DEPTH, 8, 6 * D), F32),
        compiler_params=_params(("arbitrary", "arbitrary")),
        name="modulation",
    )(cvecs, w_mod, b_mod.reshape(DEPTH, 1, 6 * D))


def _proj_even_kernel(x_ref, mod_ref, rt_ref, win_ref, gcq_ref, wuq_ref, gckv_ref, wukv_ref,
                      qa_ref, ka_ref, va_ref, bq_ref, ckv_ref, kr_ref, kvb_ref):
    x = x_ref[...]
    mod = mod_ref[0, 0]
    h = x * (1.0 + mod[1:2]) + mod[0:1]
    p = _dot(h.astype(BF16), win_ref[...])
    rt = rt_ref[0]
    cos_a, sin_a = rt[:, 0:512], rt[:, 512:1024]
    cos_b, sin_b = rt[:, 1024:1280], rt[:, 1280:1536]
    qa_ref[...] = _rope(p[:, 0:512], cos_a, sin_a, D_A)
    ka_ref[...] = _rope(p[:, 512:1024], cos_a, sin_a, D_A)
    va_ref[...] = p[:, 1024:1536]
    cq = _rms_norm(p[:, 1536:1792], gcq_ref[...])
    bq = _dot(cq.astype(BF16), wuq_ref[...])
    bq_ref[:, 0:512] = bq[:, 0:512]
    bq_ref[:, 512:768] = _rope(bq[:, 512:768], cos_b, sin_b, B_ROPE)
    ckv = _rms_norm(p[:, 1792:1920], gckv_ref[...])
    ckv_ref[...] = ckv
    kvb_ref[...] = _dot(ckv.astype(BF16), wukv_ref[...])
    kr = _rope(p[:, 1920:2048], cos_b[:, 0:128], sin_b[:, 0:128], B_ROPE)
    kr_ref[...] = kr[:, 0:B_ROPE]


def _proj_even(x, mods, l, rt, w_in, g_cq, w_uq, g_ckv, w_ukv):
    tok = lambda w: pl.BlockSpec((TM, w), lambda i: (i, 0))
    widths = (512, 512, 512, 768, 128, B_ROPE, 1024)
    return pl.pallas_call(
        _proj_even_kernel,
        grid=(N_TILES,),
        in_specs=[tok(D),
                  pl.BlockSpec((1, 1, 6, D), lambda i: (l, _mod_row(i), 0, 0)),
                  pl.BlockSpec((1, TM, 1536), lambda i: (_rope_blk(i), 0, 0)),
                  _full((D, EVEN_IN_PAD)), _full((1, B_QRANK)), _full((B_QRANK, 768)),
                  _full((1, B_KVRANK)), _full((B_KVRANK, 1024))],
        out_specs=[tok(w) for w in widths],
        out_shape=[jax.ShapeDtypeStruct((T_ALL, w), F32) for w in widths],
        compiler_params=_params(("arbitrary",)),
        name="proj_even",
    )(x, mods, rt, w_in, g_cq, w_uq, g_ckv, w_ukv)


def _proj_odd_kernel(x_ref, mod_ref, rt_ref, win_ref, gq_ref, gk_ref, bd_ref,
                     cq_ref, ck_ref, cv_ref, dq_ref, dk_ref, dv_ref):
    x = x_ref[...]
    mod = mod_ref[0, 0]
    h = x * (1.0 + mod[1:2]) + mod[0:1]
    p = _dot(h.astype(BF16), win_ref[...])
    rt = rt_ref[0]
    cos_c, sin_c = rt[:, 0:512], rt[:, 512:1024]
    bd = bd_ref[...]
    cq = _chunk_rms_norm(p[:, 0:512], gq_ref[...], bd, D_C)
    cq_ref[...] = _rope(cq, cos_c, sin_c, D_C)
    ck = _chunk_rms_norm(p[:, 512:640], gk_ref[...], bd[0:128, 0:128], D_C)
    ck_ref[...] = _rope(ck, cos_c[:, 0:128], sin_c[:, 0:128], D_C)
    cv_ref[...] = p[:, 640:768]
    dq_ref[...] = p[:, 768:1280]
    dk_ref[...] = p[:, 1280:1792]
    dv_ref[...] = p[:, 1792:2304]


def _proj_odd(x, mods, l, rt, w_in, g_qn, g_kn, ones_bd):
    tok = lambda w: pl.BlockSpec((TM, w), lambda i: (i, 0))
    widths = (512, 128, 128, 512, 512, 512)
    return pl.pallas_call(
        _proj_odd_kernel,
        grid=(N_TILES,),
        in_specs=[tok(D),
                  pl.BlockSpec((1, 1, 6, D), lambda i: (l, _mod_row(i), 0, 0)),
                  pl.BlockSpec((1, TM, 1024), lambda i: (_rope_blk(i), 0, 0)),
                  _full((D, ODD_IN)), _full((1, 512)), _full((1, 128)), _full((512, 512))],
        out_specs=[tok(w) for w in widths],
        out_shape=[jax.ShapeDtypeStruct((T_ALL, w), F32) for w in widths],
        compiler_params=_params(("arbitrary",)),
        name="proj_odd",
    )(x, mods, rt, w_in, g_qn, g_kn, ones_bd)


def _matmul_kernel(x_ref, w_ref, o_ref):
    o_ref[...] = _dot(x_ref[...].astype(BF16), w_ref[...])


def _matmul(x, w):
    m, k = x.shape
    n = w.shape[1]
    return pl.pallas_call(
        _matmul_kernel,
        grid=(m // TM,),
        in_specs=[pl.BlockSpec((TM, k), lambda i: (i, 0)), _full((k, n))],
        out_specs=pl.BlockSpec((TM, n), lambda i: (i, 0)),
        out_shape=jax.ShapeDtypeStruct((m, n), F32),
        compiler_params=_params(("arbitrary",)),
        name="cache_kv_up",
    )(x, w)


def _attn_even_kernel(lq1_ref, lk1_ref, lq2_ref, lk2_ref, gsub_ref,
                      qa_ref, bq_ref, ka_ref, va_ref, kvb_ref, kr_ref,
                      oa_ref, ob_ref, *, lam_init):
    lam = (jnp.exp(jnp.sum(lq1_ref[...] * lk1_ref[...], axis=-1, keepdims=True))
           - jnp.exp(jnp.sum(lq2_ref[...] * lk2_ref[...], axis=-1, keepdims=True)) + lam_init)
    gsub = gsub_ref[...]
    for hd in range(H_A):
        ps = []
        for m in range(2):
            c0 = hd * 2 * D_A + m * D_A
            q = qa_ref[:, c0:c0 + D_A].astype(BF16)
            k = ka_ref[0, :, c0:c0 + D_A].astype(BF16)
            ps.append(_softmax(_dot_nt(q, k) * (D_A ** -0.5)))
        p = ps[0] - lam * ps[1]
        v = va_ref[0, :, hd * 2 * D_A:(hd + 1) * 2 * D_A].astype(BF16)
        o = _dot(p.astype(BF16), v)
        oa_ref[:, hd * 2 * D_A:(hd + 1) * 2 * D_A] = _rms_norm(o, gsub) * (1.0 - lam_init)
    kr = kr_ref[0].astype(BF16)
    for hd in range(H_B):
        qn = bq_ref[:, hd * B_NOPE:(hd + 1) * B_NOPE].astype(BF16)
        qr = bq_ref[:, 512 + hd * B_ROPE:512 + (hd + 1) * B_ROPE].astype(BF16)
        kn = kvb_ref[0, :, hd * B_NOPE:(hd + 1) * B_NOPE].astype(BF16)
        s = _dot_nt(qn, kn) + _dot_nt(qr, kr)
        p = _softmax(s * ((B_NOPE + B_ROPE) ** -0.5))
        v = kvb_ref[0, :, 512 + hd * B_V:512 + (hd + 1) * B_V].astype(BF16)
        ob_ref[:, hd * B_V:(hd + 1) * B_V] = _dot(p.astype(BF16), v)


def _attn_even(lam_vecs, g_sub, lam_init, qa, bq, ka, va, kvb, kr, nb, nq):
    nk = ka.shape[1]
    nqb = nq // TQ
    qspec = lambda w: pl.BlockSpec((TQ, w), lambda b, j: (b * nqb + j, 0))
    kspec = lambda w: pl.BlockSpec((1, nk, w), lambda b, j: (b, 0, 0))
    return pl.pallas_call(
        functools.partial(_attn_even_kernel, lam_init=lam_init),
        grid=(nb, nqb),
        in_specs=[_full((1, D_A))] * 4 + [_full((1, 2 * D_A)),
                  qspec(512), qspec(768), kspec(512), kspec(512), kspec(1024), kspec(B_ROPE)],
        out_specs=[qspec(512), qspec(512)],
        out_shape=[jax.ShapeDtypeStruct((nb * nq, 512), F32)] * 2,
        compiler_params=_params(("arbitrary", "arbitrary")),
        name="attn_even",
    )(*lam_vecs, g_sub, qa, bq, ka, va, kvb, kr)


def _gqa_kernel(cq_ref, ck_ref, cv_ref, o_ref):
    for hd in range(H_C):
        g = hd // (H_C // G_C)
        q = cq_ref[:, hd * D_C:(hd + 1) * D_C].astype(BF16)
        k = ck_ref[0, :, g * D_C:(g + 1) * D_C].astype(BF16)
        p = _softmax(_dot_nt(q, k) * (D_C ** -0.5))
        v = cv_ref[0, :, g * D_C:(g + 1) * D_C].astype(BF16)
        o_ref[:, hd * D_C:(hd + 1) * D_C] = _dot(p.astype(BF16), v)


def _gqa(cq, ck, cv, nb, nq):
    nk = ck.shape[1]
    nqb = nq // TQ
    return pl.pallas_call(
        _gqa_kernel,
        grid=(nb, nqb),
        in_specs=[pl.BlockSpec((TQ, 512), lambda b, j: (b * nqb + j, 0)),
                  pl.BlockSpec((1, nk, 128), lambda b, j: (b, 0, 0)),
                  pl.BlockSpec((1, nk, 128), lambda b, j: (b, 0, 0))],
        out_specs=pl.BlockSpec((TQ, 512), lambda b, j: (b * nqb + j, 0)),
        out_shape=jax.ShapeDtypeStruct((nb * nq, 512), F32),
        compiler_params=_params(("arbitrary", "arbitrary")),
        name="attn_gqa",
    )(cq, ck, cv)


def _mha_kernel(q_ref, k_ref, v_ref, o_ref):
    for hd in range(H_D):
        sl = slice(hd * D_D, (hd + 1) * D_D)
        q = q_ref[:, sl].astype(BF16)
        k = k_ref[0, :, sl].astype(BF16)
        p = _softmax(_dot_nt(q, k) * (D_D ** -0.5))
        o_ref[:, sl] = _dot(p.astype(BF16), v_ref[0, :, sl].astype(BF16))


def _mha(q, k, v, nb, nq):
    nk = k.shape[1]
    nqb = nq // TQ
    return pl.pallas_call(
        _mha_kernel,
        grid=(nb, nqb),
        in_specs=[pl.BlockSpec((TQ, 512), lambda b, j: (b * nqb + j, 0)),
                  pl.BlockSpec((1, nk, 512), lambda b, j: (b, 0, 0)),
                  pl.BlockSpec((1, nk, 512), lambda b, j: (b, 0, 0))],
        out_specs=pl.BlockSpec((TQ, 512), lambda b, j: (b * nqb + j, 0)),
        out_shape=jax.ShapeDtypeStruct((nb * nq, 512), F32),
        compiler_params=_params(("arbitrary", "arbitrary")),
        name="attn_mha",
    )(q, k, v)


def _na_bias_kernel(rpb_ref, o_ref):
    hd = pl.program_id(0)
    qc = lax.broadcasted_iota(jnp.int32, (GRID_W, GRID_W), 0)
    kc = lax.broadcasted_iota(jnp.int32, (GRID_W, GRID_W), 1)
    dc = jnp.clip(kc - qc, -(NA_COLS - 1), NA_COLS - 1) + (NA_COLS - 1)
    c0 = jnp.clip(qc - NA_COLS // 2, 0, GRID_W - NA_COLS)
    n_dc = 2 * NA_COLS - 1
    n_dr = 2 * NA_ROWS - 1
    for dr in range(n_dr):
        acc = jnp.zeros((GRID_W, GRID_W), F32)
        for j in range(n_dc):
            acc = jnp.where(dc == j, rpb_ref[(hd * n_dr + dr) * n_dc + j], acc)
        o_ref[0, dr] = jnp.where(kc >= c0, jnp.where(kc < c0 + NA_COLS, acc, NEG_INF), NEG_INF)


def _na_bias(rpb):
    n_dr = 2 * NA_ROWS - 1
    return pl.pallas_call(
        _na_bias_kernel,
        grid=(H_D,),
        in_specs=[pl.BlockSpec(memory_space=pltpu.SMEM)],
        out_specs=pl.BlockSpec((1, n_dr, GRID_W, GRID_W), lambda h: (h, 0, 0, 0)),
        out_shape=jax.ShapeDtypeStruct((H_D, n_dr, GRID_W, GRID_W), F32),
        compiler_params=_params(("arbitrary",)),
        name="na_bias",
    )(rpb.reshape(-1))


def _na_window_start(r):
    return jnp.clip(r - NA_ROWS // 2, 0, GRID_ROWS - NA_ROWS)


def _na_kernel(q_ref, k_ref, v_ref, ck_ref, cv_ref, bias_ref, o_ref):
    r = pl.program_id(1)
    start = pl.multiple_of(_na_window_start(r) * GRID_W, GRID_W)
    n_loc = NA_ROWS * GRID_W
    scale = D_D ** -0.5
    for hd in range(H_D):
        sl = slice(hd * D_D, (hd + 1) * D_D)
        q = q_ref[:, sl].astype(BF16)
        kw = k_ref[0, pl.ds(start, n_loc), sl].astype(BF16)
        vw = v_ref[0, pl.ds(start, n_loc), sl].astype(BF16)
        s_loc = _dot_nt(q, kw) * scale + bias_ref[0, hd]
        s_ctx = _dot_nt(q, ck_ref[0, :, sl].astype(BF16)) * scale
        m = jnp.maximum(jnp.max(s_loc, axis=-1, keepdims=True), jnp.max(s_ctx, axis=-1, keepdims=True))
        e_loc = jnp.exp(s_loc - m)
        e_ctx = jnp.exp(s_ctx - m)
        den = jnp.sum(e_loc, axis=-1, keepdims=True) + jnp.sum(e_ctx, axis=-1, keepdims=True)
        o = (_dot((e_loc / den).astype(BF16), vw)
             + _dot((e_ctx / den).astype(BF16), cv_ref[0, :, sl].astype(BF16)))
        o_ref[:, sl] = o


def _neighbourhood(q, k, v, ck, cv, bias_win):
    n_loc = NA_ROWS * GRID_W
    return pl.pallas_call(
        _na_kernel,
        grid=(N_LAT_B, GRID_ROWS),
        in_specs=[pl.BlockSpec((GRID_W, 512), lambda b, r: (b * GRID_ROWS + r, 0)),
                  pl.BlockSpec((1, LAT_S, 512), lambda b, r: (b, 0, 0)),
                  pl.BlockSpec((1, LAT_S, 512), lambda b, r: (b, 0, 0)),
                  pl.BlockSpec((1, PAST, 512), lambda b, r: (b, 0, 0)),
                  pl.BlockSpec((1, PAST, 512), lambda b, r: (b, 0, 0)),
                  pl.BlockSpec((1, H_D, GRID_W, n_loc),
                               lambda b, r: (_na_window_start(r) - r + NA_ROWS - 1, 0, 0, 0))],
        out_specs=pl.BlockSpec((GRID_W, 512), lambda b, r: (b * GRID_ROWS + r, 0)),
        out_shape=jax.ShapeDtypeStruct((T_LAT, 512), F32),
        compiler_params=_params(("arbitrary", "arbitrary")),
        name="attn_neighbourhood",
    )(q, k, v, ck, cv, bias_win)


def _post_kernel(x_ref, mod_ref, o1_ref, o2_ref, wo1_ref, wo2_ref, g1_ref, b1_ref,
                 wq_ref, k1_ref, k2_ref, x1_ref, h2t_ref, s1_ref, s2_ref):
    mod = mod_ref[0, 0]
    y = _dot(o1_ref[...].astype(BF16), wo1_ref[...]) + _dot(o2_ref[...].astype(BF16), wo2_ref[...])
    x1 = _layer_norm(ALPHA * x_ref[...] + mod[2:3] * y, g1_ref[...], b1_ref[...])
    x1_ref[...] = x1
    h2f = x1 * (1.0 + mod[4:5]) + mod[3:4]
    h2 = h2f.astype(BF16)
    h2t_ref[...] = h2f.T.astype(BF16)
    q = _dot(h2, wq_ref[...]).astype(BF16)
    half = P_DKEY // 2
    k1 = k1_ref[...]
    k2 = k2_ref[...]
    for hd in range(P_HEADS):
        s1_ref[hd] = _dot_nt(k1, q[:, hd * P_DKEY:hd * P_DKEY + half])
        s2_ref[hd] = _dot_nt(k2, q[:, hd * P_DKEY + half:(hd + 1) * P_DKEY])


def _post(x, mods, l, o1, o2, wo1, wo2, g1, b1, wq, k1, k2):
    tok = lambda w: pl.BlockSpec((TM, w), lambda i: (i, 0))
    sspec = pl.BlockSpec((P_HEADS, P_KEYS, TM), lambda i: (0, 0, i))
    return pl.pallas_call(
        _post_kernel,
        grid=(N_TILES,),
        in_specs=[tok(D),
                  pl.BlockSpec((1, 1, 6, D), lambda i: (l, _mod_row(i), 0, 0)),
                  tok(512), tok(512), _full((512, D)), _full((512, D)), _full((1, D)), _full((1, D)),
                  _full((D, P_HEADS * P_DKEY)), _full((P_KEYS, P_DKEY // 2)), _full((P_KEYS, P_DKEY // 2))],
        out_specs=[tok(D), pl.BlockSpec((D, TM), lambda i: (0, i)), sspec, sspec],
        out_shape=[jax.ShapeDtypeStruct((T_ALL, D), F32),
                   jax.ShapeDtypeStruct((D, T_ALL), BF16),
                   jax.ShapeDtypeStruct((P_HEADS, P_KEYS, T_ALL), F32),
                   jax.ShapeDtypeStruct((P_HEADS, P_KEYS, T_ALL), F32)],
        compiler_params=_params(("arbitrary",)),
        name="post_mixer",
    )(x, mods, o1, o2, wo1, wo2, g1, b1, wq, k1, k2)


def _top_values(x, n):
    vals = []
    for _ in range(n):
        m = jnp.max(x, axis=0, keepdims=True)
        vals.append(m)
        x = jnp.where(x == m, NEG_INF, x)
    return vals


PEER_PAIRS = [(a, b) for a in range(PEER_NSEL) for b in range(PEER_NSEL) if (a + 1) * (b + 1) <= PEER_NSEL]
PEER_NCAND = -(-len(PEER_PAIRS) // 8) * 8


def _peer_select(s1_ref, s2_ref, eth_scr, e1_scr, e2_scr, cand_scr):
    n_pad = PEER_NCAND - len(PEER_PAIRS)
    cand_scr[len(PEER_PAIRS):, :] = jnp.full((n_pad, cand_scr.shape[1]), NEG_INF, F32)

    def body(hd, carry):
        s1 = s1_ref[hd]
        s2 = s2_ref[hd]
        v1 = _top_values(s1, PEER_NSEL)
        v2 = _top_values(s2, PEER_NSEL)
        for k, (a, b) in enumerate(PEER_PAIRS):
            cand_scr[k:k + 1, :] = v1[a] + v2[b]
        cs = _top_values(cand_scr[...], PEER_NSEL)
        tau = 0.5 * (cs[P_TOPK - 1] + cs[P_TOPK])
        z = jnp.exp(cs[0] - cs[0])
        for j in range(1, P_TOPK):
            z = z + jnp.exp(cs[j] - cs[0])
        half_inv_z = 0.5 / z
        eth_scr[hd] = jnp.exp((tau - s1) - v2[0]) * half_inv_z
        e1_scr[hd] = jnp.exp(s1 - v1[0])
        e2_scr[hd] = jnp.exp(s2 - v2[0]) * half_inv_z
        return carry

    lax.fori_loop(0, P_HEADS, body, 0)


def _gelu2(a):
    return a * (1.0 + lax.erf(a * (2.0 ** -0.5)))


PEER_LW = 256
PEER_CH = 256


def _peer_kernel(h2t_ref, s1_ref, s2_ref, u_ref, vt_prev_ref, vt_last_ref, x1_ref, mod_ref, g2_ref, b2_ref,
                 o_ref, eth_scr, e1_scr, e2_scr, cand_scr, c_cur, c_prev, acc_scr):
    eb = pl.program_id(1)
    n_eb = pl.num_programs(1)
    rows_per_step = PEER_EB // P_KEYS

    @pl.when(eb == 0)
    def _():
        _peer_select(s1_ref, s2_ref, eth_scr, e1_scr, e2_scr, cand_scr)
        acc_scr[...] = jnp.zeros_like(acc_scr)
        c_prev[...] = jnp.zeros_like(c_prev)

    @pl.when(eb > 0)
    def _():
        c_prev[...] = c_cur[...]

    h2t = h2t_ref[...]
    n_ch = PEER_EB // PEER_CH
    rb = D // n_ch
    act_next = _dot(u_ref[0:PEER_CH, :], h2t)
    for k in range(n_ch):
        act = act_next
        if k + 1 < n_ch:
            act_next = _dot(u_ref[(k + 1) * PEER_CH:(k + 2) * PEER_CH, :], h2t)
        acc_scr[k * rb:(k + 1) * rb, :] += _dot(vt_prev_ref[k * rb:(k + 1) * rb, :], c_prev[...])
        for jj in range(PEER_CH // P_KEYS):
            j = k * (PEER_CH // P_KEYS) + jj
            i1 = eb * rows_per_step + j
            for lc in range(PEER_TT // PEER_LW):
                ls = slice(lc * PEER_LW, (lc + 1) * PEER_LW)
                w = jnp.zeros((P_KEYS, PEER_LW), F32)
                for hd in range(P_HEADS):
                    eth = eth_scr[hd, pl.ds(i1, 1), ls]
                    e1 = e1_scr[hd, pl.ds(i1, 1), ls]
                    e2 = e2_scr[hd, :, ls]
                    w = w + jnp.where(e2 >= eth, e2, 0.0) * e1
                a = act[jj * P_KEYS:(jj + 1) * P_KEYS, ls]
                c_cur[j * P_KEYS:(j + 1) * P_KEYS, ls] = (w * _gelu2(a)).astype(BF16)

    @pl.when(eb == n_eb - 1)
    def _():
        mod = mod_ref[0, 0]
        peer = (acc_scr[...] + _dot(vt_last_ref[...], c_cur[...])).T
        o_ref[...] = _layer_norm(ALPHA * x1_ref[...] + mod[5:6] * peer, g2_ref[...], b2_ref[...])


def _peer(h2t, s1, s2, u_bf, vt_bf, x1, mods, l, g2, b2):
    tiles_ctx = T_CTX // PEER_TT
    tiles_per_lat = LAT_S // PEER_TT
    n_eb = P_EXPERTS // PEER_EB
    mod_row = lambda i: jnp.where(i < tiles_ctx, 0, 1 + (i - tiles_ctx) // tiles_per_lat)
    tok = lambda w: pl.BlockSpec((PEER_TT, w), lambda i, e: (i, 0))
    sspec = pl.BlockSpec((P_HEADS, P_KEYS, PEER_TT), lambda i, e: (0, 0, i))
    sel = pltpu.VMEM((P_HEADS, P_KEYS, PEER_TT), F32)
    return pl.pallas_call(
        _peer_kernel,
        grid=(T_ALL // PEER_TT, n_eb),
        in_specs=[pl.BlockSpec((D, PEER_TT), lambda i, e: (0, i)), sspec, sspec,
                  pl.BlockSpec((PEER_EB, D), lambda i, e: (e, 0)),
                  pl.BlockSpec((D, PEER_EB), lambda i, e: (0, jnp.maximum(e - 1, 0))),
                  pl.BlockSpec((D, PEER_EB), lambda i, e: (0, jnp.where(e == n_eb - 1, n_eb - 1, 0))),
                  tok(D),
                  pl.BlockSpec((1, 1, 6, D), lambda i, e: (l, mod_row(i), 0, 0)),
                  pl.BlockSpec((1, D), lambda i, e: (0, 0)),
                  pl.BlockSpec((1, D), lambda i, e: (0, 0))],
        out_specs=tok(D),
        out_shape=jax.ShapeDtypeStruct((T_ALL, D), F32),
        scratch_shapes=[sel, sel, sel,
                        pltpu.VMEM((PEER_NCAND, PEER_TT), F32),
                        pltpu.VMEM((PEER_EB, PEER_TT), BF16),
                        pltpu.VMEM((PEER_EB, PEER_TT), BF16),
                        pltpu.VMEM((D, PEER_TT), F32)],
        compiler_params=_params(("arbitrary", "arbitrary")),
        name="peer",
    )(h2t, s1, s2, u_bf, vt_bf, vt_bf, x1, mods, g2, b2)


def _grid_angles(rot_dim):
    t = jnp.arange(LAT_S)
    row = (t // GRID_W).astype(F32)
    col = (t % GRID_W).astype(F32)
    n_freq = rot_dim // 4
    inv = THETA ** (-jnp.arange(n_freq, dtype=F32) / n_freq)
    return jnp.concatenate([row[:, None] * inv, col[:, None] * inv], axis=-1)


def _rope_tables(rot_dim, width):
    ang = _grid_angles(rot_dim)
    cos = jnp.cos(ang)
    sin = jnp.sin(ang)
    cos_t = jnp.tile(jnp.concatenate([cos, cos], axis=-1), (1, width // rot_dim))
    sin_t = jnp.tile(jnp.concatenate([-sin, sin], axis=-1), (1, width // rot_dim))
    ident_c = jnp.ones((1, TM, width), F32)
    ident_s = jnp.zeros((1, TM, width), F32)
    cos_t = jnp.concatenate([ident_c, cos_t.reshape(TILES_PER_LAT, TM, width)], axis=0)
    sin_t = jnp.concatenate([ident_s, sin_t.reshape(TILES_PER_LAT, TM, width)], axis=0)
    return cos_t, sin_t


def _lat_keys(own, cache):
    w = own.shape[-1]
    return jnp.concatenate([own.reshape(N_LAT_B, LAT_S, w), cache.reshape(N_LAT_B, PAST, w)], axis=1)


def kernel(x_prompt, x_sample, cache_a_k, cache_a_v, cache_b_ckv, cache_b_kr, cache_c_k, cache_c_v, cache_d_k, cache_d_v, c, c_ctx, w_mod, b_mod, ln1_g, ln1_b, ln2_g, ln2_b, ev_w_in, ev_lam_q1, ev_lam_k1, ev_lam_q2, ev_lam_k2, ev_g_sub, ev_g_cq, ev_w_uq, ev_g_ckv, ev_w_ukv, ev_w_out, od_w_in, od_g_qn, od_g_kn, od_rpb, od_w_out, pk_w_q, pk_k1, pk_k2, pk_u, pk_v):
    x = jnp.concatenate([x_prompt.reshape(T_CTX, D), x_sample.reshape(T_LAT, D)], axis=0)
    cvecs = jnp.concatenate([c_ctx[None, :], c, jnp.zeros((8 - 1 - N_LAT_B, D), F32)], axis=0)
    mods = _modulation(cvecs, w_mod, b_mod).reshape(DEPTH, 8, 6, D)

    cos_a, sin_a = _rope_tables(D_A, 512)
    cos_b, sin_b = _rope_tables(B_ROPE, 256)
    rt_even = jnp.concatenate([cos_a, sin_a, cos_b, sin_b], axis=-1)
    rt_odd = jnp.concatenate(_rope_tables(D_C, 512), axis=-1)
    ones_bd = jnp.kron(jnp.eye(512 // D_C, dtype=F32), jnp.ones((D_C, D_C), F32)).astype(BF16)

    states = {}
    for l in range(DEPTH):
        j = l // 2
        if l % 2 == 0:
            lam_init = 0.8 - 0.6 * math.exp(-0.3 * l)
            w_in = jnp.pad(ev_w_in[j], ((0, 0), (0, EVEN_IN_PAD - ev_w_in.shape[-1]))).astype(BF16)
            wuq = ev_w_uq[j].reshape(B_QRANK, H_B, B_NOPE + B_ROPE)
            wuq = jnp.concatenate([wuq[:, :, :B_NOPE].reshape(B_QRANK, -1),
                                   wuq[:, :, B_NOPE:].reshape(B_QRANK, -1)], axis=1).astype(BF16)
            wukv = ev_w_ukv[j].reshape(B_KVRANK, H_B, B_NOPE + B_V)
            wukv = jnp.concatenate([wukv[:, :, :B_NOPE].reshape(B_KVRANK, -1),
                                    wukv[:, :, B_NOPE:].reshape(B_KVRANK, -1)], axis=1).astype(BF16)
            qa, ka, va, bq, ckv, kr, kvb = _proj_even(
                x, mods, l, rt_even, w_in, ev_g_cq[j][None, :], wuq, ev_g_ckv[j][None, :], wukv)
            states["a_k"], states["a_v"] = ka[:T_CTX], va[:T_CTX]
            states["b_ckv"], states["b_kr"] = ckv[:T_CTX], kr[:T_CTX]
            lam_vecs = [v[j][None, :] for v in (ev_lam_q1, ev_lam_k1, ev_lam_q2, ev_lam_k2)]
            g_sub = ev_g_sub[j][None, :]
            ctx3 = lambda a: a[:T_CTX].reshape(N_CTX_B, CTX_S, a.shape[-1])
            oa_c, ob_c = _attn_even(lam_vecs, g_sub, lam_init, qa[:T_CTX], bq[:T_CTX],
                                    ctx3(ka), ctx3(va), ctx3(kvb), ctx3(kr), N_CTX_B, CTX_S)
            cache_kvb = _matmul(cache_b_ckv[:, j].reshape(N_LAT_B * PAST, B_KVRANK), wukv)
            oa_l, ob_l = _attn_even(lam_vecs, g_sub, lam_init, qa[T_CTX:], bq[T_CTX:],
                                    _lat_keys(ka[T_CTX:], cache_a_k[:, j]),
                                    _lat_keys(va[T_CTX:], cache_a_v[:, j]),
                                    _lat_keys(kvb[T_CTX:], cache_kvb),
                                    _lat_keys(kr[T_CTX:], cache_b_kr[:, j]), N_LAT_B, LAT_S)
            o1 = jnp.concatenate([oa_c, oa_l], axis=0)
            o2 = jnp.concatenate([ob_c, ob_l], axis=0)
            w_out = ev_w_out[j]
        else:
            w_in = od_w_in[j].astype(BF16)
            g_qn = jnp.tile(od_g_qn[j], H_C)[None, :]
            g_kn = jnp.tile(od_g_kn[j], G_C)[None, :]
            cq, ck, cv, dq, dk, dv = _proj_odd(x, mods, l, rt_odd, w_in, g_qn, g_kn, ones_bd)
            states["c_k"], states["c_v"] = ck[:T_CTX], cv[:T_CTX]
            states["d_k"], states["d_v"] = dk[:T_CTX], dv[:T_CTX]
            ctx3 = lambda a: a[:T_CTX].reshape(N_CTX_B, CTX_S, a.shape[-1])
            oc_c = _gqa(cq[:T_CTX], ctx3(ck), ctx3(cv), N_CTX_B, CTX_S)
            od_c = _mha(dq[:T_CTX], ctx3(dk), ctx3(dv), N_CTX_B, CTX_S)
            oc_l = _gqa(cq[T_CTX:], _lat_keys(ck[T_CTX:], cache_c_k[:, j]),
                        _lat_keys(cv[T_CTX:], cache_c_v[:, j]), N_LAT_B, LAT_S)
            bias = _na_bias(od_rpb[j])
            offs = jnp.arange(NA_ROWS + 1)[:, None] + jnp.arange(NA_ROWS)[None, :]
            bias_win = jnp.transpose(bias[:, offs], (1, 0, 3, 2, 4)).reshape(
                NA_ROWS + 1, H_D, GRID_W, NA_ROWS * GRID_W)
            od_l = _neighbourhood(dq[T_CTX:], dk[T_CTX:].reshape(N_LAT_B, LAT_S, 512),
                                  dv[T_CTX:].reshape(N_LAT_B, LAT_S, 512),
                                  cache_d_k[:, j].reshape(N_LAT_B, PAST, 512),
                                  cache_d_v[:, j].reshape(N_LAT_B, PAST, 512), bias_win)
            o1 = jnp.concatenate([oc_c, oc_l], axis=0)
            o2 = jnp.concatenate([od_c, od_l], axis=0)
            w_out = od_w_out[j]

        x1, h2, s1, s2 = _post(x, mods, l, o1, o2, w_out[:512].astype(BF16), w_out[512:].astype(BF16),
                               ln1_g[l][None, :], ln1_b[l][None, :], pk_w_q[l].astype(BF16),
                               pk_k1[l].astype(BF16), pk_k2[l].astype(BF16))
        x = _peer(h2, s1, s2, pk_u[l].astype(BF16), pk_v[l].T.astype(BF16), x1, mods, l,
                  ln2_g[l][None, :], ln2_b[l][None, :])

    y = x[:T_CTX].reshape(N_CTX_B, CTX_S, D)
    z = x[T_CTX:].reshape(N_LAT_B, LAT_S, D)
    n_even = (DEPTH + 1) // 2
    n_odd = DEPTH // 2
    return (y, z,
            states["a_k"].reshape(N_CTX_B, n_even, CTX_S, H_A, 2, D_A),
            states["a_v"].reshape(N_CTX_B, n_even, CTX_S, H_A, 2 * D_A),
            states["b_ckv"].reshape(N_CTX_B, n_even, CTX_S, B_KVRANK),
            states["b_kr"].reshape(N_CTX_B, n_even, CTX_S, B_ROPE),
            states["c_k"].reshape(N_CTX_B, n_odd, CTX_S, G_C, D_C),
            states["c_v"].reshape(N_CTX_B, n_odd, CTX_S, G_C, D_C),
            states["d_k"].reshape(N_CTX_B, n_odd, CTX_S, H_D, D_D),
            states["d_v"].reshape(N_CTX_B, n_odd, CTX_S, H_D, D_D))
```

```python
import functools
import math

import jax
import jax.numpy as jnp
from jax import lax
from jax.experimental import pallas as pl
from jax.experimental.pallas import tpu as pltpu

F32 = jnp.float32
BF16 = jnp.bfloat16

D = 1024
N_CTX_B = 32
CTX_S = 256
N_LAT_B = 4
LAT_S = 1024
PAST = 256
T_CTX = N_CTX_B * CTX_S
T_LAT = N_LAT_B * LAT_S
T_ALL = T_CTX + T_LAT
DEPTH = 2
GRID_W = 64
GRID_ROWS = LAT_S // GRID_W
THETA = 10000.0
EPS = 1e-6
ALPHA = (2 * DEPTH) ** 0.25

H_A, D_A = 4, 64
H_B, B_NOPE, B_ROPE, B_V, B_QRANK, B_KVRANK = 8, 64, 32, 64, 256, 128
H_C, G_C, D_C = 8, 2, 64
H_D, D_D = 8, 64
NA_ROWS, NA_COLS = 8, 16
P_HEADS, P_KEYS, P_DKEY, P_TOPK = 8, 128, 256, 16
P_EXPERTS = P_KEYS * P_KEYS

EVEN_IN_PAD = 2048
ODD_IN = 2304

TM = 256
N_TILES = T_ALL // TM
N_CTX_TILES = T_CTX // TM
TILES_PER_LAT = LAT_S // TM
N_LAT_TILES = T_LAT // TM
TQ = 256
PEER_TT = 512
PEER_EB = 1024
PEER_NSEL = P_TOPK + 1
VMEM_LIMIT = 56 * 1024 * 1024

NEG_INF = float("-inf")


def _dot(a, b):
    return jnp.dot(a, b, preferred_element_type=F32)


def _dot_nt(a, b):
    return lax.dot_general(a, b, (((1,), (1,)), ((), ())), preferred_element_type=F32)


def _layer_norm(x, g, b):
    mu = jnp.mean(x, axis=-1, keepdims=True)
    xc = x - mu
    var = jnp.mean(xc * xc, axis=-1, keepdims=True)
    return xc * lax.rsqrt(var + EPS) * g + b


def _rms_norm(x, g):
    return x * lax.rsqrt(jnp.mean(x * x, axis=-1, keepdims=True) + EPS) * g


def _chunk_rms_norm(x, g, ones_bd, chunk):
    x2 = x * x
    hi = x2.astype(BF16)
    lo = (x2 - hi.astype(F32)).astype(BF16)
    ms = (_dot(hi, ones_bd) + _dot(lo, ones_bd)) * (1.0 / chunk)
    return x * lax.rsqrt(ms + EPS) * g


def _rope(x, cos_t, sin_t, chunk):
    n = x.shape[-1]
    half = chunk // 2
    lane = lax.broadcasted_iota(jnp.int32, x.shape, x.ndim - 1)
    swapped = jnp.where((lane % chunk) < half,
                        pltpu.roll(x, n - half, x.ndim - 1),
                        pltpu.roll(x, half, x.ndim - 1))
    return x * cos_t + swapped * sin_t


def _params(sem):
    return pltpu.CompilerParams(dimension_semantics=sem, vmem_limit_bytes=VMEM_LIMIT)


def _full(shape):
    zeros = (0,) * len(shape)
    return pl.BlockSpec(shape, lambda *_: zeros)


def _mod_row(i):
    return jnp.where(i < N_CTX_TILES, 0, 1 + (i - N_CTX_TILES) // TILES_PER_LAT)


def _rope_blk(i):
    return jnp.where(i < N_CTX_TILES, 0, 1 + (i - N_CTX_TILES) % TILES_PER_LAT)


def _mod_kernel(c_ref, w_ref, b_ref, o_ref):
    c = c_ref[...]
    a = c * (1.0 / (1.0 + jnp.exp(-c)))
    o_ref[0] = _dot(a.astype(BF16), w_ref[0].astype(BF16)) + b_ref[0]


def _modulation(cvecs, w_mod, b_mod):
    nb = 6
    bn = 6 * D // nb
    return pl.pallas_call(
        _mod_kernel,
        grid=(DEPTH, nb),
        in_specs=[pl.BlockSpec((8, D), lambda l, j: (0, 0)),
                  pl.BlockSpec((1, D, bn), lambda l, j: (l, 0, j)),
                  pl.BlockSpec((1, 1, bn), lambda l, j: (l, 0, j))],
        out_specs=pl.BlockSpec((1, 8, bn), lambda l, j: (l, 0, j)),
        out_shape=jax.ShapeDtypeStruct((DEPTH, 8, 6 * D), F32),
        compiler_params=_params(("arbitrary", "arbitrary")),
        name="modulation",
    )(cvecs, w_mod, b_mod.reshape(DEPTH, 1, 6 * D))


def _tile(g):
    return (g + N_CTX_TILES) % N_TILES


def _state_blk(g):
    return jnp.maximum(g - N_LAT_TILES, 0)


def _proj_even_kernel(x_ref, mod_ref, rt_ref, win_ref, gcq_ref, wuq_ref, gckv_ref, wukv_ref,
                      qa_ref, bq_ref, ka_ref, va_ref, kvb_ref, kr_ref,
                      s_ak_ref, s_av_ref, s_ckv_ref, s_kr_ref):
    x = x_ref[...]
    mod = mod_ref[0, 0]
    h = x * (1.0 + mod[1:2]) + mod[0:1]
    p = _dot(h.astype(BF16), win_ref[...])
    rt = rt_ref[0]
    cos_a, sin_a = rt[:, 0:512], rt[:, 512:1024]
    cos_b, sin_b = rt[:, 1024:1280], rt[:, 1280:1536]
    qa_ref[...] = _rope(p[:, 0:512], cos_a, sin_a, D_A).astype(BF16)
    ka = _rope(p[:, 512:1024], cos_a, sin_a, D_A)
    ka_ref[...] = ka.astype(BF16)
    s_ak_ref[...] = ka
    va = p[:, 1024:1536]
    va_ref[...] = va.astype(BF16)
    s_av_ref[...] = va
    cq = _rms_norm(p[:, 1536:1792], gcq_ref[...])
    bq = _dot(cq.astype(BF16), wuq_ref[...])
    bq_ref[:, 0:512] = bq[:, 0:512].astype(BF16)
    bq_ref[:, 512:768] = _rope(bq[:, 512:768], cos_b, sin_b, B_ROPE).astype(BF16)
    ckv = _rms_norm(p[:, 1792:1920], gckv_ref[...])
    s_ckv_ref[...] = ckv
    kvb_ref[...] = _dot(ckv.astype(BF16), wukv_ref[...]).astype(BF16)
    kr = _rope(p[:, 1920:2048], cos_b[:, 0:128], sin_b[:, 0:128], B_ROPE)[:, 0:B_ROPE]
    kr_ref[...] = kr.astype(BF16)
    s_kr_ref[...] = kr


def _proj_call(kernel_fn, name, x, mods, l, rt, weights, widths, state_widths):
    tok = lambda w: pl.BlockSpec((TM, w), lambda g: (_tile(g), 0))
    st = lambda w: pl.BlockSpec((TM, w), lambda g: (_state_blk(g), 0))
    return pl.pallas_call(
        kernel_fn,
        grid=(N_TILES,),
        in_specs=[tok(D),
                  pl.BlockSpec((1, 1, 6, D), lambda g: (l, _mod_row(_tile(g)), 0, 0)),
                  pl.BlockSpec((1, TM, rt.shape[-1]), lambda g: (_rope_blk(_tile(g)), 0, 0))]
                 + [_full(w.shape) for w in weights],
        out_specs=[tok(w) for w in widths] + [st(w) for w in state_widths],
        out_shape=[jax.ShapeDtypeStruct((T_ALL, w), BF16) for w in widths]
                  + [jax.ShapeDtypeStruct((T_CTX, w), F32) for w in state_widths],
        compiler_params=_params(("arbitrary",)),
        name=name,
    )(x, mods, rt, *weights)


def _proj_odd_kernel(x_ref, mod_ref, rt_ref, win_ref, gq_ref, gk_ref, bd_ref,
                     cq_ref, ck_ref, cv_ref, dq_ref, dk_ref, dv_ref,
                     s_ck_ref, s_cv_ref, s_dk_ref, s_dv_ref):
    x = x_ref[...]
    mod = mod_ref[0, 0]
    h = x * (1.0 + mod[1:2]) + mod[0:1]
    p = _dot(h.astype(BF16), win_ref[...])
    rt = rt_ref[0]
    cos_c, sin_c = rt[:, 0:512], rt[:, 512:1024]
    bd = bd_ref[...]
    cq = _chunk_rms_norm(p[:, 0:512], gq_ref[...], bd, D_C)
    cq_ref[...] = _rope(cq, cos_c, sin_c, D_C).astype(BF16)
    ck = _chunk_rms_norm(p[:, 512:640], gk_ref[...], bd[0:128, 0:128], D_C)
    ck = _rope(ck, cos_c[:, 0:128], sin_c[:, 0:128], D_C)
    ck_ref[...] = ck.astype(BF16)
    s_ck_ref[...] = ck
    cv = p[:, 640:768]
    cv_ref[...] = cv.astype(BF16)
    s_cv_ref[...] = cv
    dq_ref[...] = p[:, 768:1280].astype(BF16)
    dk = p[:, 1280:1792]
    dk_ref[...] = dk.astype(BF16)
    s_dk_ref[...] = dk
    dv = p[:, 1792:2304]
    dv_ref[...] = dv.astype(BF16)
    s_dv_ref[...] = dv


def _matmul_kernel(x_ref, w_ref, o_ref):
    o_ref[...] = _dot(x_ref[...].astype(BF16), w_ref[...]).astype(BF16)


def _matmul(x, w):
    m, k = x.shape
    n = w.shape[1]
    return pl.pallas_call(
        _matmul_kernel,
        grid=(m // TM,),
        in_specs=[pl.BlockSpec((TM, k), lambda i: (i, 0)), _full((k, n))],
        out_specs=pl.BlockSpec((TM, n), lambda i: (i, 0)),
        out_shape=jax.ShapeDtypeStruct((m, n), BF16),
        compiler_params=_params(("arbitrary",)),
        name="cache_kv_up",
    )(x, w)


def _softmax_segments(segs):
    m = jnp.max(segs[0], axis=-1, keepdims=True)
    for s in segs[1:]:
        m = jnp.maximum(m, jnp.max(s, axis=-1, keepdims=True))
    es = [jnp.exp(s - m) for s in segs]
    den = jnp.sum(es[0], axis=-1, keepdims=True)
    for e in es[1:]:
        den = den + jnp.sum(e, axis=-1, keepdims=True)
    return [e / den for e in es]


def _pv(ps, vs):
    o = _dot(ps[0].astype(BF16), vs[0])
    for p, v in zip(ps[1:], vs[1:]):
        o = o + _dot(p.astype(BF16), v)
    return o


def _attn_specs(row0, n_req, n_q, widths_q, widths_k, cache_shapes):
    n_own = n_q
    nqb = n_q // TQ
    q0 = row0 // TQ
    k0 = row0 // n_own
    qspec = lambda w: pl.BlockSpec((TQ, w), lambda b, j: (q0 + b * nqb + j, 0))
    kspec = lambda w: pl.BlockSpec((n_own, w), lambda b, j: (k0 + b, 0))
    cspec = lambda shp: pl.BlockSpec((1,) + tuple(shp[1:]), lambda b, j: (b,) + (0,) * (len(shp) - 1))
    specs = [qspec(w) for w in widths_q] + [kspec(w) for w in widths_k] + [cspec(s) for s in cache_shapes]
    return specs, qspec, (n_req, nqb)


def _attn_even_kernel(*refs, lam_init, has_cache):
    lq1_ref, lk1_ref, lq2_ref, lk2_ref, gsub_ref, qa_ref, bq_ref, ka_ref, va_ref, kvb_ref, kr_ref = refs[:11]
    if has_cache:
        cka_ref, cva_ref, ckvb_ref, ckr_ref = refs[11:15]
    oa_ref, ob_ref = refs[-2:]
    lam = (jnp.exp(jnp.sum(lq1_ref[...] * lk1_ref[...], axis=-1, keepdims=True))
           - jnp.exp(jnp.sum(lq2_ref[...] * lk2_ref[...], axis=-1, keepdims=True)) + lam_init)
    gsub = gsub_ref[...]
    for hd in range(H_A):
        ps = []
        for m in range(2):
            sl = slice(hd * 2 * D_A + m * D_A, hd * 2 * D_A + (m + 1) * D_A)
            q = qa_ref[:, sl]
            segs = [_dot_nt(q, ka_ref[:, sl]) * (D_A ** -0.5)]
            if has_cache:
                segs.append(_dot_nt(q, cka_ref[0, :, sl].astype(BF16)) * (D_A ** -0.5))
            ps.append(_softmax_segments(segs))
        vsl = slice(hd * 2 * D_A, (hd + 1) * 2 * D_A)
        vs = [va_ref[:, vsl]] + ([cva_ref[0, :, vsl].astype(BF16)] if has_cache else [])
        o = _pv([p0 - lam * p1 for p0, p1 in zip(ps[0], ps[1])], vs)
        oa_ref[:, vsl] = (_rms_norm(o, gsub) * (1.0 - lam_init)).astype(BF16)
    kr = kr_ref[...]
    if has_cache:
        ckr = ckr_ref[0].astype(BF16)
    sc = (B_NOPE + B_ROPE) ** -0.5
    for hd in range(H_B):
        nsl = slice(hd * B_NOPE, (hd + 1) * B_NOPE)
        vsl = slice(512 + hd * B_V, 512 + (hd + 1) * B_V)
        qn = bq_ref[:, nsl]
        qr = bq_ref[:, 512 + hd * B_ROPE:512 + (hd + 1) * B_ROPE]
        segs = [(_dot_nt(qn, kvb_ref[:, nsl]) + _dot_nt(qr, kr)) * sc]
        vs = [kvb_ref[:, vsl]]
        if has_cache:
            segs.append((_dot_nt(qn, ckvb_ref[0, :, nsl]) + _dot_nt(qr, ckr)) * sc)
            vs.append(ckvb_ref[0, :, vsl])
        ob_ref[:, hd * B_V:(hd + 1) * B_V] = _pv(_softmax_segments(segs), vs).astype(BF16)


def _attn_call(kernel_fn, name, params, arrays, widths_q, widths_k, latent, cache, prev_out, n_out):
    row0, n_req, n_q = (T_CTX, N_LAT_B, LAT_S) if latent else (0, N_CTX_B, CTX_S)
    cache = list(cache)
    cache_shapes = []
    for a in cache:
        cache_shapes.append(a.shape if a.ndim == 3 else (n_req,) + (a.shape[0] // n_req, a.shape[1]))
    cache = [a.reshape(shp) for a, shp in zip(cache, cache_shapes)]
    specs, qspec, grid = _attn_specs(row0, n_req, n_q, widths_q, widths_k, cache_shapes)
    in_specs = [_full(p.shape) for p in params] + specs
    operands = list(params) + list(arrays) + cache
    aliases = {}
    if prev_out is not None:
        for i, o in enumerate(prev_out):
            aliases[len(operands)] = i
            operands.append(o)
            in_specs.append(pl.BlockSpec(memory_space=pl.ANY))
    return pl.pallas_call(
        kernel_fn,
        grid=grid,
        in_specs=in_specs,
        out_specs=[qspec(512)] * n_out,
        out_shape=[jax.ShapeDtypeStruct((T_ALL, 512), BF16)] * n_out,
        input_output_aliases=aliases,
        compiler_params=_params(("arbitrary", "arbitrary")),
        name=name,
    )(*operands)


def _attn_even(lam_vecs, g_sub, lam_init, qa, bq, ka, va, kvb, kr, cache=(), prev_out=None):
    latent = prev_out is not None
    fn = functools.partial(_attn_even_kernel, lam_init=lam_init, has_cache=latent)
    return _attn_call(fn, "attn_even", list(lam_vecs) + [g_sub], [qa, bq, ka, va, kvb, kr],
                      (512, 768), (512, 512, 1024, B_ROPE), latent, cache, prev_out, 2)


def _gqa_kernel(*refs, has_cache):
    cq_ref, ck_ref, cv_ref = refs[:3]
    if has_cache:
        cck_ref, ccv_ref = refs[3:5]
    o_ref = refs[-1]
    for hd in range(H_C):
        g = hd // (H_C // G_C)
        gsl = slice(g * D_C, (g + 1) * D_C)
        q = cq_ref[:, hd * D_C:(hd + 1) * D_C]
        segs = [_dot_nt(q, ck_ref[:, gsl]) * (D_C ** -0.5)]
        vs = [cv_ref[:, gsl]]
        if has_cache:
            segs.append(_dot_nt(q, cck_ref[0, :, gsl].astype(BF16)) * (D_C ** -0.5))
            vs.append(ccv_ref[0, :, gsl].astype(BF16))
        o_ref[:, hd * D_C:(hd + 1) * D_C] = _pv(_softmax_segments(segs), vs).astype(BF16)


def _gqa(cq, ck, cv, cache=(), prev_out=None):
    latent = prev_out is not None
    fn = functools.partial(_gqa_kernel, has_cache=latent)
    return _attn_call(fn, "attn_gqa", [], [cq, ck, cv], (512,), (128, 128), latent, cache, prev_out, 1)[0]


def _mha_kernel(q_ref, k_ref, v_ref, o_ref):
    for hd in range(H_D):
        sl = slice(hd * D_D, (hd + 1) * D_D)
        p = _softmax_segments([_dot_nt(q_ref[:, sl], k_ref[:, sl]) * (D_D ** -0.5)])
        o_ref[:, sl] = _pv(p, [v_ref[:, sl]]).astype(BF16)


def _mha(q, k, v):
    return _attn_call(_mha_kernel, "attn_mha", [], [q, k, v], (512,), (512, 512), False, (), None, 1)[0]


def _na_bias_kernel(rpb_ref, o_ref):
    hd = pl.program_id(0)
    qc = lax.broadcasted_iota(jnp.int32, (GRID_W, GRID_W), 0)
    kc = lax.broadcasted_iota(jnp.int32, (GRID_W, GRID_W), 1)
    dc = jnp.clip(kc - qc, -(NA_COLS - 1), NA_COLS - 1) + (NA_COLS - 1)
    c0 = jnp.clip(qc - NA_COLS // 2, 0, GRID_W - NA_COLS)
    n_dc = 2 * NA_COLS - 1
    n_dr = 2 * NA_ROWS - 1
    for dr in range(n_dr):
        acc = jnp.zeros((GRID_W, GRID_W), F32)
        for j in range(n_dc):
            acc = jnp.where(dc == j, rpb_ref[(hd * n_dr + dr) * n_dc + j], acc)
        o_ref[0, dr] = jnp.where(kc >= c0, jnp.where(kc < c0 + NA_COLS, acc, NEG_INF), NEG_INF)


def _na_bias(rpb):
    n_dr = 2 * NA_ROWS - 1
    return pl.pallas_call(
        _na_bias_kernel,
        grid=(H_D,),
        in_specs=[pl.BlockSpec(memory_space=pltpu.SMEM)],
        out_specs=pl.BlockSpec((1, n_dr, GRID_W, GRID_W), lambda h: (h, 0, 0, 0)),
        out_shape=jax.ShapeDtypeStruct((H_D, n_dr, GRID_W, GRID_W), F32),
        compiler_params=_params(("arbitrary",)),
        name="na_bias",
    )(rpb.reshape(-1))


def _na_window_start(r):
    return jnp.clip(r - NA_ROWS // 2, 0, GRID_ROWS - NA_ROWS)


def _na_kernel(q_ref, k_ref, v_ref, ck_ref, cv_ref, bias_ref, prev_ref, o_ref):
    r = pl.program_id(1)
    start = pl.multiple_of(_na_window_start(r) * GRID_W, GRID_W)
    n_loc = NA_ROWS * GRID_W
    scale = D_D ** -0.5
    for hd in range(H_D):
        sl = slice(hd * D_D, (hd + 1) * D_D)
        q = q_ref[:, sl]
        s_loc = _dot_nt(q, k_ref[pl.ds(start, n_loc), sl]) * scale + bias_ref[0, hd]
        s_ctx = _dot_nt(q, ck_ref[0, :, sl].astype(BF16)) * scale
        p = _softmax_segments([s_loc, s_ctx])
        o = _pv(p, [v_ref[pl.ds(start, n_loc), sl], cv_ref[0, :, sl].astype(BF16)])
        o_ref[:, sl] = o.astype(BF16)


def _neighbourhood(q, k, v, ck, cv, bias_win, prev_out):
    n_loc = NA_ROWS * GRID_W
    q0 = T_CTX // GRID_W
    k0 = T_CTX // LAT_S
    qspec = pl.BlockSpec((GRID_W, 512), lambda b, r: (q0 + b * GRID_ROWS + r, 0))
    kspec = pl.BlockSpec((LAT_S, 512), lambda b, r: (k0 + b, 0))
    cspec = pl.BlockSpec((1, PAST, 512), lambda b, r: (b, 0, 0))
    return pl.pallas_call(
        _na_kernel,
        grid=(N_LAT_B, GRID_ROWS),
        in_specs=[qspec, kspec, kspec, cspec, cspec,
                  pl.BlockSpec((1, H_D, GRID_W, n_loc),
                               lambda b, r: (_na_window_start(r) - r + NA_ROWS - 1, 0, 0, 0)),
                  pl.BlockSpec(memory_space=pl.ANY)],
        out_specs=qspec,
        out_shape=jax.ShapeDtypeStruct((T_ALL, 512), BF16),
        input_output_aliases={6: 0},
        compiler_params=_params(("arbitrary", "arbitrary")),
        name="attn_neighbourhood",
    )(q, k, v, ck, cv, bias_win, prev_out)


def _post_kernel(x_ref, mod_ref, o1_ref, o2_ref, wo1_ref, wo2_ref, g1_ref, b1_ref,
                 wq_ref, k1_ref, k2_ref, x1_ref, h2t_ref, s1_ref, s2_ref):
    mod = mod_ref[0, 0]
    y = _dot(o1_ref[...], wo1_ref[...]) + _dot(o2_ref[...], wo2_ref[...])
    x1 = _layer_norm(ALPHA * x_ref[...] + mod[2:3] * y, g1_ref[...], b1_ref[...])
    x1_ref[...] = x1
    h2f = x1 * (1.0 + mod[4:5]) + mod[3:4]
    h2 = h2f.astype(BF16)
    h2t_ref[...] = h2f.T.astype(BF16)
    q = _dot(h2, wq_ref[...]).astype(BF16)
    half = P_DKEY // 2
    k1 = k1_ref[...]
    k2 = k2_ref[...]
    for hd in range(P_HEADS):
        s1_ref[hd] = _dot_nt(k1, q[:, hd * P_DKEY:hd * P_DKEY + half])
        s2_ref[hd] = _dot_nt(k2, q[:, hd * P_DKEY + half:(hd + 1) * P_DKEY])


def _post(x, mods, l, o1, o2, wo1, wo2, g1, b1, wq, k1, k2):
    tok = lambda w: pl.BlockSpec((TM, w), lambda i: (i, 0))
    sspec = pl.BlockSpec((P_HEADS, P_KEYS, TM), lambda i: (0, 0, i))
    return pl.pallas_call(
        _post_kernel,
        grid=(N_TILES,),
        in_specs=[tok(D),
                  pl.BlockSpec((1, 1, 6, D), lambda i: (l, _mod_row(i), 0, 0)),
                  tok(512), tok(512), _full((512, D)), _full((512, D)), _full((1, D)), _full((1, D)),
                  _full((D, P_HEADS * P_DKEY)), _full((P_KEYS, P_DKEY // 2)), _full((P_KEYS, P_DKEY // 2))],
        out_specs=[tok(D), pl.BlockSpec((D, TM), lambda i: (0, i)), sspec, sspec],
        out_shape=[jax.ShapeDtypeStruct((T_ALL, D), F32),
                   jax.ShapeDtypeStruct((D, T_ALL), BF16),
                   jax.ShapeDtypeStruct((P_HEADS, P_KEYS, T_ALL), F32),
                   jax.ShapeDtypeStruct((P_HEADS, P_KEYS, T_ALL), F32)],
        compiler_params=_params(("arbitrary",)),
        name="post_mixer",
    )(x, mods, o1, o2, wo1, wo2, g1, b1, wq, k1, k2)


def _tables_kernel(u_ref, v_ref, ub_ref, vt_ref):
    ub_ref[0] = u_ref[0].astype(BF16)
    vt_ref[0] = v_ref[0].T.astype(BF16)


def _prep_tables(pk_u, pk_v):
    blk = pl.BlockSpec((1, PEER_EB, D), lambda l, e: (l, e, 0))
    return pl.pallas_call(
        _tables_kernel,
        grid=(DEPTH, P_EXPERTS // PEER_EB),
        in_specs=[blk, blk],
        out_specs=[blk, pl.BlockSpec((1, D, PEER_EB), lambda l, e: (l, 0, e))],
        out_shape=[jax.ShapeDtypeStruct((DEPTH, P_EXPERTS, D), BF16),
                   jax.ShapeDtypeStruct((DEPTH, D, P_EXPERTS), BF16)],
        compiler_params=_params(("arbitrary", "arbitrary")),
        name="expert_tables",
    )(pk_u, pk_v)


def _top_values(x, n):
    vals = []
    for _ in range(n):
        m = jnp.max(x, axis=0, keepdims=True)
        vals.append(m)
        x = jnp.where(x == m, NEG_INF, x)
    return vals


PEER_PAIRS = [(a, b) for a in range(PEER_NSEL) for b in range(PEER_NSEL) if (a + 1) * (b + 1) <= PEER_NSEL]
PEER_NCAND = -(-len(PEER_PAIRS) // 8) * 8


def _peer_select(s1_ref, s2_ref, eth_scr, e1_scr, e2_scr, cand_scr):
    n_pad = PEER_NCAND - len(PEER_PAIRS)
    cand_scr[len(PEER_PAIRS):, :] = jnp.full((n_pad, cand_scr.shape[1]), NEG_INF, F32)

    def body(hd, carry):
        s1 = s1_ref[hd]
        s2 = s2_ref[hd]
        v1 = _top_values(s1, PEER_NSEL)
        v2 = _top_values(s2, PEER_NSEL)
        for k, (a, b) in enumerate(PEER_PAIRS):
            cand_scr[k:k + 1, :] = v1[a] + v2[b]
        cs = _top_values(cand_scr[...], PEER_NSEL)
        tau = 0.5 * (cs[P_TOPK - 1] + cs[P_TOPK])
        z = jnp.exp(cs[0] - cs[0])
        for j in range(1, P_TOPK):
            z = z + jnp.exp(cs[j] - cs[0])
        half_inv_z = 0.5 / z
        eth_scr[hd] = jnp.exp((tau - s1) - v2[0]) * half_inv_z
        e1_scr[hd] = jnp.exp(s1 - v1[0])
        e2_scr[hd] = jnp.exp(s2 - v2[0]) * half_inv_z
        return carry

    lax.fori_loop(0, P_HEADS, body, 0)


def _gelu2(a):
    return a * (1.0 + lax.erf(a * (2.0 ** -0.5)))


PEER_LW = 256
PEER_CH = 256


def _peer_kernel(h2t_ref, s1_ref, s2_ref, u_ref, vt_prev_ref, vt_last_ref, x1_ref, mod_ref, g2_ref, b2_ref,
                 o_ref, eth_scr, e1_scr, e2_scr, cand_scr, c_cur, c_prev, acc_scr):
    eb = pl.program_id(1)
    n_eb = pl.num_programs(1)
    rows_per_step = PEER_EB // P_KEYS

    @pl.when(eb == 0)
    def _():
        _peer_select(s1_ref, s2_ref, eth_scr, e1_scr, e2_scr, cand_scr)
        acc_scr[...] = jnp.zeros_like(acc_scr)
        c_prev[...] = jnp.zeros_like(c_prev)

    @pl.when(eb > 0)
    def _():
        c_prev[...] = c_cur[...]

    h2t = h2t_ref[...]
    n_ch = PEER_EB // PEER_CH
    rb = D // n_ch
    act_next = _dot(u_ref[0:PEER_CH, :], h2t)
    for k in range(n_ch):
        act = act_next
        if k + 1 < n_ch:
            act_next = _dot(u_ref[(k + 1) * PEER_CH:(k + 2) * PEER_CH, :], h2t)
        acc_scr[k * rb:(k + 1) * rb, :] += _dot(vt_prev_ref[k * rb:(k + 1) * rb, :], c_prev[...])
        for jj in range(PEER_CH // P_KEYS):
            j = k * (PEER_CH // P_KEYS) + jj
            i1 = eb * rows_per_step + j
            for lc in range(PEER_TT // PEER_LW):
                ls = slice(lc * PEER_LW, (lc + 1) * PEER_LW)
                w = jnp.zeros((P_KEYS, PEER_LW), F32)
                for hd in range(P_HEADS):
                    eth = eth_scr[hd, pl.ds(i1, 1), ls]
                    e1 = e1_scr[hd, pl.ds(i1, 1), ls]
                    e2 = e2_scr[hd, :, ls]
                    w = w + jnp.where(e2 >= eth, e2, 0.0) * e1
                a = act[jj * P_KEYS:(jj + 1) * P_KEYS, ls]
                c_cur[j * P_KEYS:(j + 1) * P_KEYS, ls] = (w * _gelu2(a)).astype(BF16)

    @pl.when(eb == n_eb - 1)
    def _():
        mod = mod_ref[0, 0]
        peer = (acc_scr[...] + _dot(vt_last_ref[...], c_cur[...])).T
        o_ref[...] = _layer_norm(ALPHA * x1_ref[...] + mod[5:6] * peer, g2_ref[...], b2_ref[...])


def _peer(h2t, s1, s2, u_bf, vt_bf, x1, mods, l, g2, b2):
    tiles_ctx = T_CTX // PEER_TT
    tiles_per_lat = LAT_S // PEER_TT
    n_eb = P_EXPERTS // PEER_EB
    mod_row = lambda i: jnp.where(i < tiles_ctx, 0, 1 + (i - tiles_ctx) // tiles_per_lat)
    tok = lambda w: pl.BlockSpec((PEER_TT, w), lambda i, e: (i, 0))
    sspec = pl.BlockSpec((P_HEADS, P_KEYS, PEER_TT), lambda i, e: (0, 0, i))
    sel = pltpu.VMEM((P_HEADS, P_KEYS, PEER_TT), F32)
    return pl.pallas_call(
        _peer_kernel,
        grid=(T_ALL // PEER_TT, n_eb),
        in_specs=[pl.BlockSpec((D, PEER_TT), lambda i, e: (0, i)), sspec, sspec,
                  pl.BlockSpec((None, PEER_EB, D), lambda i, e: (l, e, 0)),
                  pl.BlockSpec((None, D, PEER_EB), lambda i, e: (l, 0, jnp.maximum(e - 1, 0))),
                  pl.BlockSpec((None, D, PEER_EB), lambda i, e: (l, 0, jnp.where(e == n_eb - 1, n_eb - 1, 0))),
                  tok(D),
                  pl.BlockSpec((1, 1, 6, D), lambda i, e: (l, mod_row(i), 0, 0)),
                  pl.BlockSpec((1, D), lambda i, e: (0, 0)),
                  pl.BlockSpec((1, D), lambda i, e: (0, 0))],
        out_specs=tok(D),
        out_shape=jax.ShapeDtypeStruct((T_ALL, D), F32),
        scratch_shapes=[sel, sel, sel,
                        pltpu.VMEM((PEER_NCAND, PEER_TT), F32),
                        pltpu.VMEM((PEER_EB, PEER_TT), BF16),
                        pltpu.VMEM((PEER_EB, PEER_TT), BF16),
                        pltpu.VMEM((D, PEER_TT), F32)],
        compiler_params=_params(("arbitrary", "arbitrary")),
        name="peer",
    )(h2t, s1, s2, u_bf, vt_bf, vt_bf, x1, mods, g2, b2)


def _grid_angles(rot_dim):
    t = jnp.arange(LAT_S)
    row = (t // GRID_W).astype(F32)
    col = (t % GRID_W).astype(F32)
    n_freq = rot_dim // 4
    inv = THETA ** (-jnp.arange(n_freq, dtype=F32) / n_freq)
    return jnp.concatenate([row[:, None] * inv, col[:, None] * inv], axis=-1)


def _rope_tables(rot_dim, width):
    ang = _grid_angles(rot_dim)
    cos = jnp.cos(ang)
    sin = jnp.sin(ang)
    cos_t = jnp.tile(jnp.concatenate([cos, cos], axis=-1), (1, width // rot_dim))
    sin_t = jnp.tile(jnp.concatenate([-sin, sin], axis=-1), (1, width // rot_dim))
    ident_c = jnp.ones((1, TM, width), F32)
    ident_s = jnp.zeros((1, TM, width), F32)
    cos_t = jnp.concatenate([ident_c, cos_t.reshape(TILES_PER_LAT, TM, width)], axis=0)
    sin_t = jnp.concatenate([ident_s, sin_t.reshape(TILES_PER_LAT, TM, width)], axis=0)
    return cos_t, sin_t


def kernel(x_prompt, x_sample, cache_a_k, cache_a_v, cache_b_ckv, cache_b_kr, cache_c_k, cache_c_v, cache_d_k, cache_d_v, c, c_ctx, w_mod, b_mod, ln1_g, ln1_b, ln2_g, ln2_b, ev_w_in, ev_lam_q1, ev_lam_k1, ev_lam_q2, ev_lam_k2, ev_g_sub, ev_g_cq, ev_w_uq, ev_g_ckv, ev_w_ukv, ev_w_out, od_w_in, od_g_qn, od_g_kn, od_rpb, od_w_out, pk_w_q, pk_k1, pk_k2, pk_u, pk_v):
    x = jnp.concatenate([x_prompt.reshape(T_CTX, D), x_sample.reshape(T_LAT, D)], axis=0)
    cvecs = jnp.concatenate([c_ctx[None, :], c, jnp.zeros((8 - 1 - N_LAT_B, D), F32)], axis=0)
    mods = _modulation(cvecs, w_mod, b_mod).reshape(DEPTH, 8, 6, D)
    u_bf, vt_bf = _prep_tables(pk_u, pk_v)

    cos_a, sin_a = _rope_tables(D_A, 512)
    cos_b, sin_b = _rope_tables(B_ROPE, 256)
    rt_even = jnp.concatenate([cos_a, sin_a, cos_b, sin_b], axis=-1)
    rt_odd = jnp.concatenate(_rope_tables(D_C, 512), axis=-1)
    ones_bd = jnp.kron(jnp.eye(512 // D_C, dtype=F32), jnp.ones((D_C, D_C), F32)).astype(BF16)
    lat3 = lambda a, w: a.reshape(N_LAT_B, PAST, w)

    states = {}
    for l in range(DEPTH):
        j = l // 2
        if l % 2 == 0:
            lam_init = 0.8 - 0.6 * math.exp(-0.3 * l)
            w_in = jnp.pad(ev_w_in[j], ((0, 0), (0, EVEN_IN_PAD - ev_w_in.shape[-1]))).astype(BF16)
            wuq = ev_w_uq[j].reshape(B_QRANK, H_B, B_NOPE + B_ROPE)
            wuq = jnp.concatenate([wuq[:, :, :B_NOPE].reshape(B_QRANK, -1),
                                   wuq[:, :, B_NOPE:].reshape(B_QRANK, -1)], axis=1).astype(BF16)
            wukv = ev_w_ukv[j].reshape(B_KVRANK, H_B, B_NOPE + B_V)
            wukv = jnp.concatenate([wukv[:, :, :B_NOPE].reshape(B_KVRANK, -1),
                                    wukv[:, :, B_NOPE:].reshape(B_KVRANK, -1)], axis=1).astype(BF16)
            (qa, bq, ka, va, kvb, kr,
             states["a_k"], states["a_v"], states["b_ckv"], states["b_kr"]) = _proj_call(
                _proj_even_kernel, "proj_even", x, mods, l, rt_even,
                [w_in, ev_g_cq[j][None, :], wuq, ev_g_ckv[j][None, :], wukv],
                (512, 768, 512, 512, 1024, B_ROPE), (512, 512, B_KVRANK, B_ROPE))
            lam_vecs = [v[j][None, :] for v in (ev_lam_q1, ev_lam_k1, ev_lam_q2, ev_lam_k2)]
            g_sub = ev_g_sub[j][None, :]
            proj = (qa, bq, ka, va, kvb, kr)
            outs = _attn_even(lam_vecs, g_sub, lam_init, *proj)
            cache_kvb = _matmul(cache_b_ckv[:, j].reshape(N_LAT_B * PAST, B_KVRANK), wukv)
            cache = (lat3(cache_a_k[:, j], 512), lat3(cache_a_v[:, j], 512), cache_kvb,
                     lat3(cache_b_kr[:, j], B_ROPE))
            o1, o2 = _attn_even(lam_vecs, g_sub, lam_init, *proj, cache=cache, prev_out=outs)
            w_out = ev_w_out[j]
        else:
            w_in = od_w_in[j].astype(BF16)
            g_qn = jnp.tile(od_g_qn[j], H_C)[None, :]
            g_kn = jnp.tile(od_g_kn[j], G_C)[None, :]
            (cq, ck, cv, dq, dk, dv,
             states["c_k"], states["c_v"], states["d_k"], states["d_v"]) = _proj_call(
                _proj_odd_kernel, "proj_odd", x, mods, l, rt_odd, [w_in, g_qn, g_kn, ones_bd],
                (512, 128, 128, 512, 512, 512), (128, 128, 512, 512))
            o1 = _gqa(cq, ck, cv)
            o1 = _gqa(cq, ck, cv, cache=(lat3(cache_c_k[:, j], 128), lat3(cache_c_v[:, j], 128)),
                      prev_out=[o1])
            o2 = _mha(dq, dk, dv)
            bias = _na_bias(od_rpb[j])
            offs = jnp.arange(NA_ROWS + 1)[:, None] + jnp.arange(NA_ROWS)[None, :]
            bias_win = jnp.transpose(bias[:, offs], (1, 0, 3, 2, 4)).reshape(
                NA_ROWS + 1, H_D, GRID_W, NA_ROWS * GRID_W)
            o2 = _neighbourhood(dq, dk, dv, lat3(cache_d_k[:, j], 512), lat3(cache_d_v[:, j], 512),
                                bias_win, o2)
            w_out = od_w_out[j]

        x1, h2t, s1, s2 = _post(x, mods, l, o1, o2, w_out[:512].astype(BF16), w_out[512:].astype(BF16),
                                ln1_g[l][None, :], ln1_b[l][None, :], pk_w_q[l].astype(BF16),
                                pk_k1[l].astype(BF16), pk_k2[l].astype(BF16))
        x = _peer(h2t, s1, s2, u_bf, vt_bf, x1, mods, l, ln2_g[l][None, :], ln2_b[l][None, :])

    y = x[:T_CTX].reshape(N_CTX_B, CTX_S, D)
    z = x[T_CTX:].reshape(N_LAT_B, LAT_S, D)
    n_even = (DEPTH + 1) // 2
    n_odd = DEPTH // 2
    return (y, z,
            states["a_k"].reshape(N_CTX_B, n_even, CTX_S, H_A, 2, D_A),
            states["a_v"].reshape(N_CTX_B, n_even, CTX_S, H_A, 2 * D_A),
            states["b_ckv"].reshape(N_CTX_B, n_even, CTX_S, B_KVRANK),
            states["b_kr"].reshape(N_CTX_B, n_even, CTX_S, B_ROPE),
            states["c_k"].reshape(N_CTX_B, n_odd, CTX_S, G_C, D_C),
            states["c_v"].reshape(N_CTX_B, n_odd, CTX_S, G_C, D_C),
            states["d_k"].reshape(N_CTX_B, n_odd, CTX_S, H_D, D_D),
            states["d_v"].reshape(N_CTX_B, n_odd, CTX_S, H_D, D_D))
```

```python
import functools
import math

import jax
import jax.numpy as jnp
from jax import lax
from jax.experimental import pallas as pl
from jax.experimental.pallas import tpu as pltpu

F32 = jnp.float32
BF16 = jnp.bfloat16

D = 1024
N_CTX_B = 32
CTX_S = 256
N_LAT_B = 4
LAT_S = 1024
PAST = 256
T_CTX = N_CTX_B * CTX_S
T_LAT = N_LAT_B * LAT_S
T_ALL = T_CTX + T_LAT
DEPTH = 2
GRID_W = 64
GRID_ROWS = LAT_S // GRID_W
THETA = 10000.0
EPS = 1e-6
ALPHA = (2 * DEPTH) ** 0.25

H_A, D_A = 4, 64
H_B, B_NOPE, B_ROPE, B_V, B_QRANK, B_KVRANK = 8, 64, 32, 64, 256, 128
H_C, G_C, D_C = 8, 2, 64
H_D, D_D = 8, 64
NA_ROWS, NA_COLS = 8, 16
P_HEADS, P_KEYS, P_DKEY, P_TOPK = 8, 128, 256, 16
P_EXPERTS = P_KEYS * P_KEYS

EVEN_IN_PAD = 2048
ODD_IN = 2304

TM = 256
N_TILES = T_ALL // TM
N_CTX_TILES = T_CTX // TM
TILES_PER_LAT = LAT_S // TM
N_LAT_TILES = T_LAT // TM
TQ = 256
PEER_TT = 512
PEER_EB = 1024
PEER_NSEL = P_TOPK + 1
VMEM_LIMIT = 56 * 1024 * 1024

NEG_INF = float("-inf")


def _dot(a, b):
    return jnp.dot(a, b, preferred_element_type=F32)


def _dot_nt(a, b):
    return lax.dot_general(a, b, (((1,), (1,)), ((), ())), preferred_element_type=F32)


def _layer_norm(x, g, b):
    mu = jnp.mean(x, axis=-1, keepdims=True)
    xc = x - mu
    var = jnp.mean(xc * xc, axis=-1, keepdims=True)
    return xc * lax.rsqrt(var + EPS) * g + b


def _rms_norm(x, g):
    return x * lax.rsqrt(jnp.mean(x * x, axis=-1, keepdims=True) + EPS) * g


def _chunk_rms_norm(x, g, ones_bd, chunk):
    x2 = x * x
    hi = x2.astype(BF16)
    lo = (x2 - hi.astype(F32)).astype(BF16)
    ms = (_dot(hi, ones_bd) + _dot(lo, ones_bd)) * (1.0 / chunk)
    return x * lax.rsqrt(ms + EPS) * g


def _rope(x, cos_t, sin_t, chunk):
    n = x.shape[-1]
    half = chunk // 2
    lane = lax.broadcasted_iota(jnp.int32, x.shape, x.ndim - 1)
    swapped = jnp.where((lane % chunk) < half,
                        pltpu.roll(x, n - half, x.ndim - 1),
                        pltpu.roll(x, half, x.ndim - 1))
    return x * cos_t + swapped * sin_t


def _params(sem):
    return pltpu.CompilerParams(dimension_semantics=sem, vmem_limit_bytes=VMEM_LIMIT)


def _full(shape):
    zeros = (0,) * len(shape)
    return pl.BlockSpec(shape, lambda *_: zeros)


def _mod_row(i):
    return jnp.where(i < N_CTX_TILES, 0, 1 + (i - N_CTX_TILES) // TILES_PER_LAT)


def _rope_blk(i):
    return jnp.where(i < N_CTX_TILES, 0, 1 + (i - N_CTX_TILES) % TILES_PER_LAT)


def _mod_kernel(c_ref, w_ref, b_ref, o_ref):
    c = c_ref[...]
    a = c * (1.0 / (1.0 + jnp.exp(-c)))
    o_ref[0] = _dot(a.astype(BF16), w_ref[0].astype(BF16)) + b_ref[0]


def _modulation(cvecs, w_mod, b_mod):
    nb = 6
    bn = 6 * D // nb
    return pl.pallas_call(
        _mod_kernel,
        grid=(DEPTH, nb),
        in_specs=[pl.BlockSpec((8, D), lambda l, j: (0, 0)),
                  pl.BlockSpec((1, D, bn), lambda l, j: (l, 0, j)),
                  pl.BlockSpec((1, 1, bn), lambda l, j: (l, 0, j))],
        out_specs=pl.BlockSpec((1, 8, bn), lambda l, j: (l, 0, j)),
        out_shape=jax.ShapeDtypeStruct((DEPTH, 8, 6 * D), F32),
        compiler_params=_params(("arbitrary", "arbitrary")),
        name="modulation",
    )(cvecs, w_mod, b_mod.reshape(DEPTH, 1, 6 * D))


def _tile(g):
    return (g + N_CTX_TILES) % N_TILES


def _state_blk(g):
    return jnp.maximum(g - N_LAT_TILES, 0)


def _proj_even_kernel(x_ref, mod_ref, rt_ref, win_ref, gcq_ref, wuq_ref, gckv_ref, wukv_ref,
                      qa_ref, bq_ref, ka_ref, va_ref, kvb_ref, kr_ref,
                      s_ak_ref, s_av_ref, s_ckv_ref, s_kr_ref):
    x = x_ref[...]
    mod = mod_ref[0, 0]
    h = x * (1.0 + mod[1:2]) + mod[0:1]
    p = _dot(h.astype(BF16), win_ref[...])
    rt = rt_ref[0]
    cos_a, sin_a = rt[:, 0:512], rt[:, 512:1024]
    cos_b, sin_b = rt[:, 1024:1280], rt[:, 1280:1536]
    qa_ref[...] = _rope(p[:, 0:512], cos_a, sin_a, D_A).astype(BF16)
    ka = _rope(p[:, 512:1024], cos_a, sin_a, D_A)
    ka_ref[...] = ka.astype(BF16)
    s_ak_ref[...] = ka
    va = p[:, 1024:1536]
    va_ref[...] = va.astype(BF16)
    s_av_ref[...] = va
    cq = _rms_norm(p[:, 1536:1792], gcq_ref[...])
    bq = _dot(cq.astype(BF16), wuq_ref[...])
    bq_ref[:, 0:512] = bq[:, 0:512].astype(BF16)
    bq_ref[:, 512:768] = _rope(bq[:, 512:768], cos_b, sin_b, B_ROPE).astype(BF16)
    ckv = _rms_norm(p[:, 1792:1920], gckv_ref[...])
    s_ckv_ref[...] = ckv
    kvb_ref[...] = _dot(ckv.astype(BF16), wukv_ref[...]).astype(BF16)
    kr = _rope(p[:, 1920:2048], cos_b[:, 0:128], sin_b[:, 0:128], B_ROPE)[:, 0:B_ROPE]
    kr_ref[...] = kr.astype(BF16)
    s_kr_ref[...] = kr


def _proj_call(kernel_fn, name, x, mods, l, rt, weights, widths, state_widths):
    tok = lambda w: pl.BlockSpec((TM, w), lambda g: (_tile(g), 0))
    st = lambda w: pl.BlockSpec((TM, w), lambda g: (_state_blk(g), 0))
    return pl.pallas_call(
        kernel_fn,
        grid=(N_TILES,),
        in_specs=[tok(D),
                  pl.BlockSpec((1, 1, 6, D), lambda g: (l, _mod_row(_tile(g)), 0, 0)),
                  pl.BlockSpec((1, TM, rt.shape[-1]), lambda g: (_rope_blk(_tile(g)), 0, 0))]
                 + [_full(w.shape) for w in weights],
        out_specs=[tok(w) for w in widths] + [st(w) for w in state_widths],
        out_shape=[jax.ShapeDtypeStruct((T_ALL, w), BF16) for w in widths]
                  + [jax.ShapeDtypeStruct((T_CTX, w), F32) for w in state_widths],
        compiler_params=_params(("arbitrary",)),
        name=name,
    )(x, mods, rt, *weights)


def _proj_odd_kernel(x_ref, mod_ref, rt_ref, win_ref, gq_ref, gk_ref, bd_ref,
                     cq_ref, ck_ref, cv_ref, dq_ref, dk_ref, dv_ref,
                     s_ck_ref, s_cv_ref, s_dk_ref, s_dv_ref):
    x = x_ref[...]
    mod = mod_ref[0, 0]
    h = x * (1.0 + mod[1:2]) + mod[0:1]
    p = _dot(h.astype(BF16), win_ref[...])
    rt = rt_ref[0]
    cos_c, sin_c = rt[:, 0:512], rt[:, 512:1024]
    bd = bd_ref[...]
    cq = _chunk_rms_norm(p[:, 0:512], gq_ref[...], bd, D_C)
    cq_ref[...] = _rope(cq, cos_c, sin_c, D_C).astype(BF16)
    ck = _chunk_rms_norm(p[:, 512:640], gk_ref[...], bd[0:128, 0:128], D_C)
    ck = _rope(ck, cos_c[:, 0:128], sin_c[:, 0:128], D_C)
    ck_ref[...] = ck.astype(BF16)
    s_ck_ref[...] = ck
    cv = p[:, 640:768]
    cv_ref[...] = cv.astype(BF16)
    s_cv_ref[...] = cv
    dq_ref[...] = p[:, 768:1280].astype(BF16)
    dk = p[:, 1280:1792]
    dk_ref[...] = dk.astype(BF16)
    s_dk_ref[...] = dk
    dv = p[:, 1792:2304]
    dv_ref[...] = dv.astype(BF16)
    s_dv_ref[...] = dv


def _matmul_kernel(x_ref, w_ref, o_ref):
    o_ref[...] = _dot(x_ref[...].astype(BF16), w_ref[...]).astype(BF16)


def _matmul(x, w):
    m, k = x.shape
    n = w.shape[1]
    return pl.pallas_call(
        _matmul_kernel,
        grid=(m // TM,),
        in_specs=[pl.BlockSpec((TM, k), lambda i: (i, 0)), _full((k, n))],
        out_specs=pl.BlockSpec((TM, n), lambda i: (i, 0)),
        out_shape=jax.ShapeDtypeStruct((m, n), BF16),
        compiler_params=_params(("arbitrary",)),
        name="cache_kv_up",
    )(x, w)


def _exp_segments(segs):
    m = jnp.max(segs[0], axis=-1, keepdims=True)
    for s in segs[1:]:
        m = jnp.maximum(m, jnp.max(s, axis=-1, keepdims=True))
    es = [jnp.exp(s - m) for s in segs]
    den = jnp.sum(es[0], axis=-1, keepdims=True)
    for e in es[1:]:
        den = den + jnp.sum(e, axis=-1, keepdims=True)
    return es, den


def _pv(ps, vs):
    o = _dot(ps[0].astype(BF16), vs[0])
    for p, v in zip(ps[1:], vs[1:]):
        o = o + _dot(p.astype(BF16), v)
    return o


def _attend(segs, vs):
    es, den = _exp_segments(segs)
    return _pv(es, vs) / den


def _scaled(q, scale):
    return (q.astype(F32) * scale).astype(q.dtype)


def _attn_specs(row0, n_req, n_q, widths_q, widths_k, cache_shapes):
    n_own = n_q
    nqb = n_q // TQ
    q0 = row0 // TQ
    k0 = row0 // n_own
    qspec = lambda w: pl.BlockSpec((TQ, w), lambda b, j: (q0 + b * nqb + j, 0))
    kspec = lambda w: pl.BlockSpec((n_own, w), lambda b, j: (k0 + b, 0))
    cspec = lambda shp: pl.BlockSpec((1,) + tuple(shp[1:]), lambda b, j: (b,) + (0,) * (len(shp) - 1))
    specs = [qspec(w) for w in widths_q] + [kspec(w) for w in widths_k] + [cspec(s) for s in cache_shapes]
    return specs, qspec, (n_req, nqb)


def _attn_even_kernel(*refs, lam_init, has_cache):
    lq1_ref, lk1_ref, lq2_ref, lk2_ref, gsub_ref, qa_ref, bq_ref, ka_ref, va_ref, kvb_ref, kr_ref = refs[:11]
    if has_cache:
        cka_ref, cva_ref, ckvb_ref, ckr_ref = refs[11:15]
    oa_ref, ob_ref = refs[-2:]
    lam = (jnp.exp(jnp.sum(lq1_ref[...] * lk1_ref[...], axis=-1, keepdims=True))
           - jnp.exp(jnp.sum(lq2_ref[...] * lk2_ref[...], axis=-1, keepdims=True)) + lam_init)
    gsub = gsub_ref[...]
    for hd in range(H_A):
        vsl = slice(hd * 2 * D_A, (hd + 1) * 2 * D_A)
        vs = [va_ref[:, vsl]] + ([cva_ref[0, :, vsl].astype(BF16)] if has_cache else [])
        os = []
        for m in range(2):
            sl = slice(hd * 2 * D_A + m * D_A, hd * 2 * D_A + (m + 1) * D_A)
            q = _scaled(qa_ref[:, sl], D_A ** -0.5)
            segs = [_dot_nt(q, ka_ref[:, sl])]
            if has_cache:
                segs.append(_dot_nt(q, cka_ref[0, :, sl].astype(BF16)))
            os.append(_attend(segs, vs))
        o = os[0] - lam * os[1]
        oa_ref[:, vsl] = (_rms_norm(o, gsub) * (1.0 - lam_init)).astype(BF16)
    kr = kr_ref[...]
    if has_cache:
        ckr = ckr_ref[0].astype(BF16)
    sc = (B_NOPE + B_ROPE) ** -0.5
    for hd in range(H_B):
        nsl = slice(hd * B_NOPE, (hd + 1) * B_NOPE)
        vsl = slice(512 + hd * B_V, 512 + (hd + 1) * B_V)
        qn = bq_ref[:, nsl]
        qr = bq_ref[:, 512 + hd * B_ROPE:512 + (hd + 1) * B_ROPE]
        segs = [(_dot_nt(qn, kvb_ref[:, nsl]) + _dot_nt(qr, kr)) * sc]
        vs = [kvb_ref[:, vsl]]
        if has_cache:
            segs.append((_dot_nt(qn, ckvb_ref[0, :, nsl]) + _dot_nt(qr, ckr)) * sc)
            vs.append(ckvb_ref[0, :, vsl])
        ob_ref[:, hd * B_V:(hd + 1) * B_V] = _attend(segs, vs).astype(BF16)


def _attn_call(kernel_fn, name, params, arrays, widths_q, widths_k, latent, cache, prev_out, n_out):
    row0, n_req, n_q = (T_CTX, N_LAT_B, LAT_S) if latent else (0, N_CTX_B, CTX_S)
    cache = list(cache)
    cache_shapes = []
    for a in cache:
        cache_shapes.append(a.shape if a.ndim == 3 else (n_req,) + (a.shape[0] // n_req, a.shape[1]))
    cache = [a.reshape(shp) for a, shp in zip(cache, cache_shapes)]
    specs, qspec, grid = _attn_specs(row0, n_req, n_q, widths_q, widths_k, cache_shapes)
    in_specs = [_full(p.shape) for p in params] + specs
    operands = list(params) + list(arrays) + cache
    aliases = {}
    if prev_out is not None:
        for i, o in enumerate(prev_out):
            aliases[len(operands)] = i
            operands.append(o)
            in_specs.append(pl.BlockSpec(memory_space=pl.ANY))
    return pl.pallas_call(
        kernel_fn,
        grid=grid,
        in_specs=in_specs,
        out_specs=[qspec(512)] * n_out,
        out_shape=[jax.ShapeDtypeStruct((T_ALL, 512), BF16)] * n_out,
        input_output_aliases=aliases,
        compiler_params=_params(("arbitrary", "arbitrary")),
        name=name,
    )(*operands)


def _attn_even(lam_vecs, g_sub, lam_init, qa, bq, ka, va, kvb, kr, cache=(), prev_out=None):
    latent = prev_out is not None
    fn = functools.partial(_attn_even_kernel, lam_init=lam_init, has_cache=latent)
    return _attn_call(fn, "attn_even", list(lam_vecs) + [g_sub], [qa, bq, ka, va, kvb, kr],
                      (512, 768), (512, 512, 1024, B_ROPE), latent, cache, prev_out, 2)


def _gqa_kernel(*refs, has_cache):
    cq_ref, ck_ref, cv_ref = refs[:3]
    if has_cache:
        cck_ref, ccv_ref = refs[3:5]
    o_ref = refs[-1]
    for hd in range(H_C):
        g = hd // (H_C // G_C)
        gsl = slice(g * D_C, (g + 1) * D_C)
        q = _scaled(cq_ref[:, hd * D_C:(hd + 1) * D_C], D_C ** -0.5)
        segs = [_dot_nt(q, ck_ref[:, gsl])]
        vs = [cv_ref[:, gsl]]
        if has_cache:
            segs.append(_dot_nt(q, cck_ref[0, :, gsl].astype(BF16)))
            vs.append(ccv_ref[0, :, gsl].astype(BF16))
        o_ref[:, hd * D_C:(hd + 1) * D_C] = _attend(segs, vs).astype(BF16)


def _gqa(cq, ck, cv, cache=(), prev_out=None):
    latent = prev_out is not None
    fn = functools.partial(_gqa_kernel, has_cache=latent)
    return _attn_call(fn, "attn_gqa", [], [cq, ck, cv], (512,), (128, 128), latent, cache, prev_out, 1)[0]


def _mha_kernel(q_ref, k_ref, v_ref, o_ref):
    for hd in range(H_D):
        sl = slice(hd * D_D, (hd + 1) * D_D)
        segs = [_dot_nt(_scaled(q_ref[:, sl], D_D ** -0.5), k_ref[:, sl])]
        o_ref[:, sl] = _attend(segs, [v_ref[:, sl]]).astype(BF16)


def _mha(q, k, v):
    return _attn_call(_mha_kernel, "attn_mha", [], [q, k, v], (512,), (512, 512), False, (), None, 1)[0]


def _na_bias_kernel(rpb_ref, o_ref):
    hd = pl.program_id(0)
    qc = lax.broadcasted_iota(jnp.int32, (GRID_W, GRID_W), 0)
    kc = lax.broadcasted_iota(jnp.int32, (GRID_W, GRID_W), 1)
    dc = jnp.clip(kc - qc, -(NA_COLS - 1), NA_COLS - 1) + (NA_COLS - 1)
    c0 = jnp.clip(qc - NA_COLS // 2, 0, GRID_W - NA_COLS)
    n_dc = 2 * NA_COLS - 1
    n_dr = 2 * NA_ROWS - 1
    for dr in range(n_dr):
        acc = jnp.zeros((GRID_W, GRID_W), F32)
        for j in range(n_dc):
            acc = jnp.where(dc == j, rpb_ref[(hd * n_dr + dr) * n_dc + j], acc)
        o_ref[0, dr] = jnp.where(kc >= c0, jnp.where(kc < c0 + NA_COLS, acc, NEG_INF), NEG_INF)


def _na_bias(rpb):
    n_dr = 2 * NA_ROWS - 1
    return pl.pallas_call(
        _na_bias_kernel,
        grid=(H_D,),
        in_specs=[pl.BlockSpec(memory_space=pltpu.SMEM)],
        out_specs=pl.BlockSpec((1, n_dr, GRID_W, GRID_W), lambda h: (h, 0, 0, 0)),
        out_shape=jax.ShapeDtypeStruct((H_D, n_dr, GRID_W, GRID_W), F32),
        compiler_params=_params(("arbitrary",)),
        name="na_bias",
    )(rpb.reshape(-1))


def _na_window_start(r):
    return jnp.clip(r - NA_ROWS // 2, 0, GRID_ROWS - NA_ROWS)


def _na_kernel(q_ref, k_ref, v_ref, ck_ref, cv_ref, bias_ref, prev_ref, o_ref):
    r = pl.program_id(1)
    start = pl.multiple_of(_na_window_start(r) * GRID_W, GRID_W)
    n_loc = NA_ROWS * GRID_W
    scale = D_D ** -0.5
    heads = [slice(hd * D_D, (hd + 1) * D_D) for hd in range(H_D)]
    segs = []
    for hd, sl in enumerate(heads):
        q = _scaled(q_ref[:, sl], scale)
        segs.append([_dot_nt(q, k_ref[pl.ds(start, n_loc), sl]) + bias_ref[0, hd],
                     _dot_nt(q, ck_ref[0, :, sl].astype(BF16))])
    exps = [_exp_segments(s) for s in segs]
    for sl, (es, den) in zip(heads, exps):
        o = _pv(es, [v_ref[pl.ds(start, n_loc), sl], cv_ref[0, :, sl].astype(BF16)]) / den
        o_ref[:, sl] = o.astype(BF16)


def _neighbourhood(q, k, v, ck, cv, bias_win, prev_out):
    n_loc = NA_ROWS * GRID_W
    q0 = T_CTX // GRID_W
    k0 = T_CTX // LAT_S
    qspec = pl.BlockSpec((GRID_W, 512), lambda b, r: (q0 + b * GRID_ROWS + r, 0))
    kspec = pl.BlockSpec((LAT_S, 512), lambda b, r: (k0 + b, 0))
    cspec = pl.BlockSpec((1, PAST, 512), lambda b, r: (b, 0, 0))
    return pl.pallas_call(
        _na_kernel,
        grid=(N_LAT_B, GRID_ROWS),
        in_specs=[qspec, kspec, kspec, cspec, cspec,
                  pl.BlockSpec((1, H_D, GRID_W, n_loc),
                               lambda b, r: (_na_window_start(r) - r + NA_ROWS - 1, 0, 0, 0)),
                  pl.BlockSpec(memory_space=pl.ANY)],
        out_specs=qspec,
        out_shape=jax.ShapeDtypeStruct((T_ALL, 512), BF16),
        input_output_aliases={6: 0},
        compiler_params=_params(("arbitrary", "arbitrary")),
        name="attn_neighbourhood",
    )(q, k, v, ck, cv, bias_win, prev_out)


def _post_kernel(x_ref, mod_ref, o1_ref, o2_ref, wo1_ref, wo2_ref, g1_ref, b1_ref,
                 wq_ref, k1_ref, k2_ref, x1_ref, h2t_ref, s1_ref, s2_ref):
    mod = mod_ref[0, 0]
    y = _dot(o1_ref[...], wo1_ref[...]) + _dot(o2_ref[...], wo2_ref[...])
    x1 = _layer_norm(ALPHA * x_ref[...] + mod[2:3] * y, g1_ref[...], b1_ref[...])
    x1_ref[...] = x1
    h2f = x1 * (1.0 + mod[4:5]) + mod[3:4]
    h2 = h2f.astype(BF16)
    h2t_ref[...] = h2f.T.astype(BF16)
    q = _dot(h2, wq_ref[...]).astype(BF16)
    half = P_DKEY // 2
    k1 = k1_ref[...]
    k2 = k2_ref[...]
    for hd in range(P_HEADS):
        s1_ref[hd] = _dot_nt(k1, q[:, hd * P_DKEY:hd * P_DKEY + half])
        s2_ref[hd] = _dot_nt(k2, q[:, hd * P_DKEY + half:(hd + 1) * P_DKEY])


def _post(x, mods, l, o1, o2, wo1, wo2, g1, b1, wq, k1, k2):
    tok = lambda w: pl.BlockSpec((TM, w), lambda i: (i, 0))
    sspec = pl.BlockSpec((P_HEADS, P_KEYS, TM), lambda i: (0, 0, i))
    return pl.pallas_call(
        _post_kernel,
        grid=(N_TILES,),
        in_specs=[tok(D),
                  pl.BlockSpec((1, 1, 6, D), lambda i: (l, _mod_row(i), 0, 0)),
                  tok(512), tok(512), _full((512, D)), _full((512, D)), _full((1, D)), _full((1, D)),
                  _full((D, P_HEADS * P_DKEY)), _full((P_KEYS, P_DKEY // 2)), _full((P_KEYS, P_DKEY // 2))],
        out_specs=[tok(D), pl.BlockSpec((D, TM), lambda i: (0, i)), sspec, sspec],
        out_shape=[jax.ShapeDtypeStruct((T_ALL, D), F32),
                   jax.ShapeDtypeStruct((D, T_ALL), BF16),
                   jax.ShapeDtypeStruct((P_HEADS, P_KEYS, T_ALL), F32),
                   jax.ShapeDtypeStruct((P_HEADS, P_KEYS, T_ALL), F32)],
        compiler_params=_params(("arbitrary",)),
        name="post_mixer",
    )(x, mods, o1, o2, wo1, wo2, g1, b1, wq, k1, k2)


def _tables_kernel(u_ref, v_ref, ub_ref, vt_ref):
    ub_ref[0] = u_ref[0].astype(BF16)
    vt_ref[0] = v_ref[0].T.astype(BF16)


def _prep_tables(pk_u, pk_v):
    blk = pl.BlockSpec((1, PEER_EB, D), lambda l, e: (l, e, 0))
    return pl.pallas_call(
        _tables_kernel,
        grid=(DEPTH, P_EXPERTS // PEER_EB),
        in_specs=[blk, blk],
        out_specs=[blk, pl.BlockSpec((1, D, PEER_EB), lambda l, e: (l, 0, e))],
        out_shape=[jax.ShapeDtypeStruct((DEPTH, P_EXPERTS, D), BF16),
                   jax.ShapeDtypeStruct((DEPTH, D, P_EXPERTS), BF16)],
        compiler_params=_params(("arbitrary", "arbitrary")),
        name="expert_tables",
    )(pk_u, pk_v)


def _top_values(x, n):
    vals = []
    for _ in range(n):
        m = jnp.max(x, axis=0, keepdims=True)
        vals.append(m)
        x = jnp.where(x == m, NEG_INF, x)
    return vals


PEER_PAIRS = [(a, b) for a in range(PEER_NSEL) for b in range(PEER_NSEL) if (a + 1) * (b + 1) <= PEER_NSEL]
PEER_NCAND = -(-len(PEER_PAIRS) // 8) * 8


def _peer_select(s1_ref, s2_ref, eth_scr, e1_scr, e2_scr, cand_scr):
    n_pad = PEER_NCAND - len(PEER_PAIRS)
    cand_scr[len(PEER_PAIRS):, :] = jnp.full((n_pad, cand_scr.shape[1]), NEG_INF, F32)

    def body(hd, carry):
        s1 = s1_ref[hd]
        s2 = s2_ref[hd]
        v1 = _top_values(s1, PEER_NSEL)
        v2 = _top_values(s2, PEER_NSEL)
        for k, (a, b) in enumerate(PEER_PAIRS):
            cand_scr[k:k + 1, :] = v1[a] + v2[b]
        cs = _top_values(cand_scr[...], PEER_NSEL)
        tau = 0.5 * (cs[P_TOPK - 1] + cs[P_TOPK])
        z = jnp.exp(cs[0] - cs[0])
        for j in range(1, P_TOPK):
            z = z + jnp.exp(cs[j] - cs[0])
        half_inv_z = 0.5 / z
        eth_scr[hd] = jnp.exp((tau - s1) - v2[0]) * half_inv_z
        e1_scr[hd] = jnp.exp(s1 - v1[0])
        e2_scr[hd] = jnp.exp(s2 - v2[0]) * half_inv_z
        return carry

    lax.fori_loop(0, P_HEADS, body, 0)


def _gelu2(a):
    return a * (1.0 + lax.erf(a * (2.0 ** -0.5)))


PEER_LW = 256
PEER_CH = 256


def _peer_kernel(h2t_ref, s1_ref, s2_ref, u_ref, vt_prev_ref, vt_last_ref, x1_ref, mod_ref, g2_ref, b2_ref,
                 o_ref, eth_scr, e1_scr, e2_scr, cand_scr, c_cur, c_prev, acc_scr):
    eb = pl.program_id(1)
    n_eb = pl.num_programs(1)
    rows_per_step = PEER_EB // P_KEYS

    @pl.when(eb == 0)
    def _():
        _peer_select(s1_ref, s2_ref, eth_scr, e1_scr, e2_scr, cand_scr)
        acc_scr[...] = jnp.zeros_like(acc_scr)
        c_prev[...] = jnp.zeros_like(c_prev)

    @pl.when(eb > 0)
    def _():
        c_prev[...] = c_cur[...]

    h2t = h2t_ref[...]
    n_ch = PEER_EB // PEER_CH
    rb = D // n_ch
    act_next = _dot(u_ref[0:PEER_CH, :], h2t)
    for k in range(n_ch):
        act = act_next
        if k + 1 < n_ch:
            act_next = _dot(u_ref[(k + 1) * PEER_CH:(k + 2) * PEER_CH, :], h2t)
        acc_scr[k * rb:(k + 1) * rb, :] += _dot(vt_prev_ref[k * rb:(k + 1) * rb, :], c_prev[...])
        for jj in range(PEER_CH // P_KEYS):
            j = k * (PEER_CH // P_KEYS) + jj
            i1 = eb * rows_per_step + j
            for lc in range(PEER_TT // PEER_LW):
                ls = slice(lc * PEER_LW, (lc + 1) * PEER_LW)
                w = jnp.zeros((P_KEYS, PEER_LW), F32)
                for hd in range(P_HEADS):
                    eth = eth_scr[hd, pl.ds(i1, 1), ls]
                    e1 = e1_scr[hd, pl.ds(i1, 1), ls]
                    e2 = e2_scr[hd, :, ls]
                    w = w + jnp.where(e2 >= eth, e2, 0.0) * e1
                a = act[jj * P_KEYS:(jj + 1) * P_KEYS, ls]
                c_cur[j * P_KEYS:(j + 1) * P_KEYS, ls] = (w * _gelu2(a)).astype(BF16)

    @pl.when(eb == n_eb - 1)
    def _():
        mod = mod_ref[0, 0]
        peer = (acc_scr[...] + _dot(vt_last_ref[...], c_cur[...])).T
        o_ref[...] = _layer_norm(ALPHA * x1_ref[...] + mod[5:6] * peer, g2_ref[...], b2_ref[...])


def _peer(h2t, s1, s2, u_bf, vt_bf, x1, mods, l, g2, b2):
    tiles_ctx = T_CTX // PEER_TT
    tiles_per_lat = LAT_S // PEER_TT
    n_eb = P_EXPERTS // PEER_EB
    mod_row = lambda i: jnp.where(i < tiles_ctx, 0, 1 + (i - tiles_ctx) // tiles_per_lat)
    tok = lambda w: pl.BlockSpec((PEER_TT, w), lambda i, e: (i, 0))
    sspec = pl.BlockSpec((P_HEADS, P_KEYS, PEER_TT), lambda i, e: (0, 0, i))
    sel = pltpu.VMEM((P_HEADS, P_KEYS, PEER_TT), F32)
    return pl.pallas_call(
        _peer_kernel,
        grid=(T_ALL // PEER_TT, n_eb),
        in_specs=[pl.BlockSpec((D, PEER_TT), lambda i, e: (0, i)), sspec, sspec,
                  pl.BlockSpec((None, PEER_EB, D), lambda i, e: (l, e, 0)),
                  pl.BlockSpec((None, D, PEER_EB), lambda i, e: (l, 0, jnp.maximum(e - 1, 0))),
                  pl.BlockSpec((None, D, PEER_EB), lambda i, e: (l, 0, jnp.where(e == n_eb - 1, n_eb - 1, 0))),
                  tok(D),
                  pl.BlockSpec((1, 1, 6, D), lambda i, e: (l, mod_row(i), 0, 0)),
                  pl.BlockSpec((1, D), lambda i, e: (0, 0)),
                  pl.BlockSpec((1, D), lambda i, e: (0, 0))],
        out_specs=tok(D),
        out_shape=jax.ShapeDtypeStruct((T_ALL, D), F32),
        scratch_shapes=[sel, sel, sel,
                        pltpu.VMEM((PEER_NCAND, PEER_TT), F32),
                        pltpu.VMEM((PEER_EB, PEER_TT), BF16),
                        pltpu.VMEM((PEER_EB, PEER_TT), BF16),
                        pltpu.VMEM((D, PEER_TT), F32)],
        compiler_params=_params(("arbitrary", "arbitrary")),
        name="peer",
    )(h2t, s1, s2, u_bf, vt_bf, vt_bf, x1, mods, g2, b2)


def _grid_angles(rot_dim):
    t = jnp.arange(LAT_S)
    row = (t // GRID_W).astype(F32)
    col = (t % GRID_W).astype(F32)
    n_freq = rot_dim // 4
    inv = THETA ** (-jnp.arange(n_freq, dtype=F32) / n_freq)
    return jnp.concatenate([row[:, None] * inv, col[:, None] * inv], axis=-1)


def _rope_tables(rot_dim, width):
    ang = _grid_angles(rot_dim)
    cos = jnp.cos(ang)
    sin = jnp.sin(ang)
    cos_t = jnp.tile(jnp.concatenate([cos, cos], axis=-1), (1, width // rot_dim))
    sin_t = jnp.tile(jnp.concatenate([-sin, sin], axis=-1), (1, width // rot_dim))
    ident_c = jnp.ones((1, TM, width), F32)
    ident_s = jnp.zeros((1, TM, width), F32)
    cos_t = jnp.concatenate([ident_c, cos_t.reshape(TILES_PER_LAT, TM, width)], axis=0)
    sin_t = jnp.concatenate([ident_s, sin_t.reshape(TILES_PER_LAT, TM, width)], axis=0)
    return cos_t, sin_t


def kernel(x_prompt, x_sample, cache_a_k, cache_a_v, cache_b_ckv, cache_b_kr, cache_c_k, cache_c_v, cache_d_k, cache_d_v, c, c_ctx, w_mod, b_mod, ln1_g, ln1_b, ln2_g, ln2_b, ev_w_in, ev_lam_q1, ev_lam_k1, ev_lam_q2, ev_lam_k2, ev_g_sub, ev_g_cq, ev_w_uq, ev_g_ckv, ev_w_ukv, ev_w_out, od_w_in, od_g_qn, od_g_kn, od_rpb, od_w_out, pk_w_q, pk_k1, pk_k2, pk_u, pk_v):
    x = jnp.concatenate([x_prompt.reshape(T_CTX, D), x_sample.reshape(T_LAT, D)], axis=0)
    cvecs = jnp.concatenate([c_ctx[None, :], c, jnp.zeros((8 - 1 - N_LAT_B, D), F32)], axis=0)
    mods = _modulation(cvecs, w_mod, b_mod).reshape(DEPTH, 8, 6, D)
    u_bf, vt_bf = _prep_tables(pk_u, pk_v)

    cos_a, sin_a = _rope_tables(D_A, 512)
    cos_b, sin_b = _rope_tables(B_ROPE, 256)
    rt_even = jnp.concatenate([cos_a, sin_a, cos_b, sin_b], axis=-1)
    rt_odd = jnp.concatenate(_rope_tables(D_C, 512), axis=-1)
    ones_bd = jnp.kron(jnp.eye(512 // D_C, dtype=F32), jnp.ones((D_C, D_C), F32)).astype(BF16)
    lat3 = lambda a, w: a.reshape(N_LAT_B, PAST, w)

    states = {}
    for l in range(DEPTH):
        j = l // 2
        if l % 2 == 0:
            lam_init = 0.8 - 0.6 * math.exp(-0.3 * l)
            w_in = jnp.pad(ev_w_in[j], ((0, 0), (0, EVEN_IN_PAD - ev_w_in.shape[-1]))).astype(BF16)
            wuq = ev_w_uq[j].reshape(B_QRANK, H_B, B_NOPE + B_ROPE)
            wuq = jnp.concatenate([wuq[:, :, :B_NOPE].reshape(B_QRANK, -1),
                                   wuq[:, :, B_NOPE:].reshape(B_QRANK, -1)], axis=1).astype(BF16)
            wukv = ev_w_ukv[j].reshape(B_KVRANK, H_B, B_NOPE + B_V)
            wukv = jnp.concatenate([wukv[:, :, :B_NOPE].reshape(B_KVRANK, -1),
                                    wukv[:, :, B_NOPE:].reshape(B_KVRANK, -1)], axis=1).astype(BF16)
            (qa, bq, ka, va, kvb, kr,
             states["a_k"], states["a_v"], states["b_ckv"], states["b_kr"]) = _proj_call(
                _proj_even_kernel, "proj_even", x, mods, l, rt_even,
                [w_in, ev_g_cq[j][None, :], wuq, ev_g_ckv[j][None, :], wukv],
                (512, 768, 512, 512, 1024, B_ROPE), (512, 512, B_KVRANK, B_ROPE))
            lam_vecs = [v[j][None, :] for v in (ev_lam_q1, ev_lam_k1, ev_lam_q2, ev_lam_k2)]
            g_sub = ev_g_sub[j][None, :]
            proj = (qa, bq, ka, va, kvb, kr)
            outs = _attn_even(lam_vecs, g_sub, lam_init, *proj)
            cache_kvb = _matmul(cache_b_ckv[:, j].reshape(N_LAT_B * PAST, B_KVRANK), wukv)
            cache = (lat3(cache_a_k[:, j], 512), lat3(cache_a_v[:, j], 512), cache_kvb,
                     lat3(cache_b_kr[:, j], B_ROPE))
            o1, o2 = _attn_even(lam_vecs, g_sub, lam_init, *proj, cache=cache, prev_out=outs)
            w_out = ev_w_out[j]
        else:
            w_in = od_w_in[j].astype(BF16)
            g_qn = jnp.tile(od_g_qn[j], H_C)[None, :]
            g_kn = jnp.tile(od_g_kn[j], G_C)[None, :]
            (cq, ck, cv, dq, dk, dv,
             states["c_k"], states["c_v"], states["d_k"], states["d_v"]) = _proj_call(
                _proj_odd_kernel, "proj_odd", x, mods, l, rt_odd, [w_in, g_qn, g_kn, ones_bd],
                (512, 128, 128, 512, 512, 512), (128, 128, 512, 512))
            o1 = _gqa(cq, ck, cv)
            o1 = _gqa(cq, ck, cv, cache=(lat3(cache_c_k[:, j], 128), lat3(cache_c_v[:, j], 128)),
                      prev_out=[o1])
            o2 = _mha(dq, dk, dv)
            bias = _na_bias(od_rpb[j])
            offs = jnp.arange(NA_ROWS + 1)[:, None] + jnp.arange(NA_ROWS)[None, :]
            bias_win = jnp.transpose(bias[:, offs], (1, 0, 3, 2, 4)).reshape(
                NA_ROWS + 1, H_D, GRID_W, NA_ROWS * GRID_W)
            o2 = _neighbourhood(dq, dk, dv, lat3(cache_d_k[:, j], 512), lat3(cache_d_v[:, j], 512),
                                bias_win, o2)
            w_out = od_w_out[j]

        x1, h2t, s1, s2 = _post(x, mods, l, o1, o2, w_out[:512].astype(BF16), w_out[512:].astype(BF16),
                                ln1_g[l][None, :], ln1_b[l][None, :], pk_w_q[l].astype(BF16),
                                pk_k1[l].astype(BF16), pk_k2[l].astype(BF16))
        x = _peer(h2t, s1, s2, u_bf, vt_bf, x1, mods, l, ln2_g[l][None, :], ln2_b[l][None, :])

    y = x[:T_CTX].reshape(N_CTX_B, CTX_S, D)
    z = x[T_CTX:].reshape(N_LAT_B, LAT_S, D)
    n_even = (DEPTH + 1) // 2
    n_odd = DEPTH // 2
    return (y, z,
            states["a_k"].reshape(N_CTX_B, n_even, CTX_S, H_A, 2, D_A),
            states["a_v"].reshape(N_CTX_B, n_even, CTX_S, H_A, 2 * D_A),
            states["b_ckv"].reshape(N_CTX_B, n_even, CTX_S, B_KVRANK),
            states["b_kr"].reshape(N_CTX_B, n_even, CTX_S, B_ROPE),
            states["c_k"].reshape(N_CTX_B, n_odd, CTX_S, G_C, D_C),
            states["c_v"].reshape(N_CTX_B, n_odd, CTX_S, G_C, D_C),
            states["d_k"].reshape(N_CTX_B, n_odd, CTX_S, H_D, D_D),
            states["d_v"].reshape(N_CTX_B, n_odd, CTX_S, H_D, D_D))
```

```python
import functools
import math

import jax
import jax.numpy as jnp
from jax import lax
from jax.experimental import pallas as pl
from jax.experimental.pallas import tpu as pltpu

F32 = jnp.float32
BF16 = jnp.bfloat16

D = 1024
N_CTX_B = 32
CTX_S = 256
N_LAT_B = 4
LAT_S = 1024
PAST = 256
T_CTX = N_CTX_B * CTX_S
T_LAT = N_LAT_B * LAT_S
T_ALL = T_CTX + T_LAT
DEPTH = 2
GRID_W = 64
GRID_ROWS = LAT_S // GRID_W
THETA = 10000.0
EPS = 1e-6
ALPHA = (2 * DEPTH) ** 0.25

H_A, D_A = 4, 64
H_B, B_NOPE, B_ROPE, B_V, B_QRANK, B_KVRANK = 8, 64, 32, 64, 256, 128
H_C, G_C, D_C = 8, 2, 64
H_D, D_D = 8, 64
NA_ROWS, NA_COLS = 8, 16
P_HEADS, P_KEYS, P_DKEY, P_TOPK = 8, 128, 256, 16
P_EXPERTS = P_KEYS * P_KEYS

EVEN_IN_PAD = 2048
ODD_IN = 2304

TM = 256
N_TILES = T_ALL // TM
N_CTX_TILES = T_CTX // TM
TILES_PER_LAT = LAT_S // TM
N_LAT_TILES = T_LAT // TM
TQ = 256
PEER_TT = 512
PEER_EB = 1024
PEER_NSEL = P_TOPK + 1
VMEM_LIMIT = 56 * 1024 * 1024

NEG_INF = float("-inf")


def _dot(a, b):
    return jnp.dot(a, b, preferred_element_type=F32)


def _dot_nt(a, b):
    return lax.dot_general(a, b, (((1,), (1,)), ((), ())), preferred_element_type=F32)


def _layer_norm(x, g, b):
    mu = jnp.mean(x, axis=-1, keepdims=True)
    xc = x - mu
    var = jnp.mean(xc * xc, axis=-1, keepdims=True)
    return xc * lax.rsqrt(var + EPS) * g + b


def _rms_norm(x, g):
    return x * lax.rsqrt(jnp.mean(x * x, axis=-1, keepdims=True) + EPS) * g


def _chunk_rms_norm(x, g, ones_bd, chunk):
    x2 = x * x
    hi = x2.astype(BF16)
    lo = (x2 - hi.astype(F32)).astype(BF16)
    ms = (_dot(hi, ones_bd) + _dot(lo, ones_bd)) * (1.0 / chunk)
    return x * lax.rsqrt(ms + EPS) * g


def _rope(x, cos_t, sin_t, chunk):
    n = x.shape[-1]
    half = chunk // 2
    lane = lax.broadcasted_iota(jnp.int32, x.shape, x.ndim - 1)
    swapped = jnp.where((lane % chunk) < half,
                        pltpu.roll(x, n - half, x.ndim - 1),
                        pltpu.roll(x, half, x.ndim - 1))
    return x * cos_t + swapped * sin_t


def _params(sem):
    return pltpu.CompilerParams(dimension_semantics=sem, vmem_limit_bytes=VMEM_LIMIT)


def _full(shape):
    zeros = (0,) * len(shape)
    return pl.BlockSpec(shape, lambda *_: zeros)


def _mod_row(i):
    return jnp.where(i < N_CTX_TILES, 0, 1 + (i - N_CTX_TILES) // TILES_PER_LAT)


def _rope_blk(i):
    return jnp.where(i < N_CTX_TILES, 0, 1 + (i - N_CTX_TILES) % TILES_PER_LAT)


def _mod_kernel(c_ref, w_ref, b_ref, o_ref):
    c = c_ref[...]
    a = c * (1.0 / (1.0 + jnp.exp(-c)))
    o_ref[0] = _dot(a.astype(BF16), w_ref[0].astype(BF16)) + b_ref[0]


def _modulation(cvecs, w_mod, b_mod):
    nb = 6
    bn = 6 * D // nb
    return pl.pallas_call(
        _mod_kernel,
        grid=(DEPTH, nb),
        in_specs=[pl.BlockSpec((8, D), lambda l, j: (0, 0)),
                  pl.BlockSpec((1, D, bn), lambda l, j: (l, 0, j)),
                  pl.BlockSpec((1, 1, bn), lambda l, j: (l, 0, j))],
        out_specs=pl.BlockSpec((1, 8, bn), lambda l, j: (l, 0, j)),
        out_shape=jax.ShapeDtypeStruct((DEPTH, 8, 6 * D), F32),
        compiler_params=_params(("arbitrary", "arbitrary")),
        name="modulation",
    )(cvecs, w_mod, b_mod.reshape(DEPTH, 1, 6 * D))


def _tile(g):
    return (g + N_CTX_TILES) % N_TILES


def _state_blk(g):
    return jnp.maximum(g - N_LAT_TILES, 0)


def _proj_even_kernel(x_ref, mod_ref, rt_ref, win_ref, gcq_ref, wuq_ref, gckv_ref, wukv_ref,
                      qa_ref, bq_ref, ka_ref, va_ref, kvb_ref, kr_ref,
                      s_ak_ref, s_av_ref, s_ckv_ref, s_kr_ref):
    x = x_ref[...]
    mod = mod_ref[0, 0]
    h = x * (1.0 + mod[1:2]) + mod[0:1]
    p = _dot(h.astype(BF16), win_ref[...])
    rt = rt_ref[0]
    cos_a, sin_a = rt[:, 0:512], rt[:, 512:1024]
    cos_b, sin_b = rt[:, 1024:1280], rt[:, 1280:1536]
    qa_ref[...] = _rope(p[:, 0:512], cos_a, sin_a, D_A).astype(BF16)
    ka = _rope(p[:, 512:1024], cos_a, sin_a, D_A)
    ka_ref[...] = ka.astype(BF16)
    s_ak_ref[...] = ka
    va = p[:, 1024:1536]
    va_ref[...] = va.astype(BF16)
    s_av_ref[...] = va
    cq = _rms_norm(p[:, 1536:1792], gcq_ref[...])
    bq = _dot(cq.astype(BF16), wuq_ref[...])
    bq_ref[:, 0:512] = bq[:, 0:512].astype(BF16)
    bq_ref[:, 512:768] = _rope(bq[:, 512:768], cos_b, sin_b, B_ROPE).astype(BF16)
    ckv = _rms_norm(p[:, 1792:1920], gckv_ref[...])
    s_ckv_ref[...] = ckv
    kvb_ref[...] = _dot(ckv.astype(BF16), wukv_ref[...]).astype(BF16)
    kr = _rope(p[:, 1920:2048], cos_b[:, 0:128], sin_b[:, 0:128], B_ROPE)[:, 0:B_ROPE]
    kr_ref[...] = kr.astype(BF16)
    s_kr_ref[...] = kr


def _proj_call(kernel_fn, name, x, mods, l, rt, weights, widths, state_widths):
    tok = lambda w: pl.BlockSpec((TM, w), lambda g: (_tile(g), 0))
    st = lambda w: pl.BlockSpec((TM, w), lambda g: (_state_blk(g), 0))
    return pl.pallas_call(
        kernel_fn,
        grid=(N_TILES,),
        in_specs=[tok(D),
                  pl.BlockSpec((1, 1, 6, D), lambda g: (l, _mod_row(_tile(g)), 0, 0)),
                  pl.BlockSpec((1, TM, rt.shape[-1]), lambda g: (_rope_blk(_tile(g)), 0, 0))]
                 + [_full(w.shape) for w in weights],
        out_specs=[tok(w) for w in widths] + [st(w) for w in state_widths],
        out_shape=[jax.ShapeDtypeStruct((T_ALL, w), BF16) for w in widths]
                  + [jax.ShapeDtypeStruct((T_CTX, w), F32) for w in state_widths],
        compiler_params=_params(("arbitrary",)),
        name=name,
    )(x, mods, rt, *weights)


def _proj_odd_kernel(x_ref, mod_ref, rt_ref, win_ref, gq_ref, gk_ref, bd_ref,
                     cq_ref, ck_ref, cv_ref, dq_ref, dk_ref, dv_ref,
                     s_ck_ref, s_cv_ref, s_dk_ref, s_dv_ref):
    x = x_ref[...]
    mod = mod_ref[0, 0]
    h = x * (1.0 + mod[1:2]) + mod[0:1]
    p = _dot(h.astype(BF16), win_ref[...])
    rt = rt_ref[0]
    cos_c, sin_c = rt[:, 0:512], rt[:, 512:1024]
    bd = bd_ref[...]
    cq = _chunk_rms_norm(p[:, 0:512], gq_ref[...], bd, D_C)
    cq_ref[...] = _rope(cq, cos_c, sin_c, D_C).astype(BF16)
    ck = _chunk_rms_norm(p[:, 512:640], gk_ref[...], bd[0:128, 0:128], D_C)
    ck = _rope(ck, cos_c[:, 0:128], sin_c[:, 0:128], D_C)
    ck_ref[...] = ck.astype(BF16)
    s_ck_ref[...] = ck
    cv = p[:, 640:768]
    cv_ref[...] = cv.astype(BF16)
    s_cv_ref[...] = cv
    dq_ref[...] = p[:, 768:1280].astype(BF16)
    dk = p[:, 1280:1792]
    dk_ref[...] = dk.astype(BF16)
    s_dk_ref[...] = dk
    dv = p[:, 1792:2304]
    dv_ref[...] = dv.astype(BF16)
    s_dv_ref[...] = dv


def _matmul_kernel(x_ref, w_ref, o_ref):
    o_ref[...] = _dot(x_ref[...].astype(BF16), w_ref[...]).astype(BF16)


def _matmul(x, w):
    m, k = x.shape
    n = w.shape[1]
    return pl.pallas_call(
        _matmul_kernel,
        grid=(m // TM,),
        in_specs=[pl.BlockSpec((TM, k), lambda i: (i, 0)), _full((k, n))],
        out_specs=pl.BlockSpec((TM, n), lambda i: (i, 0)),
        out_shape=jax.ShapeDtypeStruct((m, n), BF16),
        compiler_params=_params(("arbitrary",)),
        name="cache_kv_up",
    )(x, w)


def _exp_segments(segs):
    m = jnp.max(segs[0], axis=-1, keepdims=True)
    for s in segs[1:]:
        m = jnp.maximum(m, jnp.max(s, axis=-1, keepdims=True))
    es = [jnp.exp(s - m) for s in segs]
    den = jnp.sum(es[0], axis=-1, keepdims=True)
    for e in es[1:]:
        den = den + jnp.sum(e, axis=-1, keepdims=True)
    return es, den


def _pv(ps, vs):
    o = _dot(ps[0].astype(BF16), vs[0])
    for p, v in zip(ps[1:], vs[1:]):
        o = o + _dot(p.astype(BF16), v)
    return o


def _attend(segs, vs):
    es, den = _exp_segments(segs)
    return _pv(es, vs) / den


def _scaled(q, scale):
    return (q.astype(F32) * scale).astype(q.dtype)


def _attn_specs(row0, n_req, n_q, widths_q, widths_k, cache_shapes):
    n_own = n_q
    nqb = n_q // TQ
    q0 = row0 // TQ
    k0 = row0 // n_own
    qspec = lambda w: pl.BlockSpec((TQ, w), lambda b, j: (q0 + b * nqb + j, 0))
    kspec = lambda w: pl.BlockSpec((n_own, w), lambda b, j: (k0 + b, 0))
    cspec = lambda shp: pl.BlockSpec((1,) + tuple(shp[1:]), lambda b, j: (b,) + (0,) * (len(shp) - 1))
    specs = [qspec(w) for w in widths_q] + [kspec(w) for w in widths_k] + [cspec(s) for s in cache_shapes]
    return specs, qspec, (n_req, nqb)


def _attn_even_kernel(*refs, lam_init, has_cache):
    lq1_ref, lk1_ref, lq2_ref, lk2_ref, gsub_ref, qa_ref, bq_ref, ka_ref, va_ref, kvb_ref, kr_ref = refs[:11]
    if has_cache:
        cka_ref, cva_ref, ckvb_ref, ckr_ref = refs[11:15]
    oa_ref, ob_ref = refs[-2:]
    lam = (jnp.exp(jnp.sum(lq1_ref[...] * lk1_ref[...], axis=-1, keepdims=True))
           - jnp.exp(jnp.sum(lq2_ref[...] * lk2_ref[...], axis=-1, keepdims=True)) + lam_init)
    gsub = gsub_ref[...]
    for hd in range(H_A):
        vsl = slice(hd * 2 * D_A, (hd + 1) * 2 * D_A)
        vs = [va_ref[:, vsl]] + ([cva_ref[0, :, vsl].astype(BF16)] if has_cache else [])
        os = []
        for m in range(2):
            sl = slice(hd * 2 * D_A + m * D_A, hd * 2 * D_A + (m + 1) * D_A)
            q = _scaled(qa_ref[:, sl], D_A ** -0.5)
            segs = [_dot_nt(q, ka_ref[:, sl])]
            if has_cache:
                segs.append(_dot_nt(q, cka_ref[0, :, sl].astype(BF16)))
            os.append(_attend(segs, vs))
        o = os[0] - lam * os[1]
        oa_ref[:, vsl] = (_rms_norm(o, gsub) * (1.0 - lam_init)).astype(BF16)
    kr = kr_ref[...]
    if has_cache:
        ckr = ckr_ref[0].astype(BF16)
    sc = (B_NOPE + B_ROPE) ** -0.5
    for hd in range(H_B):
        nsl = slice(hd * B_NOPE, (hd + 1) * B_NOPE)
        vsl = slice(512 + hd * B_V, 512 + (hd + 1) * B_V)
        qn = bq_ref[:, nsl]
        qr = bq_ref[:, 512 + hd * B_ROPE:512 + (hd + 1) * B_ROPE]
        segs = [(_dot_nt(qn, kvb_ref[:, nsl]) + _dot_nt(qr, kr)) * sc]
        vs = [kvb_ref[:, vsl]]
        if has_cache:
            segs.append((_dot_nt(qn, ckvb_ref[0, :, nsl]) + _dot_nt(qr, ckr)) * sc)
            vs.append(ckvb_ref[0, :, vsl])
        ob_ref[:, hd * B_V:(hd + 1) * B_V] = _attend(segs, vs).astype(BF16)


def _attn_call(kernel_fn, name, params, arrays, widths_q, widths_k, latent, cache, prev_out, n_out):
    row0, n_req, n_q = (T_CTX, N_LAT_B, LAT_S) if latent else (0, N_CTX_B, CTX_S)
    cache = list(cache)
    cache_shapes = []
    for a in cache:
        cache_shapes.append(a.shape if a.ndim == 3 else (n_req,) + (a.shape[0] // n_req, a.shape[1]))
    cache = [a.reshape(shp) for a, shp in zip(cache, cache_shapes)]
    specs, qspec, grid = _attn_specs(row0, n_req, n_q, widths_q, widths_k, cache_shapes)
    in_specs = [_full(p.shape) for p in params] + specs
    operands = list(params) + list(arrays) + cache
    aliases = {}
    if prev_out is not None:
        for i, o in enumerate(prev_out):
            aliases[len(operands)] = i
            operands.append(o)
            in_specs.append(pl.BlockSpec(memory_space=pl.ANY))
    return pl.pallas_call(
        kernel_fn,
        grid=grid,
        in_specs=in_specs,
        out_specs=[qspec(512)] * n_out,
        out_shape=[jax.ShapeDtypeStruct((T_ALL, 512), BF16)] * n_out,
        input_output_aliases=aliases,
        compiler_params=_params(("arbitrary", "arbitrary")),
        name=name,
    )(*operands)


def _attn_even(lam_vecs, g_sub, lam_init, qa, bq, ka, va, kvb, kr, cache=(), prev_out=None):
    latent = prev_out is not None
    fn = functools.partial(_attn_even_kernel, lam_init=lam_init, has_cache=latent)
    return _attn_call(fn, "attn_even", list(lam_vecs) + [g_sub], [qa, bq, ka, va, kvb, kr],
                      (512, 768), (512, 512, 1024, B_ROPE), latent, cache, prev_out, 2)


def _gqa_kernel(*refs, has_cache):
    cq_ref, ck_ref, cv_ref = refs[:3]
    if has_cache:
        cck_ref, ccv_ref = refs[3:5]
    o_ref = refs[-1]
    for hd in range(H_C):
        g = hd // (H_C // G_C)
        gsl = slice(g * D_C, (g + 1) * D_C)
        q = _scaled(cq_ref[:, hd * D_C:(hd + 1) * D_C], D_C ** -0.5)
        segs = [_dot_nt(q, ck_ref[:, gsl])]
        vs = [cv_ref[:, gsl]]
        if has_cache:
            segs.append(_dot_nt(q, cck_ref[0, :, gsl].astype(BF16)))
            vs.append(ccv_ref[0, :, gsl].astype(BF16))
        o_ref[:, hd * D_C:(hd + 1) * D_C] = _attend(segs, vs).astype(BF16)


def _gqa(cq, ck, cv, cache=(), prev_out=None):
    latent = prev_out is not None
    fn = functools.partial(_gqa_kernel, has_cache=latent)
    return _attn_call(fn, "attn_gqa", [], [cq, ck, cv], (512,), (128, 128), latent, cache, prev_out, 1)[0]


def _mha_kernel(q_ref, k_ref, v_ref, o_ref):
    for hd in range(H_D):
        sl = slice(hd * D_D, (hd + 1) * D_D)
        segs = [_dot_nt(_scaled(q_ref[:, sl], D_D ** -0.5), k_ref[:, sl])]
        o_ref[:, sl] = _attend(segs, [v_ref[:, sl]]).astype(BF16)


def _mha(q, k, v):
    return _attn_call(_mha_kernel, "attn_mha", [], [q, k, v], (512,), (512, 512), False, (), None, 1)[0]


def _na_bias_kernel(rpb_ref, o_ref):
    hd = pl.program_id(0)
    qc = lax.broadcasted_iota(jnp.int32, (GRID_W, GRID_W), 0)
    kc = lax.broadcasted_iota(jnp.int32, (GRID_W, GRID_W), 1)
    dc = jnp.clip(kc - qc, -(NA_COLS - 1), NA_COLS - 1) + (NA_COLS - 1)
    c0 = jnp.clip(qc - NA_COLS // 2, 0, GRID_W - NA_COLS)
    n_dc = 2 * NA_COLS - 1
    n_dr = 2 * NA_ROWS - 1
    for dr in range(n_dr):
        acc = jnp.zeros((GRID_W, GRID_W), F32)
        for j in range(n_dc):
            acc = jnp.where(dc == j, rpb_ref[(hd * n_dr + dr) * n_dc + j], acc)
        o_ref[0, dr] = jnp.where(kc >= c0, jnp.where(kc < c0 + NA_COLS, acc, NEG_INF), NEG_INF)


def _na_bias(rpb):
    n_dr = 2 * NA_ROWS - 1
    return pl.pallas_call(
        _na_bias_kernel,
        grid=(H_D,),
        in_specs=[pl.BlockSpec(memory_space=pltpu.SMEM)],
        out_specs=pl.BlockSpec((1, n_dr, GRID_W, GRID_W), lambda h: (h, 0, 0, 0)),
        out_shape=jax.ShapeDtypeStruct((H_D, n_dr, GRID_W, GRID_W), F32),
        compiler_params=_params(("arbitrary",)),
        name="na_bias",
    )(rpb.reshape(-1))


def _na_window_start(r):
    return jnp.clip(r - NA_ROWS // 2, 0, GRID_ROWS - NA_ROWS)


def _na_kernel(q_ref, k_ref, v_ref, ck_ref, cv_ref, bias_ref, prev_ref, o_ref):
    r = pl.program_id(1)
    start = pl.multiple_of(_na_window_start(r) * GRID_W, GRID_W)
    n_loc = NA_ROWS * GRID_W
    scale = D_D ** -0.5
    heads = [slice(hd * D_D, (hd + 1) * D_D) for hd in range(H_D)]
    segs = []
    for hd, sl in enumerate(heads):
        q = _scaled(q_ref[:, sl], scale)
        segs.append([_dot_nt(q, k_ref[pl.ds(start, n_loc), sl]) + bias_ref[0, hd],
                     _dot_nt(q, ck_ref[0, :, sl].astype(BF16))])
    exps = [_exp_segments(s) for s in segs]
    for sl, (es, den) in zip(heads, exps):
        o = _pv(es, [v_ref[pl.ds(start, n_loc), sl], cv_ref[0, :, sl].astype(BF16)]) / den
        o_ref[:, sl] = o.astype(BF16)


def _neighbourhood(q, k, v, ck, cv, bias_win, prev_out):
    n_loc = NA_ROWS * GRID_W
    q0 = T_CTX // GRID_W
    k0 = T_CTX // LAT_S
    qspec = pl.BlockSpec((GRID_W, 512), lambda b, r: (q0 + b * GRID_ROWS + r, 0))
    kspec = pl.BlockSpec((LAT_S, 512), lambda b, r: (k0 + b, 0))
    cspec = pl.BlockSpec((1, PAST, 512), lambda b, r: (b, 0, 0))
    return pl.pallas_call(
        _na_kernel,
        grid=(N_LAT_B, GRID_ROWS),
        in_specs=[qspec, kspec, kspec, cspec, cspec,
                  pl.BlockSpec((1, H_D, GRID_W, n_loc),
                               lambda b, r: (_na_window_start(r) - r + NA_ROWS - 1, 0, 0, 0)),
                  pl.BlockSpec(memory_space=pl.ANY)],
        out_specs=qspec,
        out_shape=jax.ShapeDtypeStruct((T_ALL, 512), BF16),
        input_output_aliases={6: 0},
        compiler_params=_params(("arbitrary", "arbitrary")),
        name="attn_neighbourhood",
    )(q, k, v, ck, cv, bias_win, prev_out)


def _post_kernel(x_ref, mod_ref, o1_ref, o2_ref, wo1_ref, wo2_ref, g1_ref, b1_ref,
                 wq_ref, k1_ref, k2_ref, x1_ref, h2t_ref, s1_ref, s2_ref):
    mod = mod_ref[0, 0]
    y = _dot(o1_ref[...], wo1_ref[...]) + _dot(o2_ref[...], wo2_ref[...])
    x1 = _layer_norm(ALPHA * x_ref[...] + mod[2:3] * y, g1_ref[...], b1_ref[...])
    x1_ref[...] = x1
    h2f = x1 * (1.0 + mod[4:5]) + mod[3:4]
    h2 = h2f.astype(BF16)
    h2t_ref[...] = h2f.T.astype(BF16)
    q = _dot(h2, wq_ref[...]).astype(BF16)
    half = P_DKEY // 2
    k1 = k1_ref[...]
    k2 = k2_ref[...]
    for hd in range(P_HEADS):
        s1_ref[hd] = _dot_nt(k1, q[:, hd * P_DKEY:hd * P_DKEY + half])
        s2_ref[hd] = _dot_nt(k2, q[:, hd * P_DKEY + half:(hd + 1) * P_DKEY])


def _post(x, mods, l, o1, o2, wo1, wo2, g1, b1, wq, k1, k2):
    tok = lambda w: pl.BlockSpec((TM, w), lambda i: (i, 0))
    sspec = pl.BlockSpec((P_HEADS, P_KEYS, TM), lambda i: (0, 0, i))
    return pl.pallas_call(
        _post_kernel,
        grid=(N_TILES,),
        in_specs=[tok(D),
                  pl.BlockSpec((1, 1, 6, D), lambda i: (l, _mod_row(i), 0, 0)),
                  tok(512), tok(512), _full((512, D)), _full((512, D)), _full((1, D)), _full((1, D)),
                  _full((D, P_HEADS * P_DKEY)), _full((P_KEYS, P_DKEY // 2)), _full((P_KEYS, P_DKEY // 2))],
        out_specs=[tok(D), pl.BlockSpec((D, TM), lambda i: (0, i)), sspec, sspec],
        out_shape=[jax.ShapeDtypeStruct((T_ALL, D), F32),
                   jax.ShapeDtypeStruct((D, T_ALL), BF16),
                   jax.ShapeDtypeStruct((P_HEADS, P_KEYS, T_ALL), F32),
                   jax.ShapeDtypeStruct((P_HEADS, P_KEYS, T_ALL), F32)],
        compiler_params=_params(("arbitrary",)),
        name="post_mixer",
    )(x, mods, o1, o2, wo1, wo2, g1, b1, wq, k1, k2)


def _tables_kernel(u_ref, v_ref, ub_ref, vt_ref):
    ub_ref[0] = u_ref[0].astype(BF16)
    vt_ref[0] = v_ref[0].T.astype(BF16)


def _prep_tables(pk_u, pk_v):
    blk = pl.BlockSpec((1, PEER_EB, D), lambda l, e: (l, e, 0))
    return pl.pallas_call(
        _tables_kernel,
        grid=(DEPTH, P_EXPERTS // PEER_EB),
        in_specs=[blk, blk],
        out_specs=[blk, pl.BlockSpec((1, D, PEER_EB), lambda l, e: (l, 0, e))],
        out_shape=[jax.ShapeDtypeStruct((DEPTH, P_EXPERTS, D), BF16),
                   jax.ShapeDtypeStruct((DEPTH, D, P_EXPERTS), BF16)],
        compiler_params=_params(("arbitrary", "arbitrary")),
        name="expert_tables",
    )(pk_u, pk_v)


def _sorting_network(n):
    pairs = []
    p = 1
    while p < n:
        k = p
        while k >= 1:
            for j in range(k % p, n - k, 2 * k):
                for i in range(min(k, n - j - k)):
                    if (i + j) // (2 * p) == (i + j + k) // (2 * p):
                        pairs.append((i + j, i + j + k))
            k //= 2
        p *= 2
    return pairs


def _top_values(x, n):
    depth = x.shape[0] // 8
    groups = [x[8 * k:8 * k + 8, :] for k in range(depth)]
    size = 1
    while size < depth:
        size *= 2
    bottom = jnp.full_like(groups[0], NEG_INF)
    groups = groups + [bottom] * (size - depth)
    for i, j in _sorting_network(size):
        a, b = groups[i], groups[j]
        groups[i], groups[j] = jnp.maximum(a, b), jnp.minimum(a, b)
    groups = groups[:depth]
    vals = []
    for t in range(n):
        head = groups[0]
        m = jnp.max(head, axis=0, keepdims=True)
        vals.append(m)
        taken = head == m
        for k in range(min(n - 1 - t, depth)):
            groups[k] = jnp.where(taken, groups[k + 1] if k + 1 < depth else bottom, groups[k])
    return vals


PEER_PAIRS = [(a, b) for a in range(PEER_NSEL) for b in range(PEER_NSEL) if (a + 1) * (b + 1) <= PEER_NSEL]
PEER_NCAND = -(-len(PEER_PAIRS) // 8) * 8


def _peer_select(s1_ref, s2_ref, eth_scr, e1_scr, e2_scr, cand_scr):
    n_pad = PEER_NCAND - len(PEER_PAIRS)
    cand_scr[len(PEER_PAIRS):, :] = jnp.full((n_pad, cand_scr.shape[1]), NEG_INF, F32)

    def body(hd, carry):
        s1 = s1_ref[hd]
        s2 = s2_ref[hd]
        v1 = _top_values(s1, PEER_NSEL)
        v2 = _top_values(s2, PEER_NSEL)
        for k, (a, b) in enumerate(PEER_PAIRS):
            cand_scr[k:k + 1, :] = v1[a] + v2[b]
        cs = _top_values(cand_scr[...], PEER_NSEL)
        tau = 0.5 * (cs[P_TOPK - 1] + cs[P_TOPK])
        z = jnp.exp(cs[0] - cs[0])
        for j in range(1, P_TOPK):
            z = z + jnp.exp(cs[j] - cs[0])
        half_inv_z = 0.5 / z
        eth_scr[hd] = jnp.exp((tau - s1) - v2[0]) * half_inv_z
        e1_scr[hd] = jnp.exp(s1 - v1[0])
        e2_scr[hd] = jnp.exp(s2 - v2[0]) * half_inv_z
        return carry

    lax.fori_loop(0, P_HEADS, body, 0)


def _gelu2(a):
    return a * (1.0 + lax.erf(a * (2.0 ** -0.5)))


PEER_LW = 256
PEER_CH = 256


def _peer_kernel(h2t_ref, s1_ref, s2_ref, u_ref, vt_prev_ref, vt_last_ref, x1_ref, mod_ref, g2_ref, b2_ref,
                 o_ref, eth_scr, e1_scr, e2_scr, cand_scr, c_cur, c_prev, acc_scr):
    eb = pl.program_id(1)
    n_eb = pl.num_programs(1)
    rows_per_step = PEER_EB // P_KEYS

    @pl.when(eb == 0)
    def _():
        _peer_select(s1_ref, s2_ref, eth_scr, e1_scr, e2_scr, cand_scr)
        acc_scr[...] = jnp.zeros_like(acc_scr)
        c_prev[...] = jnp.zeros_like(c_prev)

    @pl.when(eb > 0)
    def _():
        c_prev[...] = c_cur[...]

    h2t = h2t_ref[...]
    n_ch = PEER_EB // PEER_CH
    rb = D // n_ch
    act_next = _dot(u_ref[0:PEER_CH, :], h2t)
    for k in range(n_ch):
        act = act_next
        if k + 1 < n_ch:
            act_next = _dot(u_ref[(k + 1) * PEER_CH:(k + 2) * PEER_CH, :], h2t)
        acc_scr[k * rb:(k + 1) * rb, :] += _dot(vt_prev_ref[k * rb:(k + 1) * rb, :], c_prev[...])
        for jj in range(PEER_CH // P_KEYS):
            j = k * (PEER_CH // P_KEYS) + jj
            i1 = eb * rows_per_step + j
            for lc in range(PEER_TT // PEER_LW):
                ls = slice(lc * PEER_LW, (lc + 1) * PEER_LW)
                w = jnp.zeros((P_KEYS, PEER_LW), F32)
                for hd in range(P_HEADS):
                    eth = eth_scr[hd, pl.ds(i1, 1), ls]
                    e1 = e1_scr[hd, pl.ds(i1, 1), ls]
                    e2 = e2_scr[hd, :, ls]
                    w = w + jnp.where(e2 >= eth, e2, 0.0) * e1
                a = act[jj * P_KEYS:(jj + 1) * P_KEYS, ls]
                c_cur[j * P_KEYS:(j + 1) * P_KEYS, ls] = (w * _gelu2(a)).astype(BF16)

    @pl.when(eb == n_eb - 1)
    def _():
        mod = mod_ref[0, 0]
        peer = (acc_scr[...] + _dot(vt_last_ref[...], c_cur[...])).T
        o_ref[...] = _layer_norm(ALPHA * x1_ref[...] + mod[5:6] * peer, g2_ref[...], b2_ref[...])


def _peer(h2t, s1, s2, u_bf, vt_bf, x1, mods, l, g2, b2):
    tiles_ctx = T_CTX // PEER_TT
    tiles_per_lat = LAT_S // PEER_TT
    n_eb = P_EXPERTS // PEER_EB
    mod_row = lambda i: jnp.where(i < tiles_ctx, 0, 1 + (i - tiles_ctx) // tiles_per_lat)
    tok = lambda w: pl.BlockSpec((PEER_TT, w), lambda i, e: (i, 0))
    sspec = pl.BlockSpec((P_HEADS, P_KEYS, PEER_TT), lambda i, e: (0, 0, i))
    sel = pltpu.VMEM((P_HEADS, P_KEYS, PEER_TT), F32)
    return pl.pallas_call(
        _peer_kernel,
        grid=(T_ALL // PEER_TT, n_eb),
        in_specs=[pl.BlockSpec((D, PEER_TT), lambda i, e: (0, i)), sspec, sspec,
                  pl.BlockSpec((None, PEER_EB, D), lambda i, e: (l, e, 0)),
                  pl.BlockSpec((None, D, PEER_EB), lambda i, e: (l, 0, jnp.maximum(e - 1, 0))),
                  pl.BlockSpec((None, D, PEER_EB), lambda i, e: (l, 0, jnp.where(e == n_eb - 1, n_eb - 1, 0))),
                  tok(D),
                  pl.BlockSpec((1, 1, 6, D), lambda i, e: (l, mod_row(i), 0, 0)),
                  pl.BlockSpec((1, D), lambda i, e: (0, 0)),
                  pl.BlockSpec((1, D), lambda i, e: (0, 0))],
        out_specs=tok(D),
        out_shape=jax.ShapeDtypeStruct((T_ALL, D), F32),
        scratch_shapes=[sel, sel, sel,
                        pltpu.VMEM((PEER_NCAND, PEER_TT), F32),
                        pltpu.VMEM((PEER_EB, PEER_TT), BF16),
                        pltpu.VMEM((PEER_EB, PEER_TT), BF16),
                        pltpu.VMEM((D, PEER_TT), F32)],
        compiler_params=_params(("arbitrary", "arbitrary")),
        name="peer",
    )(h2t, s1, s2, u_bf, vt_bf, vt_bf, x1, mods, g2, b2)


def _grid_angles(rot_dim):
    t = jnp.arange(LAT_S)
    row = (t // GRID_W).astype(F32)
    col = (t % GRID_W).astype(F32)
    n_freq = rot_dim // 4
    inv = THETA ** (-jnp.arange(n_freq, dtype=F32) / n_freq)
    return jnp.concatenate([row[:, None] * inv, col[:, None] * inv], axis=-1)


def _rope_tables(rot_dim, width):
    ang = _grid_angles(rot_dim)
    cos = jnp.cos(ang)
    sin = jnp.sin(ang)
    cos_t = jnp.tile(jnp.concatenate([cos, cos], axis=-1), (1, width // rot_dim))
    sin_t = jnp.tile(jnp.concatenate([-sin, sin], axis=-1), (1, width // rot_dim))
    ident_c = jnp.ones((1, TM, width), F32)
    ident_s = jnp.zeros((1, TM, width), F32)
    cos_t = jnp.concatenate([ident_c, cos_t.reshape(TILES_PER_LAT, TM, width)], axis=0)
    sin_t = jnp.concatenate([ident_s, sin_t.reshape(TILES_PER_LAT, TM, width)], axis=0)
    return cos_t, sin_t


def kernel(x_prompt, x_sample, cache_a_k, cache_a_v, cache_b_ckv, cache_b_kr, cache_c_k, cache_c_v, cache_d_k, cache_d_v, c, c_ctx, w_mod, b_mod, ln1_g, ln1_b, ln2_g, ln2_b, ev_w_in, ev_lam_q1, ev_lam_k1, ev_lam_q2, ev_lam_k2, ev_g_sub, ev_g_cq, ev_w_uq, ev_g_ckv, ev_w_ukv, ev_w_out, od_w_in, od_g_qn, od_g_kn, od_rpb, od_w_out, pk_w_q, pk_k1, pk_k2, pk_u, pk_v):
    x = jnp.concatenate([x_prompt.reshape(T_CTX, D), x_sample.reshape(T_LAT, D)], axis=0)
    cvecs = jnp.concatenate([c_ctx[None, :], c, jnp.zeros((8 - 1 - N_LAT_B, D), F32)], axis=0)
    mods = _modulation(cvecs, w_mod, b_mod).reshape(DEPTH, 8, 6, D)
    u_bf, vt_bf = _prep_tables(pk_u, pk_v)

    cos_a, sin_a = _rope_tables(D_A, 512)
    cos_b, sin_b = _rope_tables(B_ROPE, 256)
    rt_even = jnp.concatenate([cos_a, sin_a, cos_b, sin_b], axis=-1)
    rt_odd = jnp.concatenate(_rope_tables(D_C, 512), axis=-1)
    ones_bd = jnp.kron(jnp.eye(512 // D_C, dtype=F32), jnp.ones((D_C, D_C), F32)).astype(BF16)
    lat3 = lambda a, w: a.reshape(N_LAT_B, PAST, w)

    states = {}
    for l in range(DEPTH):
        j = l // 2
        if l % 2 == 0:
            lam_init = 0.8 - 0.6 * math.exp(-0.3 * l)
            w_in = jnp.pad(ev_w_in[j], ((0, 0), (0, EVEN_IN_PAD - ev_w_in.shape[-1]))).astype(BF16)
            wuq = ev_w_uq[j].reshape(B_QRANK, H_B, B_NOPE + B_ROPE)
            wuq = jnp.concatenate([wuq[:, :, :B_NOPE].reshape(B_QRANK, -1),
                                   wuq[:, :, B_NOPE:].reshape(B_QRANK, -1)], axis=1).astype(BF16)
            wukv = ev_w_ukv[j].reshape(B_KVRANK, H_B, B_NOPE + B_V)
            wukv = jnp.concatenate([wukv[:, :, :B_NOPE].reshape(B_KVRANK, -1),
                                    wukv[:, :, B_NOPE:].reshape(B_KVRANK, -1)], axis=1).astype(BF16)
            (qa, bq, ka, va, kvb, kr,
             states["a_k"], states["a_v"], states["b_ckv"], states["b_kr"]) = _proj_call(
                _proj_even_kernel, "proj_even", x, mods, l, rt_even,
                [w_in, ev_g_cq[j][None, :], wuq, ev_g_ckv[j][None, :], wukv],
                (512, 768, 512, 512, 1024, B_ROPE), (512, 512, B_KVRANK, B_ROPE))
            lam_vecs = [v[j][None, :] for v in (ev_lam_q1, ev_lam_k1, ev_lam_q2, ev_lam_k2)]
            g_sub = ev_g_sub[j][None, :]
            proj = (qa, bq, ka, va, kvb, kr)
            outs = _attn_even(lam_vecs, g_sub, lam_init, *proj)
            cache_kvb = _matmul(cache_b_ckv[:, j].reshape(N_LAT_B * PAST, B_KVRANK), wukv)
            cache = (lat3(cache_a_k[:, j], 512), lat3(cache_a_v[:, j], 512), cache_kvb,
                     lat3(cache_b_kr[:, j], B_ROPE))
            o1, o2 = _attn_even(lam_vecs, g_sub, lam_init, *proj, cache=cache, prev_out=outs)
            w_out = ev_w_out[j]
        else:
            w_in = od_w_in[j].astype(BF16)
            g_qn = jnp.tile(od_g_qn[j], H_C)[None, :]
            g_kn = jnp.tile(od_g_kn[j], G_C)[None, :]
            (cq, ck, cv, dq, dk, dv,
             states["c_k"], states["c_v"], states["d_k"], states["d_v"]) = _proj_call(
                _proj_odd_kernel, "proj_odd", x, mods, l, rt_odd, [w_in, g_qn, g_kn, ones_bd],
                (512, 128, 128, 512, 512, 512), (128, 128, 512, 512))
            o1 = _gqa(cq, ck, cv)
            o1 = _gqa(cq, ck, cv, cache=(lat3(cache_c_k[:, j], 128), lat3(cache_c_v[:, j], 128)),
                      prev_out=[o1])
            o2 = _mha(dq, dk, dv)
            bias = _na_bias(od_rpb[j])
            offs = jnp.arange(NA_ROWS + 1)[:, None] + jnp.arange(NA_ROWS)[None, :]
            bias_win = jnp.transpose(bias[:, offs], (1, 0, 3, 2, 4)).reshape(
                NA_ROWS + 1, H_D, GRID_W, NA_ROWS * GRID_W)
            o2 = _neighbourhood(dq, dk, dv, lat3(cache_d_k[:, j], 512), lat3(cache_d_v[:, j], 512),
                                bias_win, o2)
            w_out = od_w_out[j]

        x1, h2t, s1, s2 = _post(x, mods, l, o1, o2, w_out[:512].astype(BF16), w_out[512:].astype(BF16),
                                ln1_g[l][None, :], ln1_b[l][None, :], pk_w_q[l].astype(BF16),
                                pk_k1[l].astype(BF16), pk_k2[l].astype(BF16))
        x = _peer(h2t, s1, s2, u_bf, vt_bf, x1, mods, l, ln2_g[l][None, :], ln2_b[l][None, :])

    y = x[:T_CTX].reshape(N_CTX_B, CTX_S, D)
    z = x[T_CTX:].reshape(N_LAT_B, LAT_S, D)
    n_even = (DEPTH + 1) // 2
    n_odd = DEPTH // 2
    return (y, z,
            states["a_k"].reshape(N_CTX_B, n_even, CTX_S, H_A, 2, D_A),
            states["a_v"].reshape(N_CTX_B, n_even, CTX_S, H_A, 2 * D_A),
            states["b_ckv"].reshape(N_CTX_B, n_even, CTX_S, B_KVRANK),
            states["b_kr"].reshape(N_CTX_B, n_even, CTX_S, B_ROPE),
            states["c_k"].reshape(N_CTX_B, n_odd, CTX_S, G_C, D_C),
            states["c_v"].reshape(N_CTX_B, n_odd, CTX_S, G_C, D_C),
            states["d_k"].reshape(N_CTX_B, n_odd, CTX_S, H_D, D_D),
            states["d_v"].reshape(N_CTX_B, n_odd, CTX_S, H_D, D_D))
```

```python
import functools
import math

import jax
import jax.numpy as jnp
from jax import lax
from jax.experimental import pallas as pl
from jax.experimental.pallas import tpu as pltpu

F32 = jnp.float32
BF16 = jnp.bfloat16

D = 1024
N_CTX_B = 32
CTX_S = 256
N_LAT_B = 4
LAT_S = 1024
PAST = 256
T_CTX = N_CTX_B * CTX_S
T_LAT = N_LAT_B * LAT_S
T_ALL = T_CTX + T_LAT
DEPTH = 2
GRID_W = 64
GRID_ROWS = LAT_S // GRID_W
THETA = 10000.0
EPS = 1e-6
ALPHA = (2 * DEPTH) ** 0.25

H_A, D_A = 4, 64
H_B, B_NOPE, B_ROPE, B_V, B_QRANK, B_KVRANK = 8, 64, 32, 64, 256, 128
H_C, G_C, D_C = 8, 2, 64
H_D, D_D = 8, 64
NA_ROWS, NA_COLS = 8, 16
P_HEADS, P_KEYS, P_DKEY, P_TOPK = 8, 128, 256, 16
P_EXPERTS = P_KEYS * P_KEYS

EVEN_IN_PAD = 2048
ODD_IN = 2304

TM = 512
N_TILES = T_ALL // TM
N_CTX_TILES = T_CTX // TM
TILES_PER_LAT = LAT_S // TM
N_LAT_TILES = T_LAT // TM
TQ = 256
PEER_TT = 512
PEER_EB = 1024
PEER_NSEL = P_TOPK + 1
VMEM_LIMIT = 56 * 1024 * 1024

NEG_INF = float("-inf")


def _dot(a, b):
    return jnp.dot(a, b, preferred_element_type=F32)


def _dot_nt(a, b):
    return lax.dot_general(a, b, (((1,), (1,)), ((), ())), preferred_element_type=F32)


def _layer_norm(x, g, b):
    mu = jnp.mean(x, axis=-1, keepdims=True)
    xc = x - mu
    var = jnp.mean(xc * xc, axis=-1, keepdims=True)
    return xc * lax.rsqrt(var + EPS) * g + b


def _rms_norm(x, g):
    return x * lax.rsqrt(jnp.mean(x * x, axis=-1, keepdims=True) + EPS) * g


def _chunk_rms_norm(x, g, ones_bd, chunk):
    x2 = x * x
    hi = x2.astype(BF16)
    lo = (x2 - hi.astype(F32)).astype(BF16)
    ms = (_dot(hi, ones_bd) + _dot(lo, ones_bd)) * (1.0 / chunk)
    return x * lax.rsqrt(ms + EPS) * g


def _rope(x, cos_t, sin_t, chunk):
    n = x.shape[-1]
    half = chunk // 2
    lane = lax.broadcasted_iota(jnp.int32, x.shape, x.ndim - 1)
    swapped = jnp.where((lane % chunk) < half,
                        pltpu.roll(x, n - half, x.ndim - 1),
                        pltpu.roll(x, half, x.ndim - 1))
    return x * cos_t + swapped * sin_t


def _params(sem):
    return pltpu.CompilerParams(dimension_semantics=sem, vmem_limit_bytes=VMEM_LIMIT)


def _full(shape):
    zeros = (0,) * len(shape)
    return pl.BlockSpec(shape, lambda *_: zeros)


def _mod_row(i):
    return jnp.where(i < N_CTX_TILES, 0, 1 + (i - N_CTX_TILES) // TILES_PER_LAT)


def _rope_blk(i):
    return jnp.where(i < N_CTX_TILES, 0, 1 + (i - N_CTX_TILES) % TILES_PER_LAT)


def _mod_kernel(c_ref, w_ref, b_ref, o_ref):
    c = c_ref[...]
    a = c * (1.0 / (1.0 + jnp.exp(-c)))
    o_ref[0] = _dot(a.astype(BF16), w_ref[0].astype(BF16)) + b_ref[0]


def _modulation(cvecs, w_mod, b_mod):
    nb = 6
    bn = 6 * D // nb
    return pl.pallas_call(
        _mod_kernel,
        grid=(DEPTH, nb),
        in_specs=[pl.BlockSpec((8, D), lambda l, j: (0, 0)),
                  pl.BlockSpec((1, D, bn), lambda l, j: (l, 0, j)),
                  pl.BlockSpec((1, 1, bn), lambda l, j: (l, 0, j))],
        out_specs=pl.BlockSpec((1, 8, bn), lambda l, j: (l, 0, j)),
        out_shape=jax.ShapeDtypeStruct((DEPTH, 8, 6 * D), F32),
        compiler_params=_params(("arbitrary", "arbitrary")),
        name="modulation",
    )(cvecs, w_mod, b_mod.reshape(DEPTH, 1, 6 * D))


def _tile(g):
    return (g + N_CTX_TILES) % N_TILES


def _state_blk(g):
    return jnp.maximum(g - N_LAT_TILES, 0)


def _proj_even_kernel(x_ref, mod_ref, rt_ref, win_ref, gcq_ref, wuq_ref, gckv_ref, wukv_ref,
                      qa_ref, bq_ref, ka_ref, va_ref, kvb_ref, kr_ref,
                      s_ak_ref, s_av_ref, s_ckv_ref, s_kr_ref):
    x = x_ref[...]
    mod = mod_ref[0, 0]
    h = x * (1.0 + mod[1:2]) + mod[0:1]
    p = _dot(h.astype(BF16), win_ref[...])
    rt = rt_ref[0]
    cos_a, sin_a = rt[:, 0:512], rt[:, 512:1024]
    cos_b, sin_b = rt[:, 1024:1280], rt[:, 1280:1536]
    qa_ref[...] = _rope(p[:, 0:512], cos_a, sin_a, D_A).astype(BF16)
    ka = _rope(p[:, 512:1024], cos_a, sin_a, D_A)
    ka_ref[...] = ka.astype(BF16)
    s_ak_ref[...] = ka
    va = p[:, 1024:1536]
    va_ref[...] = va.astype(BF16)
    s_av_ref[...] = va
    cq = _rms_norm(p[:, 1536:1792], gcq_ref[...])
    bq = _dot(cq.astype(BF16), wuq_ref[...])
    bq_ref[:, 0:512] = bq[:, 0:512].astype(BF16)
    bq_ref[:, 512:768] = _rope(bq[:, 512:768], cos_b, sin_b, B_ROPE).astype(BF16)
    ckv = _rms_norm(p[:, 1792:1920], gckv_ref[...])
    s_ckv_ref[...] = ckv
    kvb_ref[...] = _dot(ckv.astype(BF16), wukv_ref[...]).astype(BF16)
    kr = _rope(p[:, 1920:2048], cos_b[:, 0:128], sin_b[:, 0:128], B_ROPE)[:, 0:B_ROPE]
    kr_ref[...] = kr.astype(BF16)
    s_kr_ref[...] = kr


def _proj_call(kernel_fn, name, x, mods, l, rt, weights, widths, state_widths):
    tok = lambda w: pl.BlockSpec((TM, w), lambda g: (_tile(g), 0))
    st = lambda w: pl.BlockSpec((TM, w), lambda g: (_state_blk(g), 0))
    return pl.pallas_call(
        kernel_fn,
        grid=(N_TILES,),
        in_specs=[tok(D),
                  pl.BlockSpec((1, 1, 6, D), lambda g: (l, _mod_row(_tile(g)), 0, 0)),
                  pl.BlockSpec((1, TM, rt.shape[-1]), lambda g: (_rope_blk(_tile(g)), 0, 0))]
                 + [_full(w.shape) for w in weights],
        out_specs=[tok(w) for w in widths] + [st(w) for w in state_widths],
        out_shape=[jax.ShapeDtypeStruct((T_ALL, w), BF16) for w in widths]
                  + [jax.ShapeDtypeStruct((T_CTX, w), F32) for w in state_widths],
        compiler_params=_params(("arbitrary",)),
        name=name,
    )(x, mods, rt, *weights)


def _proj_odd_kernel(x_ref, mod_ref, rt_ref, win_ref, gq_ref, gk_ref, bd_ref,
                     cq_ref, ck_ref, cv_ref, dq_ref, dk_ref, dv_ref,
                     s_ck_ref, s_cv_ref, s_dk_ref, s_dv_ref):
    x = x_ref[...]
    mod = mod_ref[0, 0]
    h = x * (1.0 + mod[1:2]) + mod[0:1]
    p = _dot(h.astype(BF16), win_ref[...])
    rt = rt_ref[0]
    cos_c, sin_c = rt[:, 0:512], rt[:, 512:1024]
    bd = bd_ref[...]
    cq = _chunk_rms_norm(p[:, 0:512], gq_ref[...], bd, D_C)
    cq_ref[...] = _rope(cq, cos_c, sin_c, D_C).astype(BF16)
    ck = _chunk_rms_norm(p[:, 512:640], gk_ref[...], bd[0:128, 0:128], D_C)
    ck = _rope(ck, cos_c[:, 0:128], sin_c[:, 0:128], D_C)
    ck_ref[...] = ck.astype(BF16)
    s_ck_ref[...] = ck
    cv = p[:, 640:768]
    cv_ref[...] = cv.astype(BF16)
    s_cv_ref[...] = cv
    dq_ref[...] = p[:, 768:1280].astype(BF16)
    dk = p[:, 1280:1792]
    dk_ref[...] = dk.astype(BF16)
    s_dk_ref[...] = dk
    dv = p[:, 1792:2304]
    dv_ref[...] = dv.astype(BF16)
    s_dv_ref[...] = dv


def _matmul_kernel(x_ref, w_ref, o_ref):
    o_ref[...] = _dot(x_ref[...].astype(BF16), w_ref[...]).astype(BF16)


def _matmul(x, w):
    m, k = x.shape
    n = w.shape[1]
    return pl.pallas_call(
        _matmul_kernel,
        grid=(m // TM,),
        in_specs=[pl.BlockSpec((TM, k), lambda i: (i, 0)), _full((k, n))],
        out_specs=pl.BlockSpec((TM, n), lambda i: (i, 0)),
        out_shape=jax.ShapeDtypeStruct((m, n), BF16),
        compiler_params=_params(("arbitrary",)),
        name="cache_kv_up",
    )(x, w)


def _exp_segments(segs):
    m = jnp.max(segs[0], axis=-1, keepdims=True)
    for s in segs[1:]:
        m = jnp.maximum(m, jnp.max(s, axis=-1, keepdims=True))
    es = [jnp.exp(s - m) for s in segs]
    den = jnp.sum(es[0], axis=-1, keepdims=True)
    for e in es[1:]:
        den = den + jnp.sum(e, axis=-1, keepdims=True)
    return es, den


def _pv(ps, vs):
    o = _dot(ps[0].astype(BF16), vs[0])
    for p, v in zip(ps[1:], vs[1:]):
        o = o + _dot(p.astype(BF16), v)
    return o


def _attend(segs, vs):
    es, den = _exp_segments(segs)
    return _pv(es, vs) / den


def _scaled(q, scale):
    return (q.astype(F32) * scale).astype(q.dtype)


def _attn_specs(row0, n_req, n_q, widths_q, widths_k, cache_shapes):
    n_own = n_q
    nqb = n_q // TQ
    q0 = row0 // TQ
    k0 = row0 // n_own
    qspec = lambda w: pl.BlockSpec((TQ, w), lambda b, j: (q0 + b * nqb + j, 0))
    kspec = lambda w: pl.BlockSpec((n_own, w), lambda b, j: (k0 + b, 0))
    cspec = lambda shp: pl.BlockSpec((1,) + tuple(shp[1:]), lambda b, j: (b,) + (0,) * (len(shp) - 1))
    specs = [qspec(w) for w in widths_q] + [kspec(w) for w in widths_k] + [cspec(s) for s in cache_shapes]
    return specs, qspec, (n_req, nqb)


def _attn_even_kernel(*refs, lam_init, has_cache):
    lq1_ref, lk1_ref, lq2_ref, lk2_ref, gsub_ref, qa_ref, bq_ref, ka_ref, va_ref, kvb_ref, kr_ref = refs[:11]
    if has_cache:
        cka_ref, cva_ref, ckvb_ref, ckr_ref = refs[11:15]
    oa_ref, ob_ref = refs[-2:]
    lam = (jnp.exp(jnp.sum(lq1_ref[...] * lk1_ref[...], axis=-1, keepdims=True))
           - jnp.exp(jnp.sum(lq2_ref[...] * lk2_ref[...], axis=-1, keepdims=True)) + lam_init)
    gsub = gsub_ref[...]
    for hd in range(H_A):
        vsl = slice(hd * 2 * D_A, (hd + 1) * 2 * D_A)
        vs = [va_ref[:, vsl]] + ([cva_ref[0, :, vsl].astype(BF16)] if has_cache else [])
        os = []
        for m in range(2):
            sl = slice(hd * 2 * D_A + m * D_A, hd * 2 * D_A + (m + 1) * D_A)
            q = _scaled(qa_ref[:, sl], D_A ** -0.5)
            segs = [_dot_nt(q, ka_ref[:, sl])]
            if has_cache:
                segs.append(_dot_nt(q, cka_ref[0, :, sl].astype(BF16)))
            os.append(_attend(segs, vs))
        o = os[0] - lam * os[1]
        oa_ref[:, vsl] = (_rms_norm(o, gsub) * (1.0 - lam_init)).astype(BF16)
    kr = kr_ref[...]
    if has_cache:
        ckr = ckr_ref[0].astype(BF16)
    sc = (B_NOPE + B_ROPE) ** -0.5
    for hd in range(H_B):
        nsl = slice(hd * B_NOPE, (hd + 1) * B_NOPE)
        vsl = slice(512 + hd * B_V, 512 + (hd + 1) * B_V)
        qn = bq_ref[:, nsl]
        qr = bq_ref[:, 512 + hd * B_ROPE:512 + (hd + 1) * B_ROPE]
        segs = [(_dot_nt(qn, kvb_ref[:, nsl]) + _dot_nt(qr, kr)) * sc]
        vs = [kvb_ref[:, vsl]]
        if has_cache:
            segs.append((_dot_nt(qn, ckvb_ref[0, :, nsl]) + _dot_nt(qr, ckr)) * sc)
            vs.append(ckvb_ref[0, :, vsl])
        ob_ref[:, hd * B_V:(hd + 1) * B_V] = _attend(segs, vs).astype(BF16)


def _attn_call(kernel_fn, name, params, arrays, widths_q, widths_k, latent, cache, prev_out, n_out):
    row0, n_req, n_q = (T_CTX, N_LAT_B, LAT_S) if latent else (0, N_CTX_B, CTX_S)
    cache = list(cache)
    cache_shapes = []
    for a in cache:
        cache_shapes.append(a.shape if a.ndim == 3 else (n_req,) + (a.shape[0] // n_req, a.shape[1]))
    cache = [a.reshape(shp) for a, shp in zip(cache, cache_shapes)]
    specs, qspec, grid = _attn_specs(row0, n_req, n_q, widths_q, widths_k, cache_shapes)
    in_specs = [_full(p.shape) for p in params] + specs
    operands = list(params) + list(arrays) + cache
    aliases = {}
    if prev_out is not None:
        for i, o in enumerate(prev_out):
            aliases[len(operands)] = i
            operands.append(o)
            in_specs.append(pl.BlockSpec(memory_space=pl.ANY))
    return pl.pallas_call(
        kernel_fn,
        grid=grid,
        in_specs=in_specs,
        out_specs=[qspec(512)] * n_out,
        out_shape=[jax.ShapeDtypeStruct((T_ALL, 512), BF16)] * n_out,
        input_output_aliases=aliases,
        compiler_params=_params(("arbitrary", "arbitrary")),
        name=name,
    )(*operands)


def _attn_even(lam_vecs, g_sub, lam_init, qa, bq, ka, va, kvb, kr, cache=(), prev_out=None):
    latent = prev_out is not None
    fn = functools.partial(_attn_even_kernel, lam_init=lam_init, has_cache=latent)
    return _attn_call(fn, "attn_even", list(lam_vecs) + [g_sub], [qa, bq, ka, va, kvb, kr],
                      (512, 768), (512, 512, 1024, B_ROPE), latent, cache, prev_out, 2)


def _gqa_kernel(*refs, has_cache):
    cq_ref, ck_ref, cv_ref = refs[:3]
    if has_cache:
        cck_ref, ccv_ref = refs[3:5]
    o_ref = refs[-1]
    for hd in range(H_C):
        g = hd // (H_C // G_C)
        gsl = slice(g * D_C, (g + 1) * D_C)
        q = _scaled(cq_ref[:, hd * D_C:(hd + 1) * D_C], D_C ** -0.5)
        segs = [_dot_nt(q, ck_ref[:, gsl])]
        vs = [cv_ref[:, gsl]]
        if has_cache:
            segs.append(_dot_nt(q, cck_ref[0, :, gsl].astype(BF16)))
            vs.append(ccv_ref[0, :, gsl].astype(BF16))
        o_ref[:, hd * D_C:(hd + 1) * D_C] = _attend(segs, vs).astype(BF16)


def _gqa(cq, ck, cv, cache=(), prev_out=None):
    latent = prev_out is not None
    fn = functools.partial(_gqa_kernel, has_cache=latent)
    return _attn_call(fn, "attn_gqa", [], [cq, ck, cv], (512,), (128, 128), latent, cache, prev_out, 1)[0]


def _mha_kernel(q_ref, k_ref, v_ref, o_ref):
    for hd in range(H_D):
        sl = slice(hd * D_D, (hd + 1) * D_D)
        segs = [_dot_nt(_scaled(q_ref[:, sl], D_D ** -0.5), k_ref[:, sl])]
        o_ref[:, sl] = _attend(segs, [v_ref[:, sl]]).astype(BF16)


def _mha(q, k, v):
    return _attn_call(_mha_kernel, "attn_mha", [], [q, k, v], (512,), (512, 512), False, (), None, 1)[0]


def _na_bias_kernel(rpb_ref, o_ref):
    hd = pl.program_id(0)
    qc = lax.broadcasted_iota(jnp.int32, (GRID_W, GRID_W), 0)
    kc = lax.broadcasted_iota(jnp.int32, (GRID_W, GRID_W), 1)
    dc = jnp.clip(kc - qc, -(NA_COLS - 1), NA_COLS - 1) + (NA_COLS - 1)
    c0 = jnp.clip(qc - NA_COLS // 2, 0, GRID_W - NA_COLS)
    n_dc = 2 * NA_COLS - 1
    n_dr = 2 * NA_ROWS - 1
    for dr in range(n_dr):
        acc = jnp.zeros((GRID_W, GRID_W), F32)
        for j in range(n_dc):
            acc = jnp.where(dc == j, rpb_ref[(hd * n_dr + dr) * n_dc + j], acc)
        o_ref[0, dr] = jnp.where(kc >= c0, jnp.where(kc < c0 + NA_COLS, acc, NEG_INF), NEG_INF)


def _na_bias(rpb):
    n_dr = 2 * NA_ROWS - 1
    return pl.pallas_call(
        _na_bias_kernel,
        grid=(H_D,),
        in_specs=[pl.BlockSpec(memory_space=pltpu.SMEM)],
        out_specs=pl.BlockSpec((1, n_dr, GRID_W, GRID_W), lambda h: (h, 0, 0, 0)),
        out_shape=jax.ShapeDtypeStruct((H_D, n_dr, GRID_W, GRID_W), F32),
        compiler_params=_params(("arbitrary",)),
        name="na_bias",
    )(rpb.reshape(-1))


def _na_window_start(r):
    return jnp.clip(r - NA_ROWS // 2, 0, GRID_ROWS - NA_ROWS)


def _na_kernel(q_ref, k_ref, v_ref, ck_ref, cv_ref, bias_ref, prev_ref, o_ref):
    r = pl.program_id(1)
    start = pl.multiple_of(_na_window_start(r) * GRID_W, GRID_W)
    n_loc = NA_ROWS * GRID_W
    scale = D_D ** -0.5
    heads = [slice(hd * D_D, (hd + 1) * D_D) for hd in range(H_D)]
    segs = []
    for hd, sl in enumerate(heads):
        q = _scaled(q_ref[:, sl], scale)
        segs.append([_dot_nt(q, k_ref[pl.ds(start, n_loc), sl]) + bias_ref[0, hd],
                     _dot_nt(q, ck_ref[0, :, sl].astype(BF16))])
    exps = [_exp_segments(s) for s in segs]
    for sl, (es, den) in zip(heads, exps):
        o = _pv(es, [v_ref[pl.ds(start, n_loc), sl], cv_ref[0, :, sl].astype(BF16)]) / den
        o_ref[:, sl] = o.astype(BF16)


def _neighbourhood(q, k, v, ck, cv, bias_win, prev_out):
    n_loc = NA_ROWS * GRID_W
    q0 = T_CTX // GRID_W
    k0 = T_CTX // LAT_S
    qspec = pl.BlockSpec((GRID_W, 512), lambda b, r: (q0 + b * GRID_ROWS + r, 0))
    kspec = pl.BlockSpec((LAT_S, 512), lambda b, r: (k0 + b, 0))
    cspec = pl.BlockSpec((1, PAST, 512), lambda b, r: (b, 0, 0))
    return pl.pallas_call(
        _na_kernel,
        grid=(N_LAT_B, GRID_ROWS),
        in_specs=[qspec, kspec, kspec, cspec, cspec,
                  pl.BlockSpec((1, H_D, GRID_W, n_loc),
                               lambda b, r: (_na_window_start(r) - r + NA_ROWS - 1, 0, 0, 0)),
                  pl.BlockSpec(memory_space=pl.ANY)],
        out_specs=qspec,
        out_shape=jax.ShapeDtypeStruct((T_ALL, 512), BF16),
        input_output_aliases={6: 0},
        compiler_params=_params(("arbitrary", "arbitrary")),
        name="attn_neighbourhood",
    )(q, k, v, ck, cv, bias_win, prev_out)


def _post_kernel(x_ref, mod_ref, o1_ref, o2_ref, wo1_ref, wo2_ref, g1_ref, b1_ref,
                 wq_ref, k1_ref, k2_ref, x1_ref, h2t_ref, s1_ref, s2_ref):
    mod = mod_ref[0, 0]
    y = _dot(o1_ref[...], wo1_ref[...]) + _dot(o2_ref[...], wo2_ref[...])
    x1 = _layer_norm(ALPHA * x_ref[...] + mod[2:3] * y, g1_ref[...], b1_ref[...])
    x1_ref[...] = x1
    h2f = x1 * (1.0 + mod[4:5]) + mod[3:4]
    h2 = h2f.astype(BF16)
    h2t_ref[...] = h2f.T.astype(BF16)
    q = _dot(h2, wq_ref[...]).astype(BF16)
    half = P_DKEY // 2
    k1 = k1_ref[...]
    k2 = k2_ref[...]
    for hd in range(P_HEADS):
        s1_ref[hd] = _dot_nt(k1, q[:, hd * P_DKEY:hd * P_DKEY + half])
        s2_ref[hd] = _dot_nt(k2, q[:, hd * P_DKEY + half:(hd + 1) * P_DKEY])


def _post(x, mods, l, o1, o2, wo1, wo2, g1, b1, wq, k1, k2):
    tok = lambda w: pl.BlockSpec((TM, w), lambda i: (i, 0))
    sspec = pl.BlockSpec((P_HEADS, P_KEYS, TM), lambda i: (0, 0, i))
    return pl.pallas_call(
        _post_kernel,
        grid=(N_TILES,),
        in_specs=[tok(D),
                  pl.BlockSpec((1, 1, 6, D), lambda i: (l, _mod_row(i), 0, 0)),
                  tok(512), tok(512), _full((512, D)), _full((512, D)), _full((1, D)), _full((1, D)),
                  _full((D, P_HEADS * P_DKEY)), _full((P_KEYS, P_DKEY // 2)), _full((P_KEYS, P_DKEY // 2))],
        out_specs=[tok(D), pl.BlockSpec((D, TM), lambda i: (0, i)), sspec, sspec],
        out_shape=[jax.ShapeDtypeStruct((T_ALL, D), F32),
                   jax.ShapeDtypeStruct((D, T_ALL), BF16),
                   jax.ShapeDtypeStruct((P_HEADS, P_KEYS, T_ALL), F32),
                   jax.ShapeDtypeStruct((P_HEADS, P_KEYS, T_ALL), F32)],
        compiler_params=_params(("arbitrary",)),
        name="post_mixer",
    )(x, mods, o1, o2, wo1, wo2, g1, b1, wq, k1, k2)


def _tables_kernel(u_ref, v_ref, ub_ref, vt_ref):
    ub_ref[0] = u_ref[0].astype(BF16)
    vt_ref[0] = v_ref[0].T.astype(BF16)


def _prep_tables(pk_u, pk_v):
    blk = pl.BlockSpec((1, PEER_EB, D), lambda l, e: (l, e, 0))
    return pl.pallas_call(
        _tables_kernel,
        grid=(DEPTH, P_EXPERTS // PEER_EB),
        in_specs=[blk, blk],
        out_specs=[blk, pl.BlockSpec((1, D, PEER_EB), lambda l, e: (l, 0, e))],
        out_shape=[jax.ShapeDtypeStruct((DEPTH, P_EXPERTS, D), BF16),
                   jax.ShapeDtypeStruct((DEPTH, D, P_EXPERTS), BF16)],
        compiler_params=_params(("arbitrary", "arbitrary")),
        name="expert_tables",
    )(pk_u, pk_v)


def _sorting_network(n):
    pairs = []
    p = 1
    while p < n:
        k = p
        while k >= 1:
            for j in range(k % p, n - k, 2 * k):
                for i in range(min(k, n - j - k)):
                    if (i + j) // (2 * p) == (i + j + k) // (2 * p):
                        pairs.append((i + j, i + j + k))
            k //= 2
        p *= 2
    return pairs


def _top_values(x, n):
    depth = x.shape[0] // 8
    groups = [x[8 * k:8 * k + 8, :] for k in range(depth)]
    size = 1
    while size < depth:
        size *= 2
    bottom = jnp.full_like(groups[0], NEG_INF)
    groups = groups + [bottom] * (size - depth)
    for i, j in _sorting_network(size):
        a, b = groups[i], groups[j]
        groups[i], groups[j] = jnp.maximum(a, b), jnp.minimum(a, b)
    groups = groups[:depth]
    vals = []
    for t in range(n):
        head = groups[0]
        m = jnp.max(head, axis=0, keepdims=True)
        vals.append(m)
        taken = head == m
        for k in range(min(n - 1 - t, depth)):
            groups[k] = jnp.where(taken, groups[k + 1] if k + 1 < depth else bottom, groups[k])
    return vals


PEER_PAIRS = [(a, b) for a in range(PEER_NSEL) for b in range(PEER_NSEL) if (a + 1) * (b + 1) <= PEER_NSEL]
PEER_NCAND = -(-len(PEER_PAIRS) // 8) * 8


def _peer_select(s1_ref, s2_ref, eth_scr, e1_scr, e2_scr, cand_scr):
    n_pad = PEER_NCAND - len(PEER_PAIRS)
    cand_scr[len(PEER_PAIRS):, :] = jnp.full((n_pad, cand_scr.shape[1]), NEG_INF, F32)

    def body(hd, carry):
        s1 = s1_ref[hd]
        s2 = s2_ref[hd]
        v1 = _top_values(s1, PEER_NSEL)
        v2 = _top_values(s2, PEER_NSEL)
        for k, (a, b) in enumerate(PEER_PAIRS):
            cand_scr[k:k + 1, :] = v1[a] + v2[b]
        cs = _top_values(cand_scr[...], PEER_NSEL)
        tau = 0.5 * (cs[P_TOPK - 1] + cs[P_TOPK])
        z = jnp.exp(cs[0] - cs[0])
        for j in range(1, P_TOPK):
            z = z + jnp.exp(cs[j] - cs[0])
        half_inv_z = 0.5 / z
        eth_scr[hd] = jnp.exp((tau - s1) - v2[0]) * half_inv_z
        e1_scr[hd] = jnp.exp(s1 - v1[0])
        e2_scr[hd] = jnp.exp(s2 - v2[0]) * half_inv_z
        return carry

    lax.fori_loop(0, P_HEADS, body, 0)


def _gelu2(a):
    return a * (1.0 + lax.erf(a * (2.0 ** -0.5)))


PEER_LW = 256
PEER_CH = 256


def _peer_kernel(h2t_ref, s1_ref, s2_ref, u_ref, vt_prev_ref, vt_last_ref, x1_ref, mod_ref, g2_ref, b2_ref,
                 o_ref, eth_scr, e1_scr, e2_scr, cand_scr, c_cur, c_prev, acc_scr):
    eb = pl.program_id(1)
    n_eb = pl.num_programs(1)
    rows_per_step = PEER_EB // P_KEYS

    @pl.when(eb == 0)
    def _():
        _peer_select(s1_ref, s2_ref, eth_scr, e1_scr, e2_scr, cand_scr)
        acc_scr[...] = jnp.zeros_like(acc_scr)
        c_prev[...] = jnp.zeros_like(c_prev)

    @pl.when(eb > 0)
    def _():
        c_prev[...] = c_cur[...]

    h2t = h2t_ref[...]
    n_ch = PEER_EB // PEER_CH
    rb = D // n_ch
    act_next = _dot(u_ref[0:PEER_CH, :], h2t)
    for k in range(n_ch):
        act = act_next
        if k + 1 < n_ch:
            act_next = _dot(u_ref[(k + 1) * PEER_CH:(k + 2) * PEER_CH, :], h2t)
        acc_scr[k * rb:(k + 1) * rb, :] += _dot(vt_prev_ref[k * rb:(k + 1) * rb, :], c_prev[...])
        for jj in range(PEER_CH // P_KEYS):
            j = k * (PEER_CH // P_KEYS) + jj
            i1 = eb * rows_per_step + j
            for lc in range(PEER_TT // PEER_LW):
                ls = slice(lc * PEER_LW, (lc + 1) * PEER_LW)
                w = jnp.zeros((P_KEYS, PEER_LW), F32)
                for hd in range(P_HEADS):
                    eth = eth_scr[hd, pl.ds(i1, 1), ls]
                    e1 = e1_scr[hd, pl.ds(i1, 1), ls]
                    e2 = e2_scr[hd, :, ls]
                    w = w + jnp.where(e2 >= eth, e2, 0.0) * e1
                a = act[jj * P_KEYS:(jj + 1) * P_KEYS, ls]
                c_cur[j * P_KEYS:(j + 1) * P_KEYS, ls] = (w * _gelu2(a)).astype(BF16)

    @pl.when(eb == n_eb - 1)
    def _():
        mod = mod_ref[0, 0]
        peer = (acc_scr[...] + _dot(vt_last_ref[...], c_cur[...])).T
        o_ref[...] = _layer_norm(ALPHA * x1_ref[...] + mod[5:6] * peer, g2_ref[...], b2_ref[...])


def _peer(h2t, s1, s2, u_bf, vt_bf, x1, mods, l, g2, b2):
    tiles_ctx = T_CTX // PEER_TT
    tiles_per_lat = LAT_S // PEER_TT
    n_eb = P_EXPERTS // PEER_EB
    mod_row = lambda i: jnp.where(i < tiles_ctx, 0, 1 + (i - tiles_ctx) // tiles_per_lat)
    tok = lambda w: pl.BlockSpec((PEER_TT, w), lambda i, e: (i, 0))
    sspec = pl.BlockSpec((P_HEADS, P_KEYS, PEER_TT), lambda i, e: (0, 0, i))
    sel = pltpu.VMEM((P_HEADS, P_KEYS, PEER_TT), F32)
    return pl.pallas_call(
        _peer_kernel,
        grid=(T_ALL // PEER_TT, n_eb),
        in_specs=[pl.BlockSpec((D, PEER_TT), lambda i, e: (0, i)), sspec, sspec,
                  pl.BlockSpec((None, PEER_EB, D), lambda i, e: (l, e, 0)),
                  pl.BlockSpec((None, D, PEER_EB), lambda i, e: (l, 0, jnp.maximum(e - 1, 0))),
                  pl.BlockSpec((None, D, PEER_EB), lambda i, e: (l, 0, jnp.where(e == n_eb - 1, n_eb - 1, 0))),
                  tok(D),
                  pl.BlockSpec((1, 1, 6, D), lambda i, e: (l, mod_row(i), 0, 0)),
                  pl.BlockSpec((1, D), lambda i, e: (0, 0)),
                  pl.BlockSpec((1, D), lambda i, e: (0, 0))],
        out_specs=tok(D),
        out_shape=jax.ShapeDtypeStruct((T_ALL, D), F32),
        scratch_shapes=[sel, sel, sel,
                        pltpu.VMEM((PEER_NCAND, PEER_TT), F32),
                        pltpu.VMEM((PEER_EB, PEER_TT), BF16),
                        pltpu.VMEM((PEER_EB, PEER_TT), BF16),
                        pltpu.VMEM((D, PEER_TT), F32)],
        compiler_params=_params(("arbitrary", "arbitrary")),
        name="peer",
    )(h2t, s1, s2, u_bf, vt_bf, vt_bf, x1, mods, g2, b2)


def _grid_angles(rot_dim):
    t = jnp.arange(LAT_S)
    row = (t // GRID_W).astype(F32)
    col = (t % GRID_W).astype(F32)
    n_freq = rot_dim // 4
    inv = THETA ** (-jnp.arange(n_freq, dtype=F32) / n_freq)
    return jnp.concatenate([row[:, None] * inv, col[:, None] * inv], axis=-1)


def _rope_tables(rot_dim, width):
    ang = _grid_angles(rot_dim)
    cos = jnp.cos(ang)
    sin = jnp.sin(ang)
    cos_t = jnp.tile(jnp.concatenate([cos, cos], axis=-1), (1, width // rot_dim))
    sin_t = jnp.tile(jnp.concatenate([-sin, sin], axis=-1), (1, width // rot_dim))
    ident_c = jnp.ones((1, TM, width), F32)
    ident_s = jnp.zeros((1, TM, width), F32)
    cos_t = jnp.concatenate([ident_c, cos_t.reshape(TILES_PER_LAT, TM, width)], axis=0)
    sin_t = jnp.concatenate([ident_s, sin_t.reshape(TILES_PER_LAT, TM, width)], axis=0)
    return cos_t, sin_t


def kernel(x_prompt, x_sample, cache_a_k, cache_a_v, cache_b_ckv, cache_b_kr, cache_c_k, cache_c_v, cache_d_k, cache_d_v, c, c_ctx, w_mod, b_mod, ln1_g, ln1_b, ln2_g, ln2_b, ev_w_in, ev_lam_q1, ev_lam_k1, ev_lam_q2, ev_lam_k2, ev_g_sub, ev_g_cq, ev_w_uq, ev_g_ckv, ev_w_ukv, ev_w_out, od_w_in, od_g_qn, od_g_kn, od_rpb, od_w_out, pk_w_q, pk_k1, pk_k2, pk_u, pk_v):
    x = jnp.concatenate([x_prompt.reshape(T_CTX, D), x_sample.reshape(T_LAT, D)], axis=0)
    cvecs = jnp.concatenate([c_ctx[None, :], c, jnp.zeros((8 - 1 - N_LAT_B, D), F32)], axis=0)
    mods = _modulation(cvecs, w_mod, b_mod).reshape(DEPTH, 8, 6, D)
    u_bf, vt_bf = _prep_tables(pk_u, pk_v)

    cos_a, sin_a = _rope_tables(D_A, 512)
    cos_b, sin_b = _rope_tables(B_ROPE, 256)
    rt_even = jnp.concatenate([cos_a, sin_a, cos_b, sin_b], axis=-1)
    rt_odd = jnp.concatenate(_rope_tables(D_C, 512), axis=-1)
    ones_bd = jnp.kron(jnp.eye(512 // D_C, dtype=F32), jnp.ones((D_C, D_C), F32)).astype(BF16)
    lat3 = lambda a, w: a.reshape(N_LAT_B, PAST, w)

    states = {}
    for l in range(DEPTH):
        j = l // 2
        if l % 2 == 0:
            lam_init = 0.8 - 0.6 * math.exp(-0.3 * l)
            w_in = jnp.pad(ev_w_in[j], ((0, 0), (0, EVEN_IN_PAD - ev_w_in.shape[-1]))).astype(BF16)
            wuq = ev_w_uq[j].reshape(B_QRANK, H_B, B_NOPE + B_ROPE)
            wuq = jnp.concatenate([wuq[:, :, :B_NOPE].reshape(B_QRANK, -1),
                                   wuq[:, :, B_NOPE:].reshape(B_QRANK, -1)], axis=1).astype(BF16)
            wukv = ev_w_ukv[j].reshape(B_KVRANK, H_B, B_NOPE + B_V)
            wukv = jnp.concatenate([wukv[:, :, :B_NOPE].reshape(B_KVRANK, -1),
                                    wukv[:, :, B_NOPE:].reshape(B_KVRANK, -1)], axis=1).astype(BF16)
            (qa, bq, ka, va, kvb, kr,
             states["a_k"], states["a_v"], states["b_ckv"], states["b_kr"]) = _proj_call(
                _proj_even_kernel, "proj_even", x, mods, l, rt_even,
                [w_in, ev_g_cq[j][None, :], wuq, ev_g_ckv[j][None, :], wukv],
                (512, 768, 512, 512, 1024, B_ROPE), (512, 512, B_KVRANK, B_ROPE))
            lam_vecs = [v[j][None, :] for v in (ev_lam_q1, ev_lam_k1, ev_lam_q2, ev_lam_k2)]
            g_sub = ev_g_sub[j][None, :]
            proj = (qa, bq, ka, va, kvb, kr)
            outs = _attn_even(lam_vecs, g_sub, lam_init, *proj)
            cache_kvb = _matmul(cache_b_ckv[:, j].reshape(N_LAT_B * PAST, B_KVRANK), wukv)
            cache = (lat3(cache_a_k[:, j], 512), lat3(cache_a_v[:, j], 512), cache_kvb,
                     lat3(cache_b_kr[:, j], B_ROPE))
            o1, o2 = _attn_even(lam_vecs, g_sub, lam_init, *proj, cache=cache, prev_out=outs)
            w_out = ev_w_out[j]
        else:
            w_in = od_w_in[j].astype(BF16)
            g_qn = jnp.tile(od_g_qn[j], H_C)[None, :]
            g_kn = jnp.tile(od_g_kn[j], G_C)[None, :]
            (cq, ck, cv, dq, dk, dv,
             states["c_k"], states["c_v"], states["d_k"], states["d_v"]) = _proj_call(
                _proj_odd_kernel, "proj_odd", x, mods, l, rt_odd, [w_in, g_qn, g_kn, ones_bd],
                (512, 128, 128, 512, 512, 512), (128, 128, 512, 512))
            o1 = _gqa(cq, ck, cv)
            o1 = _gqa(cq, ck, cv, cache=(lat3(cache_c_k[:, j], 128), lat3(cache_c_v[:, j], 128)),
                      prev_out=[o1])
            o2 = _mha(dq, dk, dv)
            bias = _na_bias(od_rpb[j])
            offs = jnp.arange(NA_ROWS + 1)[:, None] + jnp.arange(NA_ROWS)[None, :]
            bias_win = jnp.transpose(bias[:, offs], (1, 0, 3, 2, 4)).reshape(
                NA_ROWS + 1, H_D, GRID_W, NA_ROWS * GRID_W)
            o2 = _neighbourhood(dq, dk, dv, lat3(cache_d_k[:, j], 512), lat3(cache_d_v[:, j], 512),
                                bias_win, o2)
            w_out = od_w_out[j]

        x1, h2t, s1, s2 = _post(x, mods, l, o1, o2, w_out[:512].astype(BF16), w_out[512:].astype(BF16),
                                ln1_g[l][None, :], ln1_b[l][None, :], pk_w_q[l].astype(BF16),
                                pk_k1[l].astype(BF16), pk_k2[l].astype(BF16))
        x = _peer(h2t, s1, s2, u_bf, vt_bf, x1, mods, l, ln2_g[l][None, :], ln2_b[l][None, :])

    y = x[:T_CTX].reshape(N_CTX_B, CTX_S, D)
    z = x[T_CTX:].reshape(N_LAT_B, LAT_S, D)
    n_even = (DEPTH + 1) // 2
    n_odd = DEPTH // 2
    return (y, z,
            states["a_k"].reshape(N_CTX_B, n_even, CTX_S, H_A, 2, D_A),
            states["a_v"].reshape(N_CTX_B, n_even, CTX_S, H_A, 2 * D_A),
            states["b_ckv"].reshape(N_CTX_B, n_even, CTX_S, B_KVRANK),
            states["b_kr"].reshape(N_CTX_B, n_even, CTX_S, B_ROPE),
            states["c_k"].reshape(N_CTX_B, n_odd, CTX_S, G_C, D_C),
            states["c_v"].reshape(N_CTX_B, n_odd, CTX_S, G_C, D_C),
            states["d_k"].reshape(N_CTX_B, n_odd, CTX_S, H_D, D_D),
            states["d_v"].reshape(N_CTX_B, n_odd, CTX_S, H_D, D_D))
```

```python
import functools
import math

import jax
import jax.numpy as jnp
from jax import lax
from jax.experimental import pallas as pl
from jax.experimental.pallas import tpu as pltpu

F32 = jnp.float32
BF16 = jnp.bfloat16

D = 1024
N_CTX_B = 32
CTX_S = 256
N_LAT_B = 4
LAT_S = 1024
PAST = 256
T_CTX = N_CTX_B * CTX_S
T_LAT = N_LAT_B * LAT_S
T_ALL = T_CTX + T_LAT
DEPTH = 2
GRID_W = 64
GRID_ROWS = LAT_S // GRID_W
THETA = 10000.0
EPS = 1e-6
ALPHA = (2 * DEPTH) ** 0.25

H_A, D_A = 4, 64
H_B, B_NOPE, B_ROPE, B_V, B_QRANK, B_KVRANK = 8, 64, 32, 64, 256, 128
H_C, G_C, D_C = 8, 2, 64
H_D, D_D = 8, 64
NA_ROWS, NA_COLS = 8, 16
P_HEADS, P_KEYS, P_DKEY, P_TOPK = 8, 128, 256, 16
P_EXPERTS = P_KEYS * P_KEYS

EVEN_IN_PAD = 2048
ODD_IN = 2304

TM = 512
N_TILES = T_ALL // TM
N_CTX_TILES = T_CTX // TM
TILES_PER_LAT = LAT_S // TM
N_LAT_TILES = T_LAT // TM
TQ = 256
PEER_TT = 512
PEER_EB = 1024
PEER_NSEL = P_TOPK + 1
VMEM_LIMIT = 56 * 1024 * 1024

NEG_INF = float("-inf")


def _dot(a, b):
    return jnp.dot(a, b, preferred_element_type=F32)


def _dot_nt(a, b):
    return lax.dot_general(a, b, (((1,), (1,)), ((), ())), preferred_element_type=F32)


def _layer_norm(x, g, b):
    mu = jnp.mean(x, axis=-1, keepdims=True)
    xc = x - mu
    var = jnp.mean(xc * xc, axis=-1, keepdims=True)
    return xc * lax.rsqrt(var + EPS) * g + b


def _rms_norm(x, g):
    return x * lax.rsqrt(jnp.mean(x * x, axis=-1, keepdims=True) + EPS) * g


def _chunk_rms_norm(x, g, ones_bd, chunk):
    x2 = x * x
    hi = x2.astype(BF16)
    lo = (x2 - hi.astype(F32)).astype(BF16)
    ms = (_dot(hi, ones_bd) + _dot(lo, ones_bd)) * (1.0 / chunk)
    return x * lax.rsqrt(ms + EPS) * g


def _rope(x, cos_t, sin_t, chunk):
    n = x.shape[-1]
    half = chunk // 2
    lane = lax.broadcasted_iota(jnp.int32, x.shape, x.ndim - 1)
    swapped = jnp.where((lane % chunk) < half,
                        pltpu.roll(x, n - half, x.ndim - 1),
                        pltpu.roll(x, half, x.ndim - 1))
    return x * cos_t + swapped * sin_t


def _params(sem):
    return pltpu.CompilerParams(dimension_semantics=sem, vmem_limit_bytes=VMEM_LIMIT)


def _full(shape):
    zeros = (0,) * len(shape)
    return pl.BlockSpec(shape, lambda *_: zeros)


def _mod_row(i):
    return jnp.where(i < N_CTX_TILES, 0, 1 + (i - N_CTX_TILES) // TILES_PER_LAT)


def _rope_blk(i):
    return jnp.where(i < N_CTX_TILES, 0, 1 + (i - N_CTX_TILES) % TILES_PER_LAT)


def _mod_kernel(c_ref, w_ref, b_ref, o_ref):
    c = c_ref[...]
    a = c * (1.0 / (1.0 + jnp.exp(-c)))
    o_ref[0] = _dot(a.astype(BF16), w_ref[0].astype(BF16)) + b_ref[0]


def _modulation(cvecs, w_mod, b_mod):
    nb = 6
    bn = 6 * D // nb
    return pl.pallas_call(
        _mod_kernel,
        grid=(DEPTH, nb),
        in_specs=[pl.BlockSpec((8, D), lambda l, j: (0, 0)),
                  pl.BlockSpec((1, D, bn), lambda l, j: (l, 0, j)),
                  pl.BlockSpec((1, 1, bn), lambda l, j: (l, 0, j))],
        out_specs=pl.BlockSpec((1, 8, bn), lambda l, j: (l, 0, j)),
        out_shape=jax.ShapeDtypeStruct((DEPTH, 8, 6 * D), F32),
        compiler_params=_params(("arbitrary", "arbitrary")),
        name="modulation",
    )(cvecs, w_mod, b_mod.reshape(DEPTH, 1, 6 * D))


def _tile(g):
    return (g + N_CTX_TILES) % N_TILES


def _state_blk(g):
    return jnp.maximum(g - N_LAT_TILES, 0)


def _proj_even_kernel(x_ref, mod_ref, rt_ref, win_ref, gcq_ref, wuq_ref, gckv_ref, wukv_ref,
                      qa_ref, bq_ref, ka_ref, va_ref, kvb_ref, kr_ref,
                      s_ak_ref, s_av_ref, s_ckv_ref, s_kr_ref):
    x = x_ref[...]
    mod = mod_ref[0, 0]
    h = x * (1.0 + mod[1:2]) + mod[0:1]
    p = _dot(h.astype(BF16), win_ref[...])
    rt = rt_ref[0]
    cos_a, sin_a = rt[:, 0:512], rt[:, 512:1024]
    cos_b, sin_b = rt[:, 1024:1280], rt[:, 1280:1536]
    qa_ref[...] = _rope(p[:, 0:512], cos_a, sin_a, D_A).astype(BF16)
    ka = _rope(p[:, 512:1024], cos_a, sin_a, D_A)
    ka_ref[...] = ka.astype(BF16)
    s_ak_ref[...] = ka
    va = p[:, 1024:1536]
    va_ref[...] = va.astype(BF16)
    s_av_ref[...] = va
    cq = _rms_norm(p[:, 1536:1792], gcq_ref[...])
    bq = _dot(cq.astype(BF16), wuq_ref[...])
    bq_ref[:, 0:512] = bq[:, 0:512].astype(BF16)
    bq_ref[:, 512:768] = _rope(bq[:, 512:768], cos_b, sin_b, B_ROPE).astype(BF16)
    ckv = _rms_norm(p[:, 1792:1920], gckv_ref[...])
    s_ckv_ref[...] = ckv
    kvb_ref[...] = _dot(ckv.astype(BF16), wukv_ref[...]).astype(BF16)
    kr = _rope(p[:, 1920:2048], cos_b[:, 0:128], sin_b[:, 0:128], B_ROPE)[:, 0:B_ROPE]
    kr_ref[...] = kr.astype(BF16)
    s_kr_ref[...] = kr


def _proj_call(kernel_fn, name, x, mods, l, rt, weights, widths, state_widths):
    tok = lambda w: pl.BlockSpec((TM, w), lambda g: (_tile(g), 0))
    st = lambda w: pl.BlockSpec((TM, w), lambda g: (_state_blk(g), 0))
    return pl.pallas_call(
        kernel_fn,
        grid=(N_TILES,),
        in_specs=[tok(D),
                  pl.BlockSpec((1, 1, 6, D), lambda g: (l, _mod_row(_tile(g)), 0, 0)),
                  pl.BlockSpec((1, TM, rt.shape[-1]), lambda g: (_rope_blk(_tile(g)), 0, 0))]
                 + [_full(w.shape) for w in weights],
        out_specs=[tok(w) for w in widths] + [st(w) for w in state_widths],
        out_shape=[jax.ShapeDtypeStruct((T_ALL, w), BF16) for w in widths]
                  + [jax.ShapeDtypeStruct((T_CTX, w), F32) for w in state_widths],
        compiler_params=_params(("arbitrary",)),
        name=name,
    )(x, mods, rt, *weights)


def _proj_odd_kernel(x_ref, mod_ref, rt_ref, win_ref, gq_ref, gk_ref, bd_ref,
                     cq_ref, ck_ref, cv_ref, dq_ref, dk_ref, dv_ref,
                     s_ck_ref, s_cv_ref, s_dk_ref, s_dv_ref):
    x = x_ref[...]
    mod = mod_ref[0, 0]
    h = x * (1.0 + mod[1:2]) + mod[0:1]
    p = _dot(h.astype(BF16), win_ref[...])
    rt = rt_ref[0]
    cos_c, sin_c = rt[:, 0:512], rt[:, 512:1024]
    bd = bd_ref[...]
    cq = _chunk_rms_norm(p[:, 0:512], gq_ref[...], bd, D_C)
    cq_ref[...] = _rope(cq, cos_c, sin_c, D_C).astype(BF16)
    ck = _chunk_rms_norm(p[:, 512:640], gk_ref[...], bd[0:128, 0:128], D_C)
    ck = _rope(ck, cos_c[:, 0:128], sin_c[:, 0:128], D_C)
    ck_ref[...] = ck.astype(BF16)
    s_ck_ref[...] = ck
    cv = p[:, 640:768]
    cv_ref[...] = cv.astype(BF16)
    s_cv_ref[...] = cv
    dq_ref[...] = p[:, 768:1280].astype(BF16)
    dk = p[:, 1280:1792]
    dk_ref[...] = dk.astype(BF16)
    s_dk_ref[...] = dk
    dv = p[:, 1792:2304]
    dv_ref[...] = dv.astype(BF16)
    s_dv_ref[...] = dv


def _matmul_kernel(x_ref, w_ref, o_ref):
    o_ref[...] = _dot(x_ref[...].astype(BF16), w_ref[...]).astype(BF16)


def _matmul(x, w):
    m, k = x.shape
    n = w.shape[1]
    return pl.pallas_call(
        _matmul_kernel,
        grid=(m // TM,),
        in_specs=[pl.BlockSpec((TM, k), lambda i: (i, 0)), _full((k, n))],
        out_specs=pl.BlockSpec((TM, n), lambda i: (i, 0)),
        out_shape=jax.ShapeDtypeStruct((m, n), BF16),
        compiler_params=_params(("arbitrary",)),
        name="cache_kv_up",
    )(x, w)


def _exp_segments(segs):
    m = jnp.max(segs[0], axis=-1, keepdims=True)
    for s in segs[1:]:
        m = jnp.maximum(m, jnp.max(s, axis=-1, keepdims=True))
    es = [jnp.exp(s - m) for s in segs]
    den = jnp.sum(es[0], axis=-1, keepdims=True)
    for e in es[1:]:
        den = den + jnp.sum(e, axis=-1, keepdims=True)
    return es, den


def _pv(ps, vs):
    o = _dot(ps[0].astype(BF16), vs[0])
    for p, v in zip(ps[1:], vs[1:]):
        o = o + _dot(p.astype(BF16), v)
    return o


def _attend(segs, vs):
    es, den = _exp_segments(segs)
    return _pv(es, vs) / den


def _scaled(q, scale):
    return (q.astype(F32) * scale).astype(q.dtype)


def _attn_specs(row0, n_req, n_q, widths_q, widths_k, cache_shapes):
    n_own = n_q
    nqb = n_q // TQ
    q0 = row0 // TQ
    k0 = row0 // n_own
    qspec = lambda w: pl.BlockSpec((TQ, w), lambda b, j: (q0 + b * nqb + j, 0))
    kspec = lambda w: pl.BlockSpec((n_own, w), lambda b, j: (k0 + b, 0))
    cspec = lambda shp: pl.BlockSpec((1,) + tuple(shp[1:]), lambda b, j: (b,) + (0,) * (len(shp) - 1))
    specs = [qspec(w) for w in widths_q] + [kspec(w) for w in widths_k] + [cspec(s) for s in cache_shapes]
    return specs, qspec, (n_req, nqb)


def _attn_even_kernel(*refs, lam_init, has_cache):
    lq1_ref, lk1_ref, lq2_ref, lk2_ref, gsub_ref, qa_ref, bq_ref, ka_ref, va_ref, kvb_ref, kr_ref = refs[:11]
    if has_cache:
        cka_ref, cva_ref, ckvb_ref, ckr_ref = refs[11:15]
    oa_ref, ob_ref = refs[-2:]
    lam = (jnp.exp(jnp.sum(lq1_ref[...] * lk1_ref[...], axis=-1, keepdims=True))
           - jnp.exp(jnp.sum(lq2_ref[...] * lk2_ref[...], axis=-1, keepdims=True)) + lam_init)
    gsub = gsub_ref[...]
    for hd in range(H_A):
        vsl = slice(hd * 2 * D_A, (hd + 1) * 2 * D_A)
        vs = [va_ref[:, vsl]] + ([cva_ref[0, :, vsl].astype(BF16)] if has_cache else [])
        os = []
        for m in range(2):
            sl = slice(hd * 2 * D_A + m * D_A, hd * 2 * D_A + (m + 1) * D_A)
            q = _scaled(qa_ref[:, sl], D_A ** -0.5)
            segs = [_dot_nt(q, ka_ref[:, sl])]
            if has_cache:
                segs.append(_dot_nt(q, cka_ref[0, :, sl].astype(BF16)))
            os.append(_attend(segs, vs))
        o = os[0] - lam * os[1]
        oa_ref[:, vsl] = (_rms_norm(o, gsub) * (1.0 - lam_init)).astype(BF16)
    kr = kr_ref[...]
    if has_cache:
        ckr = ckr_ref[0].astype(BF16)
    sc = (B_NOPE + B_ROPE) ** -0.5
    for hd in range(H_B):
        nsl = slice(hd * B_NOPE, (hd + 1) * B_NOPE)
        vsl = slice(512 + hd * B_V, 512 + (hd + 1) * B_V)
        qn = bq_ref[:, nsl]
        qr = bq_ref[:, 512 + hd * B_ROPE:512 + (hd + 1) * B_ROPE]
        segs = [(_dot_nt(qn, kvb_ref[:, nsl]) + _dot_nt(qr, kr)) * sc]
        vs = [kvb_ref[:, vsl]]
        if has_cache:
            segs.append((_dot_nt(qn, ckvb_ref[0, :, nsl]) + _dot_nt(qr, ckr)) * sc)
            vs.append(ckvb_ref[0, :, vsl])
        ob_ref[:, hd * B_V:(hd + 1) * B_V] = _attend(segs, vs).astype(BF16)


def _attn_call(kernel_fn, name, params, arrays, widths_q, widths_k, latent, cache, prev_out, n_out):
    row0, n_req, n_q = (T_CTX, N_LAT_B, LAT_S) if latent else (0, N_CTX_B, CTX_S)
    cache = list(cache)
    cache_shapes = []
    for a in cache:
        cache_shapes.append(a.shape if a.ndim == 3 else (n_req,) + (a.shape[0] // n_req, a.shape[1]))
    cache = [a.reshape(shp) for a, shp in zip(cache, cache_shapes)]
    specs, qspec, grid = _attn_specs(row0, n_req, n_q, widths_q, widths_k, cache_shapes)
    in_specs = [_full(p.shape) for p in params] + specs
    operands = list(params) + list(arrays) + cache
    aliases = {}
    if prev_out is not None:
        for i, o in enumerate(prev_out):
            aliases[len(operands)] = i
            operands.append(o)
            in_specs.append(pl.BlockSpec(memory_space=pl.ANY))
    return pl.pallas_call(
        kernel_fn,
        grid=grid,
        in_specs=in_specs,
        out_specs=[qspec(512)] * n_out,
        out_shape=[jax.ShapeDtypeStruct((T_ALL, 512), BF16)] * n_out,
        input_output_aliases=aliases,
        compiler_params=_params(("arbitrary", "arbitrary")),
        name=name,
    )(*operands)


def _attn_even(lam_vecs, g_sub, lam_init, qa, bq, ka, va, kvb, kr, cache=(), prev_out=None):
    latent = prev_out is not None
    fn = functools.partial(_attn_even_kernel, lam_init=lam_init, has_cache=latent)
    return _attn_call(fn, "attn_even", list(lam_vecs) + [g_sub], [qa, bq, ka, va, kvb, kr],
                      (512, 768), (512, 512, 1024, B_ROPE), latent, cache, prev_out, 2)


def _gqa_kernel(*refs, has_cache):
    cq_ref, ck_ref, cv_ref = refs[:3]
    if has_cache:
        cck_ref, ccv_ref = refs[3:5]
    o_ref = refs[-1]
    for hd in range(H_C):
        g = hd // (H_C // G_C)
        gsl = slice(g * D_C, (g + 1) * D_C)
        q = _scaled(cq_ref[:, hd * D_C:(hd + 1) * D_C], D_C ** -0.5)
        segs = [_dot_nt(q, ck_ref[:, gsl])]
        vs = [cv_ref[:, gsl]]
        if has_cache:
            segs.append(_dot_nt(q, cck_ref[0, :, gsl].astype(BF16)))
            vs.append(ccv_ref[0, :, gsl].astype(BF16))
        o_ref[:, hd * D_C:(hd + 1) * D_C] = _attend(segs, vs).astype(BF16)


def _gqa(cq, ck, cv, cache=(), prev_out=None):
    latent = prev_out is not None
    fn = functools.partial(_gqa_kernel, has_cache=latent)
    return _attn_call(fn, "attn_gqa", [], [cq, ck, cv], (512,), (128, 128), latent, cache, prev_out, 1)[0]


def _mha_kernel(q_ref, k_ref, v_ref, o_ref):
    for hd in range(H_D):
        sl = slice(hd * D_D, (hd + 1) * D_D)
        segs = [_dot_nt(_scaled(q_ref[:, sl], D_D ** -0.5), k_ref[:, sl])]
        o_ref[:, sl] = _attend(segs, [v_ref[:, sl]]).astype(BF16)


def _mha(q, k, v):
    return _attn_call(_mha_kernel, "attn_mha", [], [q, k, v], (512,), (512, 512), False, (), None, 1)[0]


def _na_bias_kernel(rpb_ref, o_ref):
    hd = pl.program_id(0)
    qc = lax.broadcasted_iota(jnp.int32, (GRID_W, GRID_W), 0)
    kc = lax.broadcasted_iota(jnp.int32, (GRID_W, GRID_W), 1)
    dc = jnp.clip(kc - qc, -(NA_COLS - 1), NA_COLS - 1) + (NA_COLS - 1)
    c0 = jnp.clip(qc - NA_COLS // 2, 0, GRID_W - NA_COLS)
    n_dc = 2 * NA_COLS - 1
    n_dr = 2 * NA_ROWS - 1
    for dr in range(n_dr):
        acc = jnp.zeros((GRID_W, GRID_W), F32)
        for j in range(n_dc):
            acc = jnp.where(dc == j, rpb_ref[(hd * n_dr + dr) * n_dc + j], acc)
        o_ref[0, dr] = jnp.where(kc >= c0, jnp.where(kc < c0 + NA_COLS, acc, NEG_INF), NEG_INF)


def _na_bias(rpb):
    n_dr = 2 * NA_ROWS - 1
    return pl.pallas_call(
        _na_bias_kernel,
        grid=(H_D,),
        in_specs=[pl.BlockSpec(memory_space=pltpu.SMEM)],
        out_specs=pl.BlockSpec((1, n_dr, GRID_W, GRID_W), lambda h: (h, 0, 0, 0)),
        out_shape=jax.ShapeDtypeStruct((H_D, n_dr, GRID_W, GRID_W), F32),
        compiler_params=_params(("arbitrary",)),
        name="na_bias",
    )(rpb.reshape(-1))


def _na_window_start(r):
    return jnp.clip(r - NA_ROWS // 2, 0, GRID_ROWS - NA_ROWS)


def _na_kernel(q_ref, k_ref, v_ref, ck_ref, cv_ref, bias_ref, prev_ref, o_ref):
    r = pl.program_id(1)
    start = pl.multiple_of(_na_window_start(r) * GRID_W, GRID_W)
    n_loc = NA_ROWS * GRID_W
    scale = D_D ** -0.5
    heads = [slice(hd * D_D, (hd + 1) * D_D) for hd in range(H_D)]
    segs = []
    for hd, sl in enumerate(heads):
        q = _scaled(q_ref[:, sl], scale)
        segs.append([_dot_nt(q, k_ref[pl.ds(start, n_loc), sl]) + bias_ref[0, hd],
                     _dot_nt(q, ck_ref[0, :, sl].astype(BF16))])
    exps = [_exp_segments(s) for s in segs]
    for sl, (es, den) in zip(heads, exps):
        o = _pv(es, [v_ref[pl.ds(start, n_loc), sl], cv_ref[0, :, sl].astype(BF16)]) / den
        o_ref[:, sl] = o.astype(BF16)


def _neighbourhood(q, k, v, ck, cv, bias_win, prev_out):
    n_loc = NA_ROWS * GRID_W
    q0 = T_CTX // GRID_W
    k0 = T_CTX // LAT_S
    qspec = pl.BlockSpec((GRID_W, 512), lambda b, r: (q0 + b * GRID_ROWS + r, 0))
    kspec = pl.BlockSpec((LAT_S, 512), lambda b, r: (k0 + b, 0))
    cspec = pl.BlockSpec((1, PAST, 512), lambda b, r: (b, 0, 0))
    return pl.pallas_call(
        _na_kernel,
        grid=(N_LAT_B, GRID_ROWS),
        in_specs=[qspec, kspec, kspec, cspec, cspec,
                  pl.BlockSpec((1, H_D, GRID_W, n_loc),
                               lambda b, r: (_na_window_start(r) - r + NA_ROWS - 1, 0, 0, 0)),
                  pl.BlockSpec(memory_space=pl.ANY)],
        out_specs=qspec,
        out_shape=jax.ShapeDtypeStruct((T_ALL, 512), BF16),
        input_output_aliases={6: 0},
        compiler_params=_params(("arbitrary", "arbitrary")),
        name="attn_neighbourhood",
    )(q, k, v, ck, cv, bias_win, prev_out)


def _post_kernel(x_ref, mod_ref, o1_ref, o2_ref, wo1_ref, wo2_ref, g1_ref, b1_ref,
                 wq_ref, k1_ref, k2_ref, x1_ref, h2t_ref, s1_ref, s2_ref):
    mod = mod_ref[0, 0]
    y = _dot(o1_ref[...], wo1_ref[...]) + _dot(o2_ref[...], wo2_ref[...])
    x1 = _layer_norm(ALPHA * x_ref[...] + mod[2:3] * y, g1_ref[...], b1_ref[...])
    x1_ref[...] = x1
    h2f = x1 * (1.0 + mod[4:5]) + mod[3:4]
    h2 = h2f.astype(BF16)
    h2t_ref[...] = h2f.T.astype(BF16)
    q = _dot(h2, wq_ref[...]).astype(BF16)
    half = P_DKEY // 2
    k1 = k1_ref[...]
    k2 = k2_ref[...]
    for hd in range(P_HEADS):
        s1_ref[hd] = _dot_nt(k1, q[:, hd * P_DKEY:hd * P_DKEY + half])
        s2_ref[hd] = _dot_nt(k2, q[:, hd * P_DKEY + half:(hd + 1) * P_DKEY])


def _post(x, mods, l, o1, o2, wo1, wo2, g1, b1, wq, k1, k2):
    tok = lambda w: pl.BlockSpec((TM, w), lambda i: (i, 0))
    sspec = pl.BlockSpec((P_HEADS, P_KEYS, TM), lambda i: (0, 0, i))
    return pl.pallas_call(
        _post_kernel,
        grid=(N_TILES,),
        in_specs=[tok(D),
                  pl.BlockSpec((1, 1, 6, D), lambda i: (l, _mod_row(i), 0, 0)),
                  tok(512), tok(512), _full((512, D)), _full((512, D)), _full((1, D)), _full((1, D)),
                  _full((D, P_HEADS * P_DKEY)), _full((P_KEYS, P_DKEY // 2)), _full((P_KEYS, P_DKEY // 2))],
        out_specs=[tok(D), pl.BlockSpec((D, TM), lambda i: (0, i)), sspec, sspec],
        out_shape=[jax.ShapeDtypeStruct((T_ALL, D), F32),
                   jax.ShapeDtypeStruct((D, T_ALL), BF16),
                   jax.ShapeDtypeStruct((P_HEADS, P_KEYS, T_ALL), F32),
                   jax.ShapeDtypeStruct((P_HEADS, P_KEYS, T_ALL), F32)],
        compiler_params=_params(("arbitrary",)),
        name="post_mixer",
    )(x, mods, o1, o2, wo1, wo2, g1, b1, wq, k1, k2)


def _tables_kernel(u_ref, v_ref, ub_ref, vt_ref):
    ub_ref[0] = u_ref[0].astype(BF16)
    vt_ref[0] = v_ref[0].T.astype(BF16)


def _prep_tables(pk_u, pk_v):
    blk = pl.BlockSpec((1, PEER_EB, D), lambda l, e: (l, e, 0))
    return pl.pallas_call(
        _tables_kernel,
        grid=(DEPTH, P_EXPERTS // PEER_EB),
        in_specs=[blk, blk],
        out_specs=[blk, pl.BlockSpec((1, D, PEER_EB), lambda l, e: (l, 0, e))],
        out_shape=[jax.ShapeDtypeStruct((DEPTH, P_EXPERTS, D), BF16),
                   jax.ShapeDtypeStruct((DEPTH, D, P_EXPERTS), BF16)],
        compiler_params=_params(("arbitrary", "arbitrary")),
        name="expert_tables",
    )(pk_u, pk_v)


def _sorting_network(n):
    pairs = []
    p = 1
    while p < n:
        k = p
        while k >= 1:
            for j in range(k % p, n - k, 2 * k):
                for i in range(min(k, n - j - k)):
                    if (i + j) // (2 * p) == (i + j + k) // (2 * p):
                        pairs.append((i + j, i + j + k))
            k //= 2
        p *= 2
    return pairs


def _top_values(x, n):
    depth = x.shape[0] // 8
    groups = [x[8 * k:8 * k + 8, :] for k in range(depth)]
    size = 1
    while size < depth:
        size *= 2
    bottom = jnp.full_like(groups[0], NEG_INF)
    groups = groups + [bottom] * (size - depth)
    for i, j in _sorting_network(size):
        a, b = groups[i], groups[j]
        groups[i], groups[j] = jnp.maximum(a, b), jnp.minimum(a, b)
    groups = groups[:depth]
    vals = []
    for t in range(n):
        head = groups[0]
        m = jnp.max(head, axis=0, keepdims=True)
        vals.append(m)
        taken = head == m
        for k in range(min(n - 1 - t, depth)):
            groups[k] = jnp.where(taken, groups[k + 1] if k + 1 < depth else bottom, groups[k])
    return vals


PEER_PAIRS = [(a, b) for a in range(PEER_NSEL) for b in range(PEER_NSEL) if (a + 1) * (b + 1) <= PEER_NSEL]
PEER_NCAND = -(-len(PEER_PAIRS) // 8) * 8


def _peer_select(s1_ref, s2_ref, eth_scr, e1_scr, e2_scr, cand_scr):
    n_pad = PEER_NCAND - len(PEER_PAIRS)
    cand_scr[len(PEER_PAIRS):, :] = jnp.full((n_pad, cand_scr.shape[1]), NEG_INF, F32)

    def body(hd, carry):
        s1 = s1_ref[hd]
        s2 = s2_ref[hd]
        v1 = _top_values(s1, PEER_NSEL)
        v2 = _top_values(s2, PEER_NSEL)
        for k, (a, b) in enumerate(PEER_PAIRS):
            cand_scr[k:k + 1, :] = v1[a] + v2[b]
        cs = _top_values(cand_scr[...], PEER_NSEL)
        tau = 0.5 * (cs[P_TOPK - 1] + cs[P_TOPK])
        z = jnp.exp(cs[0] - cs[0])
        for j in range(1, P_TOPK):
            z = z + jnp.exp(cs[j] - cs[0])
        half_inv_z = 0.5 / z
        eth_scr[hd] = jnp.exp((tau - s1) - v2[0]) * half_inv_z
        e1_scr[hd] = jnp.exp(s1 - v1[0])
        e2_scr[hd] = jnp.exp(s2 - v2[0]) * half_inv_z
        return carry

    lax.fori_loop(0, P_HEADS, body, 0)


def _gelu2(a):
    return a * (1.0 + lax.erf(a * (2.0 ** -0.5)))


PEER_LW = 128
PEER_CH = 256


def _peer_kernel(h2t_ref, s1_ref, s2_ref, u_ref, vt_prev_ref, vt_last_ref, x1_ref, mod_ref, g2_ref, b2_ref,
                 o_ref, eth_scr, e1_scr, e2_scr, cand_scr, c_cur, c_prev, acc_scr):
    eb = pl.program_id(1)
    n_eb = pl.num_programs(1)
    rows_per_step = PEER_EB // P_KEYS

    @pl.when(eb == 0)
    def _():
        _peer_select(s1_ref, s2_ref, eth_scr, e1_scr, e2_scr, cand_scr)
        acc_scr[...] = jnp.zeros_like(acc_scr)
        c_prev[...] = jnp.zeros_like(c_prev)

    @pl.when(eb > 0)
    def _():
        c_prev[...] = c_cur[...]

    h2t = h2t_ref[...]
    i1_step = pl.multiple_of(eb * rows_per_step, rows_per_step)
    n_ch = PEER_EB // PEER_CH
    rb = D // n_ch
    act_next = _dot(u_ref[0:PEER_CH, :], h2t)
    for k in range(n_ch):
        act = act_next
        if k + 1 < n_ch:
            act_next = _dot(u_ref[(k + 1) * PEER_CH:(k + 2) * PEER_CH, :], h2t)
        acc_scr[k * rb:(k + 1) * rb, :] += _dot(vt_prev_ref[k * rb:(k + 1) * rb, :], c_prev[...])
        nq = PEER_CH // P_KEYS
        for lc in range(PEER_TT // PEER_LW):
            ls = slice(lc * PEER_LW, (lc + 1) * PEER_LW)
            ws = [jnp.zeros((P_KEYS, PEER_LW), F32) for _ in range(nq)]
            for hd in range(P_HEADS):
                e2 = e2_scr[hd, :, ls]
                eth_rows = eth_scr[hd, pl.ds(i1_step, rows_per_step), ls]
                e1_rows = e1_scr[hd, pl.ds(i1_step, rows_per_step), ls]
                for q in range(nq):
                    r = k * nq + q
                    ws[q] = ws[q] + jnp.where(e2 >= eth_rows[r:r + 1, :], e2, 0.0) * e1_rows[r:r + 1, :]
            for q in range(nq):
                j = k * nq + q
                a = act[q * P_KEYS:(q + 1) * P_KEYS, ls]
                c_cur[j * P_KEYS:(j + 1) * P_KEYS, ls] = (ws[q] * _gelu2(a)).astype(BF16)

    @pl.when(eb == n_eb - 1)
    def _():
        mod = mod_ref[0, 0]
        peer = (acc_scr[...] + _dot(vt_last_ref[...], c_cur[...])).T
        o_ref[...] = _layer_norm(ALPHA * x1_ref[...] + mod[5:6] * peer, g2_ref[...], b2_ref[...])


def _peer(h2t, s1, s2, u_bf, vt_bf, x1, mods, l, g2, b2):
    tiles_ctx = T_CTX // PEER_TT
    tiles_per_lat = LAT_S // PEER_TT
    n_eb = P_EXPERTS // PEER_EB
    mod_row = lambda i: jnp.where(i < tiles_ctx, 0, 1 + (i - tiles_ctx) // tiles_per_lat)
    tok = lambda w: pl.BlockSpec((PEER_TT, w), lambda i, e: (i, 0))
    sspec = pl.BlockSpec((P_HEADS, P_KEYS, PEER_TT), lambda i, e: (0, 0, i))
    sel = pltpu.VMEM((P_HEADS, P_KEYS, PEER_TT), F32)
    return pl.pallas_call(
        _peer_kernel,
        grid=(T_ALL // PEER_TT, n_eb),
        in_specs=[pl.BlockSpec((D, PEER_TT), lambda i, e: (0, i)), sspec, sspec,
                  pl.BlockSpec((None, PEER_EB, D), lambda i, e: (l, e, 0)),
                  pl.BlockSpec((None, D, PEER_EB), lambda i, e: (l, 0, jnp.maximum(e - 1, 0))),
                  pl.BlockSpec((None, D, PEER_EB), lambda i, e: (l, 0, jnp.where(e == n_eb - 1, n_eb - 1, 0))),
                  tok(D),
                  pl.BlockSpec((1, 1, 6, D), lambda i, e: (l, mod_row(i), 0, 0)),
                  pl.BlockSpec((1, D), lambda i, e: (0, 0)),
                  pl.BlockSpec((1, D), lambda i, e: (0, 0))],
        out_specs=tok(D),
        out_shape=jax.ShapeDtypeStruct((T_ALL, D), F32),
        scratch_shapes=[sel, sel, sel,
                        pltpu.VMEM((PEER_NCAND, PEER_TT), F32),
                        pltpu.VMEM((PEER_EB, PEER_TT), BF16),
                        pltpu.VMEM((PEER_EB, PEER_TT), BF16),
                        pltpu.VMEM((D, PEER_TT), F32)],
        compiler_params=_params(("arbitrary", "arbitrary")),
        name="peer",
    )(h2t, s1, s2, u_bf, vt_bf, vt_bf, x1, mods, g2, b2)


def _grid_angles(rot_dim):
    t = jnp.arange(LAT_S)
    row = (t // GRID_W).astype(F32)
    col = (t % GRID_W).astype(F32)
    n_freq = rot_dim // 4
    inv = THETA ** (-jnp.arange(n_freq, dtype=F32) / n_freq)
    return jnp.concatenate([row[:, None] * inv, col[:, None] * inv], axis=-1)


def _rope_tables(rot_dim, width):
    ang = _grid_angles(rot_dim)
    cos = jnp.cos(ang)
    sin = jnp.sin(ang)
    cos_t = jnp.tile(jnp.concatenate([cos, cos], axis=-1), (1, width // rot_dim))
    sin_t = jnp.tile(jnp.concatenate([-sin, sin], axis=-1), (1, width // rot_dim))
    ident_c = jnp.ones((1, TM, width), F32)
    ident_s = jnp.zeros((1, TM, width), F32)
    cos_t = jnp.concatenate([ident_c, cos_t.reshape(TILES_PER_LAT, TM, width)], axis=0)
    sin_t = jnp.concatenate([ident_s, sin_t.reshape(TILES_PER_LAT, TM, width)], axis=0)
    return cos_t, sin_t


def kernel(x_prompt, x_sample, cache_a_k, cache_a_v, cache_b_ckv, cache_b_kr, cache_c_k, cache_c_v, cache_d_k, cache_d_v, c, c_ctx, w_mod, b_mod, ln1_g, ln1_b, ln2_g, ln2_b, ev_w_in, ev_lam_q1, ev_lam_k1, ev_lam_q2, ev_lam_k2, ev_g_sub, ev_g_cq, ev_w_uq, ev_g_ckv, ev_w_ukv, ev_w_out, od_w_in, od_g_qn, od_g_kn, od_rpb, od_w_out, pk_w_q, pk_k1, pk_k2, pk_u, pk_v):
    x = jnp.concatenate([x_prompt.reshape(T_CTX, D), x_sample.reshape(T_LAT, D)], axis=0)
    cvecs = jnp.concatenate([c_ctx[None, :], c, jnp.zeros((8 - 1 - N_LAT_B, D), F32)], axis=0)
    mods = _modulation(cvecs, w_mod, b_mod).reshape(DEPTH, 8, 6, D)
    u_bf, vt_bf = _prep_tables(pk_u, pk_v)

    cos_a, sin_a = _rope_tables(D_A, 512)
    cos_b, sin_b = _rope_tables(B_ROPE, 256)
    rt_even = jnp.concatenate([cos_a, sin_a, cos_b, sin_b], axis=-1)
    rt_odd = jnp.concatenate(_rope_tables(D_C, 512), axis=-1)
    ones_bd = jnp.kron(jnp.eye(512 // D_C, dtype=F32), jnp.ones((D_C, D_C), F32)).astype(BF16)
    lat3 = lambda a, w: a.reshape(N_LAT_B, PAST, w)

    states = {}
    for l in range(DEPTH):
        j = l // 2
        if l % 2 == 0:
            lam_init = 0.8 - 0.6 * math.exp(-0.3 * l)
            w_in = jnp.pad(ev_w_in[j], ((0, 0), (0, EVEN_IN_PAD - ev_w_in.shape[-1]))).astype(BF16)
            wuq = ev_w_uq[j].reshape(B_QRANK, H_B, B_NOPE + B_ROPE)
            wuq = jnp.concatenate([wuq[:, :, :B_NOPE].reshape(B_QRANK, -1),
                                   wuq[:, :, B_NOPE:].reshape(B_QRANK, -1)], axis=1).astype(BF16)
            wukv = ev_w_ukv[j].reshape(B_KVRANK, H_B, B_NOPE + B_V)
            wukv = jnp.concatenate([wukv[:, :, :B_NOPE].reshape(B_KVRANK, -1),
                                    wukv[:, :, B_NOPE:].reshape(B_KVRANK, -1)], axis=1).astype(BF16)
            (qa, bq, ka, va, kvb, kr,
             states["a_k"], states["a_v"], states["b_ckv"], states["b_kr"]) = _proj_call(
                _proj_even_kernel, "proj_even", x, mods, l, rt_even,
                [w_in, ev_g_cq[j][None, :], wuq, ev_g_ckv[j][None, :], wukv],
                (512, 768, 512, 512, 1024, B_ROPE), (512, 512, B_KVRANK, B_ROPE))
            lam_vecs = [v[j][None, :] for v in (ev_lam_q1, ev_lam_k1, ev_lam_q2, ev_lam_k2)]
            g_sub = ev_g_sub[j][None, :]
            proj = (qa, bq, ka, va, kvb, kr)
            outs = _attn_even(lam_vecs, g_sub, lam_init, *proj)
            cache_kvb = _matmul(cache_b_ckv[:, j].reshape(N_LAT_B * PAST, B_KVRANK), wukv)
            cache = (lat3(cache_a_k[:, j], 512), lat3(cache_a_v[:, j], 512), cache_kvb,
                     lat3(cache_b_kr[:, j], B_ROPE))
            o1, o2 = _attn_even(lam_vecs, g_sub, lam_init, *proj, cache=cache, prev_out=outs)
            w_out = ev_w_out[j]
        else:
            w_in = od_w_in[j].astype(BF16)
            g_qn = jnp.tile(od_g_qn[j], H_C)[None, :]
            g_kn = jnp.tile(od_g_kn[j], G_C)[None, :]
            (cq, ck, cv, dq, dk, dv,
             states["c_k"], states["c_v"], states["d_k"], states["d_v"]) = _proj_call(
                _proj_odd_kernel, "proj_odd", x, mods, l, rt_odd, [w_in, g_qn, g_kn, ones_bd],
                (512, 128, 128, 512, 512, 512), (128, 128, 512, 512))
            o1 = _gqa(cq, ck, cv)
            o1 = _gqa(cq, ck, cv, cache=(lat3(cache_c_k[:, j], 128), lat3(cache_c_v[:, j], 128)),
                      prev_out=[o1])
            o2 = _mha(dq, dk, dv)
            bias = _na_bias(od_rpb[j])
            offs = jnp.arange(NA_ROWS + 1)[:, None] + jnp.arange(NA_ROWS)[None, :]
            bias_win = jnp.transpose(bias[:, offs], (1, 0, 3, 2, 4)).reshape(
                NA_ROWS + 1, H_D, GRID_W, NA_ROWS * GRID_W)
            o2 = _neighbourhood(dq, dk, dv, lat3(cache_d_k[:, j], 512), lat3(cache_d_v[:, j], 512),
                                bias_win, o2)
            w_out = od_w_out[j]

        x1, h2t, s1, s2 = _post(x, mods, l, o1, o2, w_out[:512].astype(BF16), w_out[512:].astype(BF16),
                                ln1_g[l][None, :], ln1_b[l][None, :], pk_w_q[l].astype(BF16),
                                pk_k1[l].astype(BF16), pk_k2[l].astype(BF16))
        x = _peer(h2t, s1, s2, u_bf, vt_bf, x1, mods, l, ln2_g[l][None, :], ln2_b[l][None, :])

    y = x[:T_CTX].reshape(N_CTX_B, CTX_S, D)
    z = x[T_CTX:].reshape(N_LAT_B, LAT_S, D)
    n_even = (DEPTH + 1) // 2
    n_odd = DEPTH // 2
    return (y, z,
            states["a_k"].reshape(N_CTX_B, n_even, CTX_S, H_A, 2, D_A),
            states["a_v"].reshape(N_CTX_B, n_even, CTX_S, H_A, 2 * D_A),
            states["b_ckv"].reshape(N_CTX_B, n_even, CTX_S, B_KVRANK),
            states["b_kr"].reshape(N_CTX_B, n_even, CTX_S, B_ROPE),
            states["c_k"].reshape(N_CTX_B, n_odd, CTX_S, G_C, D_C),
            states["c_v"].reshape(N_CTX_B, n_odd, CTX_S, G_C, D_C),
            states["d_k"].reshape(N_CTX_B, n_odd, CTX_S, H_D, D_D),
            states["d_v"].reshape(N_CTX_B, n_odd, CTX_S, H_D, D_D))
```

```python
import functools
import math

import jax
import jax.numpy as jnp
from jax import lax
from jax.experimental import pallas as pl
from jax.experimental.pallas import tpu as pltpu

F32 = jnp.float32
BF16 = jnp.bfloat16

D = 1024
N_CTX_B = 32
CTX_S = 256
N_LAT_B = 4
LAT_S = 1024
PAST = 256
T_CTX = N_CTX_B * CTX_S
T_LAT = N_LAT_B * LAT_S
T_ALL = T_CTX + T_LAT
DEPTH = 2
GRID_W = 64
GRID_ROWS = LAT_S // GRID_W
THETA = 10000.0
EPS = 1e-6
ALPHA = (2 * DEPTH) ** 0.25

H_A, D_A = 4, 64
H_B, B_NOPE, B_ROPE, B_V, B_QRANK, B_KVRANK = 8, 64, 32, 64, 256, 128
H_C, G_C, D_C = 8, 2, 64
H_D, D_D = 8, 64
NA_ROWS, NA_COLS = 8, 16
P_HEADS, P_KEYS, P_DKEY, P_TOPK = 8, 128, 256, 16
P_EXPERTS = P_KEYS * P_KEYS

EVEN_IN_PAD = 2048
ODD_IN = 2304

TM = 512
N_TILES = T_ALL // TM
N_CTX_TILES = T_CTX // TM
TILES_PER_LAT = LAT_S // TM
N_LAT_TILES = T_LAT // TM
TQ = 256
PEER_TT = 512
PEER_EB = 1024
PEER_NSEL = P_TOPK + 1
VMEM_LIMIT = 56 * 1024 * 1024

NEG_INF = float("-inf")


def _dot(a, b):
    return jnp.dot(a, b, preferred_element_type=F32)


def _dot_nt(a, b):
    return lax.dot_general(a, b, (((1,), (1,)), ((), ())), preferred_element_type=F32)


def _layer_norm(x, g, b):
    mu = jnp.mean(x, axis=-1, keepdims=True)
    xc = x - mu
    var = jnp.mean(xc * xc, axis=-1, keepdims=True)
    return xc * lax.rsqrt(var + EPS) * g + b


def _rms_norm(x, g):
    return x * lax.rsqrt(jnp.mean(x * x, axis=-1, keepdims=True) + EPS) * g


def _chunk_rms_norm(x, g, ones_bd, chunk):
    x2 = x * x
    hi = x2.astype(BF16)
    lo = (x2 - hi.astype(F32)).astype(BF16)
    ms = (_dot(hi, ones_bd) + _dot(lo, ones_bd)) * (1.0 / chunk)
    return x * lax.rsqrt(ms + EPS) * g


def _rope(x, cos_t, sin_t, chunk):
    n = x.shape[-1]
    half = chunk // 2
    lane = lax.broadcasted_iota(jnp.int32, x.shape, x.ndim - 1)
    swapped = jnp.where((lane % chunk) < half,
                        pltpu.roll(x, n - half, x.ndim - 1),
                        pltpu.roll(x, half, x.ndim - 1))
    return x * cos_t + swapped * sin_t


def _params(sem):
    return pltpu.CompilerParams(dimension_semantics=sem, vmem_limit_bytes=VMEM_LIMIT)


def _full(shape):
    zeros = (0,) * len(shape)
    return pl.BlockSpec(shape, lambda *_: zeros)


def _mod_row(i):
    return jnp.where(i < N_CTX_TILES, 0, 1 + (i - N_CTX_TILES) // TILES_PER_LAT)


def _rope_blk(i):
    return jnp.where(i < N_CTX_TILES, 0, 1 + (i - N_CTX_TILES) % TILES_PER_LAT)


def _mod_kernel(c_ref, w_ref, b_ref, o_ref):
    c = c_ref[...]
    a = c * (1.0 / (1.0 + jnp.exp(-c)))
    o_ref[0] = _dot(a.astype(BF16), w_ref[0].astype(BF16)) + b_ref[0]


def _modulation(cvecs, w_mod, b_mod):
    nb = 6
    bn = 6 * D // nb
    return pl.pallas_call(
        _mod_kernel,
        grid=(DEPTH, nb),
        in_specs=[pl.BlockSpec((8, D), lambda l, j: (0, 0)),
                  pl.BlockSpec((1, D, bn), lambda l, j: (l, 0, j)),
                  pl.BlockSpec((1, 1, bn), lambda l, j: (l, 0, j))],
        out_specs=pl.BlockSpec((1, 8, bn), lambda l, j: (l, 0, j)),
        out_shape=jax.ShapeDtypeStruct((DEPTH, 8, 6 * D), F32),
        compiler_params=_params(("arbitrary", "arbitrary")),
        name="modulation",
    )(cvecs, w_mod, b_mod.reshape(DEPTH, 1, 6 * D))


def _tile(g):
    return (g + N_CTX_TILES) % N_TILES


def _state_blk(g):
    return jnp.maximum(g - N_LAT_TILES, 0)


def _proj_even_kernel(x_ref, mod_ref, rt_ref, win_ref, gcq_ref, wuq_ref, gckv_ref, wukv_ref,
                      qa_ref, bq_ref, ka_ref, va_ref, kvb_ref, kr_ref,
                      s_ak_ref, s_av_ref, s_ckv_ref, s_kr_ref):
    x = x_ref[...]
    mod = mod_ref[0, 0]
    h = x * (1.0 + mod[1:2]) + mod[0:1]
    p = _dot(h.astype(BF16), win_ref[...])
    rt = rt_ref[0]
    cos_a, sin_a = rt[:, 0:512], rt[:, 512:1024]
    cos_b, sin_b = rt[:, 1024:1280], rt[:, 1280:1536]
    qa_ref[...] = _rope(p[:, 0:512], cos_a, sin_a, D_A).astype(BF16)
    ka = _rope(p[:, 512:1024], cos_a, sin_a, D_A)
    ka_ref[...] = ka.astype(BF16)
    s_ak_ref[...] = ka
    va = p[:, 1024:1536]
    va_ref[...] = va.astype(BF16)
    s_av_ref[...] = va
    cq = _rms_norm(p[:, 1536:1792], gcq_ref[...])
    bq = _dot(cq.astype(BF16), wuq_ref[...])
    bq_ref[:, 0:512] = bq[:, 0:512].astype(BF16)
    bq_ref[:, 512:768] = _rope(bq[:, 512:768], cos_b, sin_b, B_ROPE).astype(BF16)
    ckv = _rms_norm(p[:, 1792:1920], gckv_ref[...])
    s_ckv_ref[...] = ckv
    kvb_ref[...] = _dot(ckv.astype(BF16), wukv_ref[...]).astype(BF16)
    kr = _rope(p[:, 1920:2048], cos_b[:, 0:128], sin_b[:, 0:128], B_ROPE)[:, 0:B_ROPE]
    kr_ref[...] = kr.astype(BF16)
    s_kr_ref[...] = kr


def _proj_call(kernel_fn, name, x, mods, l, rt, weights, widths, state_widths):
    tok = lambda w: pl.BlockSpec((TM, w), lambda g: (_tile(g), 0))
    st = lambda w: pl.BlockSpec((TM, w), lambda g: (_state_blk(g), 0))
    return pl.pallas_call(
        kernel_fn,
        grid=(N_TILES,),
        in_specs=[tok(D),
                  pl.BlockSpec((1, 1, 6, D), lambda g: (l, _mod_row(_tile(g)), 0, 0)),
                  pl.BlockSpec((1, TM, rt.shape[-1]), lambda g: (_rope_blk(_tile(g)), 0, 0))]
                 + [_full(w.shape) for w in weights],
        out_specs=[tok(w) for w in widths] + [st(w) for w in state_widths],
        out_shape=[jax.ShapeDtypeStruct((T_ALL, w), BF16) for w in widths]
                  + [jax.ShapeDtypeStruct((T_CTX, w), F32) for w in state_widths],
        compiler_params=_params(("arbitrary",)),
        name=name,
    )(x, mods, rt, *weights)


def _proj_odd_kernel(x_ref, mod_ref, rt_ref, win_ref, gq_ref, gk_ref, bd_ref,
                     cq_ref, ck_ref, cv_ref, dq_ref, dk_ref, dv_ref,
                     s_ck_ref, s_cv_ref, s_dk_ref, s_dv_ref):
    x = x_ref[...]
    mod = mod_ref[0, 0]
    h = x * (1.0 + mod[1:2]) + mod[0:1]
    p = _dot(h.astype(BF16), win_ref[...])
    rt = rt_ref[0]
    cos_c, sin_c = rt[:, 0:512], rt[:, 512:1024]
    bd = bd_ref[...]
    cq = _chunk_rms_norm(p[:, 0:512], gq_ref[...], bd, D_C)
    cq_ref[...] = _rope(cq, cos_c, sin_c, D_C).astype(BF16)
    ck = _chunk_rms_norm(p[:, 512:640], gk_ref[...], bd[0:128, 0:128], D_C)
    ck = _rope(ck, cos_c[:, 0:128], sin_c[:, 0:128], D_C)
    ck_ref[...] = ck.astype(BF16)
    s_ck_ref[...] = ck
    cv = p[:, 640:768]
    cv_ref[...] = cv.astype(BF16)
    s_cv_ref[...] = cv
    dq_ref[...] = p[:, 768:1280].astype(BF16)
    dk = p[:, 1280:1792]
    dk_ref[...] = dk.astype(BF16)
    s_dk_ref[...] = dk
    dv = p[:, 1792:2304]
    dv_ref[...] = dv.astype(BF16)
    s_dv_ref[...] = dv


def _matmul_kernel(x_ref, w_ref, o_ref):
    o_ref[...] = _dot(x_ref[...].astype(BF16), w_ref[...]).astype(BF16)


def _matmul(x, w):
    m, k = x.shape
    n = w.shape[1]
    return pl.pallas_call(
        _matmul_kernel,
        grid=(m // TM,),
        in_specs=[pl.BlockSpec((TM, k), lambda i: (i, 0)), _full((k, n))],
        out_specs=pl.BlockSpec((TM, n), lambda i: (i, 0)),
        out_shape=jax.ShapeDtypeStruct((m, n), BF16),
        compiler_params=_params(("arbitrary",)),
        name="cache_kv_up",
    )(x, w)


def _exp_segments(segs):
    m = jnp.max(segs[0], axis=-1, keepdims=True)
    for s in segs[1:]:
        m = jnp.maximum(m, jnp.max(s, axis=-1, keepdims=True))
    es = [jnp.exp(s - m) for s in segs]
    den = jnp.sum(es[0], axis=-1, keepdims=True)
    for e in es[1:]:
        den = den + jnp.sum(e, axis=-1, keepdims=True)
    return es, den


def _pv(ps, vs):
    o = _dot(ps[0].astype(BF16), vs[0])
    for p, v in zip(ps[1:], vs[1:]):
        o = o + _dot(p.astype(BF16), v)
    return o


def _attend(segs, vs):
    es, den = _exp_segments(segs)
    return _pv(es, vs) / den


def _scaled(q, scale):
    return (q.astype(F32) * scale).astype(q.dtype)


def _attn_specs(row0, n_req, n_q, widths_q, widths_k, cache_shapes):
    n_own = n_q
    nqb = n_q // TQ
    q0 = row0 // TQ
    k0 = row0 // n_own
    qspec = lambda w: pl.BlockSpec((TQ, w), lambda b, j: (q0 + b * nqb + j, 0))
    kspec = lambda w: pl.BlockSpec((n_own, w), lambda b, j: (k0 + b, 0))
    cspec = lambda shp: pl.BlockSpec((1,) + tuple(shp[1:]), lambda b, j: (b,) + (0,) * (len(shp) - 1))
    specs = [qspec(w) for w in widths_q] + [kspec(w) for w in widths_k] + [cspec(s) for s in cache_shapes]
    return specs, qspec, (n_req, nqb)


def _attn_even_kernel(*refs, lam_init, has_cache):
    lq1_ref, lk1_ref, lq2_ref, lk2_ref, gsub_ref, qa_ref, bq_ref, ka_ref, va_ref, kvb_ref, kr_ref = refs[:11]
    if has_cache:
        cka_ref, cva_ref, ckvb_ref, ckr_ref = refs[11:15]
    oa_ref, ob_ref = refs[-2:]
    lam = (jnp.exp(jnp.sum(lq1_ref[...] * lk1_ref[...], axis=-1, keepdims=True))
           - jnp.exp(jnp.sum(lq2_ref[...] * lk2_ref[...], axis=-1, keepdims=True)) + lam_init)
    gsub = gsub_ref[...]
    for hd in range(H_A):
        vsl = slice(hd * 2 * D_A, (hd + 1) * 2 * D_A)
        vs = [va_ref[:, vsl]] + ([cva_ref[0, :, vsl].astype(BF16)] if has_cache else [])
        os = []
        for m in range(2):
            sl = slice(hd * 2 * D_A + m * D_A, hd * 2 * D_A + (m + 1) * D_A)
            q = _scaled(qa_ref[:, sl], D_A ** -0.5)
            segs = [_dot_nt(q, ka_ref[:, sl])]
            if has_cache:
                segs.append(_dot_nt(q, cka_ref[0, :, sl].astype(BF16)))
            os.append(_attend(segs, vs))
        o = os[0] - lam * os[1]
        oa_ref[:, vsl] = (_rms_norm(o, gsub) * (1.0 - lam_init)).astype(BF16)
    kr = kr_ref[...]
    if has_cache:
        ckr = ckr_ref[0].astype(BF16)
    sc = (B_NOPE + B_ROPE) ** -0.5
    for hd in range(H_B):
        nsl = slice(hd * B_NOPE, (hd + 1) * B_NOPE)
        vsl = slice(512 + hd * B_V, 512 + (hd + 1) * B_V)
        qn = bq_ref[:, nsl]
        qr = bq_ref[:, 512 + hd * B_ROPE:512 + (hd + 1) * B_ROPE]
        segs = [(_dot_nt(qn, kvb_ref[:, nsl]) + _dot_nt(qr, kr)) * sc]
        vs = [kvb_ref[:, vsl]]
        if has_cache:
            segs.append((_dot_nt(qn, ckvb_ref[0, :, nsl]) + _dot_nt(qr, ckr)) * sc)
            vs.append(ckvb_ref[0, :, vsl])
        ob_ref[:, hd * B_V:(hd + 1) * B_V] = _attend(segs, vs).astype(BF16)


def _attn_call(kernel_fn, name, params, arrays, widths_q, widths_k, latent, cache, prev_out, n_out):
    row0, n_req, n_q = (T_CTX, N_LAT_B, LAT_S) if latent else (0, N_CTX_B, CTX_S)
    cache = list(cache)
    cache_shapes = []
    for a in cache:
        cache_shapes.append(a.shape if a.ndim == 3 else (n_req,) + (a.shape[0] // n_req, a.shape[1]))
    cache = [a.reshape(shp) for a, shp in zip(cache, cache_shapes)]
    specs, qspec, grid = _attn_specs(row0, n_req, n_q, widths_q, widths_k, cache_shapes)
    in_specs = [_full(p.shape) for p in params] + specs
    operands = list(params) + list(arrays) + cache
    aliases = {}
    if prev_out is not None:
        for i, o in enumerate(prev_out):
            aliases[len(operands)] = i
            operands.append(o)
            in_specs.append(pl.BlockSpec(memory_space=pl.ANY))
    return pl.pallas_call(
        kernel_fn,
        grid=grid,
        in_specs=in_specs,
        out_specs=[qspec(512)] * n_out,
        out_shape=[jax.ShapeDtypeStruct((T_ALL, 512), BF16)] * n_out,
        input_output_aliases=aliases,
        compiler_params=_params(("arbitrary", "arbitrary")),
        name=name,
    )(*operands)


def _attn_even(lam_vecs, g_sub, lam_init, qa, bq, ka, va, kvb, kr, cache=(), prev_out=None):
    latent = prev_out is not None
    fn = functools.partial(_attn_even_kernel, lam_init=lam_init, has_cache=latent)
    return _attn_call(fn, "attn_even", list(lam_vecs) + [g_sub], [qa, bq, ka, va, kvb, kr],
                      (512, 768), (512, 512, 1024, B_ROPE), latent, cache, prev_out, 2)


def _gqa_kernel(*refs, has_cache):
    cq_ref, ck_ref, cv_ref = refs[:3]
    if has_cache:
        cck_ref, ccv_ref = refs[3:5]
    o_ref = refs[-1]
    for hd in range(H_C):
        g = hd // (H_C // G_C)
        gsl = slice(g * D_C, (g + 1) * D_C)
        q = _scaled(cq_ref[:, hd * D_C:(hd + 1) * D_C], D_C ** -0.5)
        segs = [_dot_nt(q, ck_ref[:, gsl])]
        vs = [cv_ref[:, gsl]]
        if has_cache:
            segs.append(_dot_nt(q, cck_ref[0, :, gsl].astype(BF16)))
            vs.append(ccv_ref[0, :, gsl].astype(BF16))
        o_ref[:, hd * D_C:(hd + 1) * D_C] = _attend(segs, vs).astype(BF16)


def _gqa(cq, ck, cv, cache=(), prev_out=None):
    latent = prev_out is not None
    fn = functools.partial(_gqa_kernel, has_cache=latent)
    return _attn_call(fn, "attn_gqa", [], [cq, ck, cv], (512,), (128, 128), latent, cache, prev_out, 1)[0]


def _mha_kernel(q_ref, k_ref, v_ref, o_ref):
    for hd in range(H_D):
        sl = slice(hd * D_D, (hd + 1) * D_D)
        segs = [_dot_nt(_scaled(q_ref[:, sl], D_D ** -0.5), k_ref[:, sl])]
        o_ref[:, sl] = _attend(segs, [v_ref[:, sl]]).astype(BF16)


def _mha(q, k, v):
    return _attn_call(_mha_kernel, "attn_mha", [], [q, k, v], (512,), (512, 512), False, (), None, 1)[0]


def _na_bias_kernel(rpb_ref, o_ref):
    hd = pl.program_id(0)
    qc = lax.broadcasted_iota(jnp.int32, (GRID_W, GRID_W), 0)
    kc = lax.broadcasted_iota(jnp.int32, (GRID_W, GRID_W), 1)
    dc = jnp.clip(kc - qc, -(NA_COLS - 1), NA_COLS - 1) + (NA_COLS - 1)
    c0 = jnp.clip(qc - NA_COLS // 2, 0, GRID_W - NA_COLS)
    n_dc = 2 * NA_COLS - 1
    n_dr = 2 * NA_ROWS - 1
    for dr in range(n_dr):
        acc = jnp.zeros((GRID_W, GRID_W), F32)
        for j in range(n_dc):
            acc = jnp.where(dc == j, rpb_ref[(hd * n_dr + dr) * n_dc + j], acc)
        o_ref[0, dr] = jnp.where(kc >= c0, jnp.where(kc < c0 + NA_COLS, acc, NEG_INF), NEG_INF)


def _na_bias(rpb):
    n_dr = 2 * NA_ROWS - 1
    return pl.pallas_call(
        _na_bias_kernel,
        grid=(H_D,),
        in_specs=[pl.BlockSpec(memory_space=pltpu.SMEM)],
        out_specs=pl.BlockSpec((1, n_dr, GRID_W, GRID_W), lambda h: (h, 0, 0, 0)),
        out_shape=jax.ShapeDtypeStruct((H_D, n_dr, GRID_W, GRID_W), F32),
        compiler_params=_params(("arbitrary",)),
        name="na_bias",
    )(rpb.reshape(-1))


def _na_window_start(r):
    return jnp.clip(r - NA_ROWS // 2, 0, GRID_ROWS - NA_ROWS)


def _na_kernel(q_ref, k_ref, v_ref, ck_ref, cv_ref, bias_ref, prev_ref, o_ref):
    r = pl.program_id(1)
    start = pl.multiple_of(_na_window_start(r) * GRID_W, GRID_W)
    n_loc = NA_ROWS * GRID_W
    scale = D_D ** -0.5
    heads = [slice(hd * D_D, (hd + 1) * D_D) for hd in range(H_D)]
    segs = []
    for hd, sl in enumerate(heads):
        q = _scaled(q_ref[:, sl], scale)
        segs.append([_dot_nt(q, k_ref[pl.ds(start, n_loc), sl]) + bias_ref[0, hd],
                     _dot_nt(q, ck_ref[0, :, sl].astype(BF16))])
    exps = [_exp_segments(s) for s in segs]
    for sl, (es, den) in zip(heads, exps):
        o = _pv(es, [v_ref[pl.ds(start, n_loc), sl], cv_ref[0, :, sl].astype(BF16)]) / den
        o_ref[:, sl] = o.astype(BF16)


def _neighbourhood(q, k, v, ck, cv, bias_win, prev_out):
    n_loc = NA_ROWS * GRID_W
    q0 = T_CTX // GRID_W
    k0 = T_CTX // LAT_S
    qspec = pl.BlockSpec((GRID_W, 512), lambda b, r: (q0 + b * GRID_ROWS + r, 0))
    kspec = pl.BlockSpec((LAT_S, 512), lambda b, r: (k0 + b, 0))
    cspec = pl.BlockSpec((1, PAST, 512), lambda b, r: (b, 0, 0))
    return pl.pallas_call(
        _na_kernel,
        grid=(N_LAT_B, GRID_ROWS),
        in_specs=[qspec, kspec, kspec, cspec, cspec,
                  pl.BlockSpec((1, H_D, GRID_W, n_loc),
                               lambda b, r: (_na_window_start(r) - r + NA_ROWS - 1, 0, 0, 0)),
                  pl.BlockSpec(memory_space=pl.ANY)],
        out_specs=qspec,
        out_shape=jax.ShapeDtypeStruct((T_ALL, 512), BF16),
        input_output_aliases={6: 0},
        compiler_params=_params(("arbitrary", "arbitrary")),
        name="attn_neighbourhood",
    )(q, k, v, ck, cv, bias_win, prev_out)


def _post_kernel(x_ref, mod_ref, o1_ref, o2_ref, wo1_ref, wo2_ref, g1_ref, b1_ref,
                 wq_ref, k1_ref, k2_ref, x1_ref, h2t_ref, s1_ref, s2_ref):
    mod = mod_ref[0, 0]
    y = _dot(o1_ref[...], wo1_ref[...]) + _dot(o2_ref[...], wo2_ref[...])
    x1 = _layer_norm(ALPHA * x_ref[...] + mod[2:3] * y, g1_ref[...], b1_ref[...])
    x1_ref[...] = x1
    h2f = x1 * (1.0 + mod[4:5]) + mod[3:4]
    h2 = h2f.astype(BF16)
    h2t_ref[...] = h2f.T.astype(BF16)
    q = _dot(h2, wq_ref[...]).astype(BF16)
    half = P_DKEY // 2
    k1 = k1_ref[...]
    k2 = k2_ref[...]
    for hd in range(P_HEADS):
        s1_ref[hd] = _dot_nt(k1, q[:, hd * P_DKEY:hd * P_DKEY + half])
        s2_ref[hd] = _dot_nt(k2, q[:, hd * P_DKEY + half:(hd + 1) * P_DKEY])


def _post(x, mods, l, o1, o2, wo1, wo2, g1, b1, wq, k1, k2):
    tok = lambda w: pl.BlockSpec((TM, w), lambda i: (i, 0))
    sspec = pl.BlockSpec((P_HEADS, P_KEYS, TM), lambda i: (0, 0, i))
    return pl.pallas_call(
        _post_kernel,
        grid=(N_TILES,),
        in_specs=[tok(D),
                  pl.BlockSpec((1, 1, 6, D), lambda i: (l, _mod_row(i), 0, 0)),
                  tok(512), tok(512), _full((512, D)), _full((512, D)), _full((1, D)), _full((1, D)),
                  _full((D, P_HEADS * P_DKEY)), _full((P_KEYS, P_DKEY // 2)), _full((P_KEYS, P_DKEY // 2))],
        out_specs=[tok(D), pl.BlockSpec((D, TM), lambda i: (0, i)), sspec, sspec],
        out_shape=[jax.ShapeDtypeStruct((T_ALL, D), F32),
                   jax.ShapeDtypeStruct((D, T_ALL), BF16),
                   jax.ShapeDtypeStruct((P_HEADS, P_KEYS, T_ALL), F32),
                   jax.ShapeDtypeStruct((P_HEADS, P_KEYS, T_ALL), F32)],
        compiler_params=_params(("arbitrary",)),
        name="post_mixer",
    )(x, mods, o1, o2, wo1, wo2, g1, b1, wq, k1, k2)


def _tables_kernel(u_ref, v_ref, ub_ref, vt_ref):
    ub_ref[0] = u_ref[0].astype(BF16)
    vt_ref[0] = v_ref[0].T.astype(BF16)


def _prep_tables(pk_u, pk_v):
    blk = pl.BlockSpec((1, PEER_EB, D), lambda l, e: (l, e, 0))
    return pl.pallas_call(
        _tables_kernel,
        grid=(DEPTH, P_EXPERTS // PEER_EB),
        in_specs=[blk, blk],
        out_specs=[blk, pl.BlockSpec((1, D, PEER_EB), lambda l, e: (l, 0, e))],
        out_shape=[jax.ShapeDtypeStruct((DEPTH, P_EXPERTS, D), BF16),
                   jax.ShapeDtypeStruct((DEPTH, D, P_EXPERTS), BF16)],
        compiler_params=_params(("arbitrary", "arbitrary")),
        name="expert_tables",
    )(pk_u, pk_v)


def _sorting_network(n):
    pairs = []
    p = 1
    while p < n:
        k = p
        while k >= 1:
            for j in range(k % p, n - k, 2 * k):
                for i in range(min(k, n - j - k)):
                    if (i + j) // (2 * p) == (i + j + k) // (2 * p):
                        pairs.append((i + j, i + j + k))
            k //= 2
        p *= 2
    return pairs


def _top_values(x, n):
    depth = x.shape[0] // 8
    groups = [x[8 * k:8 * k + 8, :] for k in range(depth)]
    size = 1
    while size < depth:
        size *= 2
    bottom = jnp.full_like(groups[0], NEG_INF)
    groups = groups + [bottom] * (size - depth)
    for i, j in _sorting_network(size):
        a, b = groups[i], groups[j]
        groups[i], groups[j] = jnp.maximum(a, b), jnp.minimum(a, b)
    groups = groups[:depth]
    vals = []
    for t in range(n):
        head = groups[0]
        m = jnp.max(head, axis=0, keepdims=True)
        vals.append(m)
        taken = head == m
        for k in range(min(n - 1 - t, depth)):
            groups[k] = jnp.where(taken, groups[k + 1] if k + 1 < depth else bottom, groups[k])
    return vals


PEER_PAIRS = [(a, b) for a in range(PEER_NSEL) for b in range(PEER_NSEL) if (a + 1) * (b + 1) <= PEER_NSEL]
PEER_NCAND = -(-len(PEER_PAIRS) // 8) * 8


def _peer_select(s1_ref, s2_ref, eth_scr, e1_scr, e2_scr, cand_scr):
    n_pad = PEER_NCAND - len(PEER_PAIRS)
    cand_scr[len(PEER_PAIRS):, :] = jnp.full((n_pad, cand_scr.shape[1]), NEG_INF, F32)

    def body(hd, carry):
        s1 = s1_ref[hd]
        s2 = s2_ref[hd]
        v1 = _top_values(s1, PEER_NSEL)
        v2 = _top_values(s2, PEER_NSEL)
        for k, (a, b) in enumerate(PEER_PAIRS):
            cand_scr[k:k + 1, :] = v1[a] + v2[b]
        cs = _top_values(cand_scr[...], PEER_NSEL)
        tau = 0.5 * (cs[P_TOPK - 1] + cs[P_TOPK])
        z = jnp.exp(cs[0] - cs[0])
        for j in range(1, P_TOPK):
            z = z + jnp.exp(cs[j] - cs[0])
        half_inv_z = 0.5 / z
        eth_scr[hd] = jnp.exp((tau - s1) - v2[0]) * half_inv_z
        e1_scr[hd] = jnp.exp(s1 - v1[0])
        e2_scr[hd] = jnp.exp(s2 - v2[0]) * half_inv_z
        return carry

    lax.fori_loop(0, P_HEADS, body, 0)


def _gelu2(a):
    return a * (1.0 + lax.erf(a * (2.0 ** -0.5)))


PEER_LW = 128
PEER_CH = 256


def _peer_kernel(h2t_ref, s1_ref, s2_ref, u_ref, vt_prev_ref, vt_last_ref, x1_ref, mod_ref, g2_ref, b2_ref,
                 o_ref, eth_scr, e1_scr, e2_scr, cand_scr, c_cur, c_prev, acc_scr):
    eb = pl.program_id(1)
    n_eb = pl.num_programs(1)
    rows_per_step = PEER_EB // P_KEYS

    @pl.when(eb == 0)
    def _():
        _peer_select(s1_ref, s2_ref, eth_scr, e1_scr, e2_scr, cand_scr)
        acc_scr[...] = jnp.zeros_like(acc_scr)
        c_prev[...] = jnp.zeros_like(c_prev)

    @pl.when(eb > 0)
    def _():
        c_prev[...] = c_cur[...]

    h2t = h2t_ref[...]
    i1_step = pl.multiple_of(eb * rows_per_step, rows_per_step)
    n_ch = PEER_EB // PEER_CH
    rb = D // n_ch
    nq = PEER_CH // P_KEYS
    n_lc = PEER_TT // PEER_LW

    def act_rows(j):
        return _dot(u_ref[j * P_KEYS:(j + 1) * P_KEYS, :], h2t)

    def value_rows(r):
        rows = slice(r * P_KEYS, (r + 1) * P_KEYS)
        acc_scr[rows, :] += _dot(vt_prev_ref[rows, :], c_prev[...])

    act_next = [act_rows(q) for q in range(nq)]
    for k in range(n_ch):
        act = act_next
        act_next = []
        pieces = [functools.partial(value_rows, k * (rb // P_KEYS) + r) for r in range(rb // P_KEYS)]
        if k + 1 < n_ch:
            for q in range(nq):
                pieces.insert(2 * q, lambda j=(k + 1) * nq + q: act_next.append(act_rows(j)))
        for lc in range(n_lc):
            for piece in pieces[lc * len(pieces) // n_lc:(lc + 1) * len(pieces) // n_lc]:
                piece()
            ls = slice(lc * PEER_LW, (lc + 1) * PEER_LW)
            ws = [jnp.zeros((P_KEYS, PEER_LW), F32) for _ in range(nq)]
            for hd in range(P_HEADS):
                e2 = e2_scr[hd, :, ls]
                eth_rows = eth_scr[hd, pl.ds(i1_step, rows_per_step), ls]
                e1_rows = e1_scr[hd, pl.ds(i1_step, rows_per_step), ls]
                for q in range(nq):
                    r = k * nq + q
                    ws[q] = ws[q] + jnp.where(e2 >= eth_rows[r:r + 1, :], e2, 0.0) * e1_rows[r:r + 1, :]
            for q in range(nq):
                j = k * nq + q
                c_cur[j * P_KEYS:(j + 1) * P_KEYS, ls] = (ws[q] * _gelu2(act[q][:, ls])).astype(BF16)

    @pl.when(eb == n_eb - 1)
    def _():
        mod = mod_ref[0, 0]
        peer = (acc_scr[...] + _dot(vt_last_ref[...], c_cur[...])).T
        o_ref[...] = _layer_norm(ALPHA * x1_ref[...] + mod[5:6] * peer, g2_ref[...], b2_ref[...])


def _peer(h2t, s1, s2, u_bf, vt_bf, x1, mods, l, g2, b2):
    tiles_ctx = T_CTX // PEER_TT
    tiles_per_lat = LAT_S // PEER_TT
    n_eb = P_EXPERTS // PEER_EB
    mod_row = lambda i: jnp.where(i < tiles_ctx, 0, 1 + (i - tiles_ctx) // tiles_per_lat)
    tok = lambda w: pl.BlockSpec((PEER_TT, w), lambda i, e: (i, 0))
    sspec = pl.BlockSpec((P_HEADS, P_KEYS, PEER_TT), lambda i, e: (0, 0, i))
    sel = pltpu.VMEM((P_HEADS, P_KEYS, PEER_TT), F32)
    return pl.pallas_call(
        _peer_kernel,
        grid=(T_ALL // PEER_TT, n_eb),
        in_specs=[pl.BlockSpec((D, PEER_TT), lambda i, e: (0, i)), sspec, sspec,
                  pl.BlockSpec((None, PEER_EB, D), lambda i, e: (l, e, 0)),
                  pl.BlockSpec((None, D, PEER_EB), lambda i, e: (l, 0, jnp.maximum(e - 1, 0))),
                  pl.BlockSpec((None, D, PEER_EB), lambda i, e: (l, 0, jnp.where(e == n_eb - 1, n_eb - 1, 0))),
                  tok(D),
                  pl.BlockSpec((1, 1, 6, D), lambda i, e: (l, mod_row(i), 0, 0)),
                  pl.BlockSpec((1, D), lambda i, e: (0, 0)),
                  pl.BlockSpec((1, D), lambda i, e: (0, 0))],
        out_specs=tok(D),
        out_shape=jax.ShapeDtypeStruct((T_ALL, D), F32),
        scratch_shapes=[sel, sel, sel,
                        pltpu.VMEM((PEER_NCAND, PEER_TT), F32),
                        pltpu.VMEM((PEER_EB, PEER_TT), BF16),
                        pltpu.VMEM((PEER_EB, PEER_TT), BF16),
                        pltpu.VMEM((D, PEER_TT), F32)],
        compiler_params=_params(("arbitrary", "arbitrary")),
        name="peer",
    )(h2t, s1, s2, u_bf, vt_bf, vt_bf, x1, mods, g2, b2)


def _grid_angles(rot_dim):
    t = jnp.arange(LAT_S)
    row = (t // GRID_W).astype(F32)
    col = (t % GRID_W).astype(F32)
    n_freq = rot_dim // 4
    inv = THETA ** (-jnp.arange(n_freq, dtype=F32) / n_freq)
    return jnp.concatenate([row[:, None] * inv, col[:, None] * inv], axis=-1)


def _rope_tables(rot_dim, width):
    ang = _grid_angles(rot_dim)
    cos = jnp.cos(ang)
    sin = jnp.sin(ang)
    cos_t = jnp.tile(jnp.concatenate([cos, cos], axis=-1), (1, width // rot_dim))
    sin_t = jnp.tile(jnp.concatenate([-sin, sin], axis=-1), (1, width // rot_dim))
    ident_c = jnp.ones((1, TM, width), F32)
    ident_s = jnp.zeros((1, TM, width), F32)
    cos_t = jnp.concatenate([ident_c, cos_t.reshape(TILES_PER_LAT, TM, width)], axis=0)
    sin_t = jnp.concatenate([ident_s, sin_t.reshape(TILES_PER_LAT, TM, width)], axis=0)
    return cos_t, sin_t


def kernel(x_prompt, x_sample, cache_a_k, cache_a_v, cache_b_ckv, cache_b_kr, cache_c_k, cache_c_v, cache_d_k, cache_d_v, c, c_ctx, w_mod, b_mod, ln1_g, ln1_b, ln2_g, ln2_b, ev_w_in, ev_lam_q1, ev_lam_k1, ev_lam_q2, ev_lam_k2, ev_g_sub, ev_g_cq, ev_w_uq, ev_g_ckv, ev_w_ukv, ev_w_out, od_w_in, od_g_qn, od_g_kn, od_rpb, od_w_out, pk_w_q, pk_k1, pk_k2, pk_u, pk_v):
    x = jnp.concatenate([x_prompt.reshape(T_CTX, D), x_sample.reshape(T_LAT, D)], axis=0)
    cvecs = jnp.concatenate([c_ctx[None, :], c, jnp.zeros((8 - 1 - N_LAT_B, D), F32)], axis=0)
    mods = _modulation(cvecs, w_mod, b_mod).reshape(DEPTH, 8, 6, D)
    u_bf, vt_bf = _prep_tables(pk_u, pk_v)

    cos_a, sin_a = _rope_tables(D_A, 512)
    cos_b, sin_b = _rope_tables(B_ROPE, 256)
    rt_even = jnp.concatenate([cos_a, sin_a, cos_b, sin_b], axis=-1)
    rt_odd = jnp.concatenate(_rope_tables(D_C, 512), axis=-1)
    ones_bd = jnp.kron(jnp.eye(512 // D_C, dtype=F32), jnp.ones((D_C, D_C), F32)).astype(BF16)
    lat3 = lambda a, w: a.reshape(N_LAT_B, PAST, w)

    states = {}
    for l in range(DEPTH):
        j = l // 2
        if l % 2 == 0:
            lam_init = 0.8 - 0.6 * math.exp(-0.3 * l)
            w_in = jnp.pad(ev_w_in[j], ((0, 0), (0, EVEN_IN_PAD - ev_w_in.shape[-1]))).astype(BF16)
            wuq = ev_w_uq[j].reshape(B_QRANK, H_B, B_NOPE + B_ROPE)
            wuq = jnp.concatenate([wuq[:, :, :B_NOPE].reshape(B_QRANK, -1),
                                   wuq[:, :, B_NOPE:].reshape(B_QRANK, -1)], axis=1).astype(BF16)
            wukv = ev_w_ukv[j].reshape(B_KVRANK, H_B, B_NOPE + B_V)
            wukv = jnp.concatenate([wukv[:, :, :B_NOPE].reshape(B_KVRANK, -1),
                                    wukv[:, :, B_NOPE:].reshape(B_KVRANK, -1)], axis=1).astype(BF16)
            (qa, bq, ka, va, kvb, kr,
             states["a_k"], states["a_v"], states["b_ckv"], states["b_kr"]) = _proj_call(
                _proj_even_kernel, "proj_even", x, mods, l, rt_even,
                [w_in, ev_g_cq[j][None, :], wuq, ev_g_ckv[j][None, :], wukv],
                (512, 768, 512, 512, 1024, B_ROPE), (512, 512, B_KVRANK, B_ROPE))
            lam_vecs = [v[j][None, :] for v in (ev_lam_q1, ev_lam_k1, ev_lam_q2, ev_lam_k2)]
            g_sub = ev_g_sub[j][None, :]
            proj = (qa, bq, ka, va, kvb, kr)
            outs = _attn_even(lam_vecs, g_sub, lam_init, *proj)
            cache_kvb = _matmul(cache_b_ckv[:, j].reshape(N_LAT_B * PAST, B_KVRANK), wukv)
            cache = (lat3(cache_a_k[:, j], 512), lat3(cache_a_v[:, j], 512), cache_kvb,
                     lat3(cache_b_kr[:, j], B_ROPE))
            o1, o2 = _attn_even(lam_vecs, g_sub, lam_init, *proj, cache=cache, prev_out=outs)
            w_out = ev_w_out[j]
        else:
            w_in = od_w_in[j].astype(BF16)
            g_qn = jnp.tile(od_g_qn[j], H_C)[None, :]
            g_kn = jnp.tile(od_g_kn[j], G_C)[None, :]
            (cq, ck, cv, dq, dk, dv,
             states["c_k"], states["c_v"], states["d_k"], states["d_v"]) = _proj_call(
                _proj_odd_kernel, "proj_odd", x, mods, l, rt_odd, [w_in, g_qn, g_kn, ones_bd],
                (512, 128, 128, 512, 512, 512), (128, 128, 512, 512))
            o1 = _gqa(cq, ck, cv)
            o1 = _gqa(cq, ck, cv, cache=(lat3(cache_c_k[:, j], 128), lat3(cache_c_v[:, j], 128)),
                      prev_out=[o1])
            o2 = _mha(dq, dk, dv)
            bias = _na_bias(od_rpb[j])
            offs = jnp.arange(NA_ROWS + 1)[:, None] + jnp.arange(NA_ROWS)[None, :]
            bias_win = jnp.transpose(bias[:, offs], (1, 0, 3, 2, 4)).reshape(
                NA_ROWS + 1, H_D, GRID_W, NA_ROWS * GRID_W)
            o2 = _neighbourhood(dq, dk, dv, lat3(cache_d_k[:, j], 512), lat3(cache_d_v[:, j], 512),
                                bias_win, o2)
            w_out = od_w_out[j]

        x1, h2t, s1, s2 = _post(x, mods, l, o1, o2, w_out[:512].astype(BF16), w_out[512:].astype(BF16),
                                ln1_g[l][None, :], ln1_b[l][None, :], pk_w_q[l].astype(BF16),
                                pk_k1[l].astype(BF16), pk_k2[l].astype(BF16))
        x = _peer(h2t, s1, s2, u_bf, vt_bf, x1, mods, l, ln2_g[l][None, :], ln2_b[l][None, :])

    y = x[:T_CTX].reshape(N_CTX_B, CTX_S, D)
    z = x[T_CTX:].reshape(N_LAT_B, LAT_S, D)
    n_even = (DEPTH + 1) // 2
    n_odd = DEPTH // 2
    return (y, z,
            states["a_k"].reshape(N_CTX_B, n_even, CTX_S, H_A, 2, D_A),
            states["a_v"].reshape(N_CTX_B, n_even, CTX_S, H_A, 2 * D_A),
            states["b_ckv"].reshape(N_CTX_B, n_even, CTX_S, B_KVRANK),
            states["b_kr"].reshape(N_CTX_B, n_even, CTX_S, B_ROPE),
            states["c_k"].reshape(N_CTX_B, n_odd, CTX_S, G_C, D_C),
            states["c_v"].reshape(N_CTX_B, n_odd, CTX_S, G_C, D_C),
            states["d_k"].reshape(N_CTX_B, n_odd, CTX_S, H_D, D_D),
            states["d_v"].reshape(N_CTX_B, n_odd, CTX_S, H_D, D_D))
```

```python
import functools
import math

import jax
import jax.numpy as jnp
from jax import lax
from jax.experimental import pallas as pl
from jax.experimental.pallas import tpu as pltpu

F32 = jnp.float32
BF16 = jnp.bfloat16

D = 1024
N_CTX_B = 32
CTX_S = 256
N_LAT_B = 4
LAT_S = 1024
PAST = 256
T_CTX = N_CTX_B * CTX_S
T_LAT = N_LAT_B * LAT_S
T_ALL = T_CTX + T_LAT
DEPTH = 2
GRID_W = 64
GRID_ROWS = LAT_S // GRID_W
THETA = 10000.0
EPS = 1e-6
ALPHA = (2 * DEPTH) ** 0.25

H_A, D_A = 4, 64
H_B, B_NOPE, B_ROPE, B_V, B_QRANK, B_KVRANK = 8, 64, 32, 64, 256, 128
H_C, G_C, D_C = 8, 2, 64
H_D, D_D = 8, 64
NA_ROWS, NA_COLS = 8, 16
P_HEADS, P_KEYS, P_DKEY, P_TOPK = 8, 128, 256, 16
P_EXPERTS = P_KEYS * P_KEYS

EVEN_IN_PAD = 2048
ODD_IN = 2304

TM = 512
N_TILES = T_ALL // TM
N_CTX_TILES = T_CTX // TM
TILES_PER_LAT = LAT_S // TM
N_LAT_TILES = T_LAT // TM
TQ = 256
PEER_TT = 512
PEER_EB = 1024
PEER_NSEL = P_TOPK + 1
VMEM_LIMIT = 56 * 1024 * 1024

NEG_INF = float("-inf")


def _dot(a, b):
    return jnp.dot(a, b, preferred_element_type=F32)


def _dot_nt(a, b):
    return lax.dot_general(a, b, (((1,), (1,)), ((), ())), preferred_element_type=F32)


def _layer_norm(x, g, b):
    mu = jnp.mean(x, axis=-1, keepdims=True)
    xc = x - mu
    var = jnp.mean(xc * xc, axis=-1, keepdims=True)
    return xc * lax.rsqrt(var + EPS) * g + b


def _rms_norm(x, g):
    return x * lax.rsqrt(jnp.mean(x * x, axis=-1, keepdims=True) + EPS) * g


def _chunk_rms_norm(x, g, ones_bd, chunk):
    x2 = x * x
    hi = x2.astype(BF16)
    lo = (x2 - hi.astype(F32)).astype(BF16)
    ms = (_dot(hi, ones_bd) + _dot(lo, ones_bd)) * (1.0 / chunk)
    return x * lax.rsqrt(ms + EPS) * g


def _rope(x, cos_t, sin_t, chunk):
    n = x.shape[-1]
    half = chunk // 2
    lane = lax.broadcasted_iota(jnp.int32, x.shape, x.ndim - 1)
    swapped = jnp.where((lane % chunk) < half,
                        pltpu.roll(x, n - half, x.ndim - 1),
                        pltpu.roll(x, half, x.ndim - 1))
    return x * cos_t + swapped * sin_t


def _params(sem):
    return pltpu.CompilerParams(dimension_semantics=sem, vmem_limit_bytes=VMEM_LIMIT)


def _full(shape):
    zeros = (0,) * len(shape)
    return pl.BlockSpec(shape, lambda *_: zeros)


def _mod_row(i):
    return jnp.where(i < N_CTX_TILES, 0, 1 + (i - N_CTX_TILES) // TILES_PER_LAT)


def _rope_blk(i):
    return jnp.where(i < N_CTX_TILES, 0, 1 + (i - N_CTX_TILES) % TILES_PER_LAT)


def _mod_kernel(c_ref, w_ref, b_ref, o_ref):
    c = c_ref[...]
    a = c * (1.0 / (1.0 + jnp.exp(-c)))
    o_ref[0] = _dot(a.astype(BF16), w_ref[0].astype(BF16)) + b_ref[0]


def _modulation(cvecs, w_mod, b_mod):
    nb = 6
    bn = 6 * D // nb
    return pl.pallas_call(
        _mod_kernel,
        grid=(DEPTH, nb),
        in_specs=[pl.BlockSpec((8, D), lambda l, j: (0, 0)),
                  pl.BlockSpec((1, D, bn), lambda l, j: (l, 0, j)),
                  pl.BlockSpec((1, 1, bn), lambda l, j: (l, 0, j))],
        out_specs=pl.BlockSpec((1, 8, bn), lambda l, j: (l, 0, j)),
        out_shape=jax.ShapeDtypeStruct((DEPTH, 8, 6 * D), F32),
        compiler_params=_params(("arbitrary", "arbitrary")),
        name="modulation",
    )(cvecs, w_mod, b_mod.reshape(DEPTH, 1, 6 * D))


def _tile(g):
    return (g + N_CTX_TILES) % N_TILES


def _state_blk(g):
    return jnp.maximum(g - N_LAT_TILES, 0)


def _proj_even_kernel(x_ref, mod_ref, rt_ref, win_ref, gcq_ref, wuq_ref, gckv_ref, wukv_ref,
                      qa_ref, bq_ref, ka_ref, va_ref, kvb_ref, kr_ref,
                      s_ak_ref, s_av_ref, s_ckv_ref, s_kr_ref):
    x = x_ref[...]
    mod = mod_ref[0, 0]
    h = x * (1.0 + mod[1:2]) + mod[0:1]
    p = _dot(h.astype(BF16), win_ref[...])
    rt = rt_ref[0]
    cos_a, sin_a = rt[:, 0:512], rt[:, 512:1024]
    cos_b, sin_b = rt[:, 1024:1280], rt[:, 1280:1536]
    qa_ref[...] = _rope(p[:, 0:512], cos_a, sin_a, D_A).astype(BF16)
    ka = _rope(p[:, 512:1024], cos_a, sin_a, D_A)
    ka_ref[...] = ka.astype(BF16)
    s_ak_ref[...] = ka
    va = p[:, 1024:1536]
    va_ref[...] = va.astype(BF16)
    s_av_ref[...] = va
    cq = _rms_norm(p[:, 1536:1792], gcq_ref[...])
    bq = _dot(cq.astype(BF16), wuq_ref[...])
    bq_ref[:, 0:512] = bq[:, 0:512].astype(BF16)
    bq_ref[:, 512:768] = _rope(bq[:, 512:768], cos_b, sin_b, B_ROPE).astype(BF16)
    ckv = _rms_norm(p[:, 1792:1920], gckv_ref[...])
    s_ckv_ref[...] = ckv
    kvb_ref[...] = _dot(ckv.astype(BF16), wukv_ref[...]).astype(BF16)
    kr = _rope(p[:, 1920:2048], cos_b[:, 0:128], sin_b[:, 0:128], B_ROPE)[:, 0:B_ROPE]
    kr_ref[...] = kr.astype(BF16)
    s_kr_ref[...] = kr


def _proj_call(kernel_fn, name, x, mods, l, rt, weights, widths, state_widths):
    tok = lambda w: pl.BlockSpec((TM, w), lambda g: (_tile(g), 0))
    st = lambda w: pl.BlockSpec((TM, w), lambda g: (_state_blk(g), 0))
    return pl.pallas_call(
        kernel_fn,
        grid=(N_TILES,),
        in_specs=[tok(D),
                  pl.BlockSpec((1, 1, 6, D), lambda g: (l, _mod_row(_tile(g)), 0, 0)),
                  pl.BlockSpec((1, TM, rt.shape[-1]), lambda g: (_rope_blk(_tile(g)), 0, 0))]
                 + [_full(w.shape) for w in weights],
        out_specs=[tok(w) for w in widths] + [st(w) for w in state_widths],
        out_shape=[jax.ShapeDtypeStruct((T_ALL, w), BF16) for w in widths]
                  + [jax.ShapeDtypeStruct((T_CTX, w), F32) for w in state_widths],
        compiler_params=_params(("arbitrary",)),
        name=name,
    )(x, mods, rt, *weights)


def _proj_odd_kernel(x_ref, mod_ref, rt_ref, win_ref, gq_ref, gk_ref, bd_ref,
                     cq_ref, ck_ref, cv_ref, dq_ref, dk_ref, dv_ref,
                     s_ck_ref, s_cv_ref, s_dk_ref, s_dv_ref):
    x = x_ref[...]
    mod = mod_ref[0, 0]
    h = x * (1.0 + mod[1:2]) + mod[0:1]
    p = _dot(h.astype(BF16), win_ref[...])
    rt = rt_ref[0]
    cos_c, sin_c = rt[:, 0:512], rt[:, 512:1024]
    bd = bd_ref[...]
    cq = _chunk_rms_norm(p[:, 0:512], gq_ref[...], bd, D_C)
    cq_ref[...] = _rope(cq, cos_c, sin_c, D_C).astype(BF16)
    ck = _chunk_rms_norm(p[:, 512:640], gk_ref[...], bd[0:128, 0:128], D_C)
    ck = _rope(ck, cos_c[:, 0:128], sin_c[:, 0:128], D_C)
    ck_ref[...] = ck.astype(BF16)
    s_ck_ref[...] = ck
    cv = p[:, 640:768]
    cv_ref[...] = cv.astype(BF16)
    s_cv_ref[...] = cv
    dq_ref[...] = p[:, 768:1280].astype(BF16)
    dk = p[:, 1280:1792]
    dk_ref[...] = dk.astype(BF16)
    s_dk_ref[...] = dk
    dv = p[:, 1792:2304]
    dv_ref[...] = dv.astype(BF16)
    s_dv_ref[...] = dv


def _matmul_kernel(x_ref, w_ref, o_ref):
    o_ref[...] = _dot(x_ref[...].astype(BF16), w_ref[...]).astype(BF16)


def _matmul(x, w):
    m, k = x.shape
    n = w.shape[1]
    return pl.pallas_call(
        _matmul_kernel,
        grid=(m // TM,),
        in_specs=[pl.BlockSpec((TM, k), lambda i: (i, 0)), _full((k, n))],
        out_specs=pl.BlockSpec((TM, n), lambda i: (i, 0)),
        out_shape=jax.ShapeDtypeStruct((m, n), BF16),
        compiler_params=_params(("arbitrary",)),
        name="cache_kv_up",
    )(x, w)


def _exp_segments(segs):
    m = jnp.max(segs[0], axis=-1, keepdims=True)
    for s in segs[1:]:
        m = jnp.maximum(m, jnp.max(s, axis=-1, keepdims=True))
    es = [jnp.exp(s - m) for s in segs]
    den = jnp.sum(es[0], axis=-1, keepdims=True)
    for e in es[1:]:
        den = den + jnp.sum(e, axis=-1, keepdims=True)
    return es, den


def _pv(ps, vs):
    o = _dot(ps[0].astype(BF16), vs[0])
    for p, v in zip(ps[1:], vs[1:]):
        o = o + _dot(p.astype(BF16), v)
    return o


def _attend(segs, vs):
    es, den = _exp_segments(segs)
    return _pv(es, vs) / den


def _scaled(q, scale):
    return (q.astype(F32) * scale).astype(q.dtype)


def _attn_specs(row0, n_req, n_q, widths_q, widths_k, cache_shapes):
    n_own = n_q
    nqb = n_q // TQ
    q0 = row0 // TQ
    k0 = row0 // n_own
    qspec = lambda w: pl.BlockSpec((TQ, w), lambda b, j: (q0 + b * nqb + j, 0))
    kspec = lambda w: pl.BlockSpec((n_own, w), lambda b, j: (k0 + b, 0))
    cspec = lambda shp: pl.BlockSpec((1,) + tuple(shp[1:]), lambda b, j: (b,) + (0,) * (len(shp) - 1))
    specs = [qspec(w) for w in widths_q] + [kspec(w) for w in widths_k] + [cspec(s) for s in cache_shapes]
    return specs, qspec, (n_req, nqb)


def _attn_even_kernel(*refs, lam_init, has_cache):
    lq1_ref, lk1_ref, lq2_ref, lk2_ref, gsub_ref, qa_ref, bq_ref, ka_ref, va_ref, kvb_ref, kr_ref = refs[:11]
    if has_cache:
        cka_ref, cva_ref, ckvb_ref, ckr_ref = refs[11:15]
    oa_ref, ob_ref = refs[-2:]
    lam = (jnp.exp(jnp.sum(lq1_ref[...] * lk1_ref[...], axis=-1, keepdims=True))
           - jnp.exp(jnp.sum(lq2_ref[...] * lk2_ref[...], axis=-1, keepdims=True)) + lam_init)
    gsub = gsub_ref[...]
    for hd in range(H_A):
        vsl = slice(hd * 2 * D_A, (hd + 1) * 2 * D_A)
        vs = [va_ref[:, vsl]] + ([cva_ref[0, :, vsl].astype(BF16)] if has_cache else [])
        os = []
        for m in range(2):
            sl = slice(hd * 2 * D_A + m * D_A, hd * 2 * D_A + (m + 1) * D_A)
            q = _scaled(qa_ref[:, sl], D_A ** -0.5)
            segs = [_dot_nt(q, ka_ref[:, sl])]
            if has_cache:
                segs.append(_dot_nt(q, cka_ref[0, :, sl].astype(BF16)))
            os.append(_attend(segs, vs))
        o = os[0] - lam * os[1]
        oa_ref[:, vsl] = (_rms_norm(o, gsub) * (1.0 - lam_init)).astype(BF16)
    kr = kr_ref[...]
    if has_cache:
        ckr = ckr_ref[0].astype(BF16)
    sc = (B_NOPE + B_ROPE) ** -0.5
    for hd in range(H_B):
        nsl = slice(hd * B_NOPE, (hd + 1) * B_NOPE)
        vsl = slice(512 + hd * B_V, 512 + (hd + 1) * B_V)
        qn = bq_ref[:, nsl]
        qr = bq_ref[:, 512 + hd * B_ROPE:512 + (hd + 1) * B_ROPE]
        segs = [(_dot_nt(qn, kvb_ref[:, nsl]) + _dot_nt(qr, kr)) * sc]
        vs = [kvb_ref[:, vsl]]
        if has_cache:
            segs.append((_dot_nt(qn, ckvb_ref[0, :, nsl]) + _dot_nt(qr, ckr)) * sc)
            vs.append(ckvb_ref[0, :, vsl])
        ob_ref[:, hd * B_V:(hd + 1) * B_V] = _attend(segs, vs).astype(BF16)


def _attn_call(kernel_fn, name, params, arrays, widths_q, widths_k, latent, cache, prev_out, n_out):
    row0, n_req, n_q = (T_CTX, N_LAT_B, LAT_S) if latent else (0, N_CTX_B, CTX_S)
    cache = list(cache)
    cache_shapes = []
    for a in cache:
        cache_shapes.append(a.shape if a.ndim == 3 else (n_req,) + (a.shape[0] // n_req, a.shape[1]))
    cache = [a.reshape(shp) for a, shp in zip(cache, cache_shapes)]
    specs, qspec, grid = _attn_specs(row0, n_req, n_q, widths_q, widths_k, cache_shapes)
    in_specs = [_full(p.shape) for p in params] + specs
    operands = list(params) + list(arrays) + cache
    aliases = {}
    if prev_out is not None:
        for i, o in enumerate(prev_out):
            aliases[len(operands)] = i
            operands.append(o)
            in_specs.append(pl.BlockSpec(memory_space=pl.ANY))
    return pl.pallas_call(
        kernel_fn,
        grid=grid,
        in_specs=in_specs,
        out_specs=[qspec(512)] * n_out,
        out_shape=[jax.ShapeDtypeStruct((T_ALL, 512), BF16)] * n_out,
        input_output_aliases=aliases,
        compiler_params=_params(("arbitrary", "arbitrary")),
        name=name,
    )(*operands)


def _attn_even(lam_vecs, g_sub, lam_init, qa, bq, ka, va, kvb, kr, cache=(), prev_out=None):
    latent = prev_out is not None
    fn = functools.partial(_attn_even_kernel, lam_init=lam_init, has_cache=latent)
    return _attn_call(fn, "attn_even", list(lam_vecs) + [g_sub], [qa, bq, ka, va, kvb, kr],
                      (512, 768), (512, 512, 1024, B_ROPE), latent, cache, prev_out, 2)


def _gqa_kernel(*refs, has_cache):
    cq_ref, ck_ref, cv_ref = refs[:3]
    if has_cache:
        cck_ref, ccv_ref = refs[3:5]
    o_ref = refs[-1]
    for hd in range(H_C):
        g = hd // (H_C // G_C)
        gsl = slice(g * D_C, (g + 1) * D_C)
        q = _scaled(cq_ref[:, hd * D_C:(hd + 1) * D_C], D_C ** -0.5)
        segs = [_dot_nt(q, ck_ref[:, gsl])]
        vs = [cv_ref[:, gsl]]
        if has_cache:
            segs.append(_dot_nt(q, cck_ref[0, :, gsl].astype(BF16)))
            vs.append(ccv_ref[0, :, gsl].astype(BF16))
        o_ref[:, hd * D_C:(hd + 1) * D_C] = _attend(segs, vs).astype(BF16)


def _gqa(cq, ck, cv, cache=(), prev_out=None):
    latent = prev_out is not None
    fn = functools.partial(_gqa_kernel, has_cache=latent)
    return _attn_call(fn, "attn_gqa", [], [cq, ck, cv], (512,), (128, 128), latent, cache, prev_out, 1)[0]


def _mha_kernel(q_ref, k_ref, v_ref, o_ref):
    for hd in range(H_D):
        sl = slice(hd * D_D, (hd + 1) * D_D)
        segs = [_dot_nt(_scaled(q_ref[:, sl], D_D ** -0.5), k_ref[:, sl])]
        o_ref[:, sl] = _attend(segs, [v_ref[:, sl]]).astype(BF16)


def _mha(q, k, v):
    return _attn_call(_mha_kernel, "attn_mha", [], [q, k, v], (512,), (512, 512), False, (), None, 1)[0]


def _na_bias_kernel(rpb_ref, o_ref):
    hd = pl.program_id(0)
    qc = lax.broadcasted_iota(jnp.int32, (GRID_W, GRID_W), 0)
    kc = lax.broadcasted_iota(jnp.int32, (GRID_W, GRID_W), 1)
    dc = jnp.clip(kc - qc, -(NA_COLS - 1), NA_COLS - 1) + (NA_COLS - 1)
    c0 = jnp.clip(qc - NA_COLS // 2, 0, GRID_W - NA_COLS)
    n_dc = 2 * NA_COLS - 1
    n_dr = 2 * NA_ROWS - 1
    for dr in range(n_dr):
        acc = jnp.zeros((GRID_W, GRID_W), F32)
        for j in range(n_dc):
            acc = jnp.where(dc == j, rpb_ref[(hd * n_dr + dr) * n_dc + j], acc)
        o_ref[0, dr] = jnp.where(kc >= c0, jnp.where(kc < c0 + NA_COLS, acc, NEG_INF), NEG_INF)


def _na_bias(rpb):
    n_dr = 2 * NA_ROWS - 1
    return pl.pallas_call(
        _na_bias_kernel,
        grid=(H_D,),
        in_specs=[pl.BlockSpec(memory_space=pltpu.SMEM)],
        out_specs=pl.BlockSpec((1, n_dr, GRID_W, GRID_W), lambda h: (h, 0, 0, 0)),
        out_shape=jax.ShapeDtypeStruct((H_D, n_dr, GRID_W, GRID_W), F32),
        compiler_params=_params(("arbitrary",)),
        name="na_bias",
    )(rpb.reshape(-1))


def _na_window_start(r):
    return jnp.clip(r - NA_ROWS // 2, 0, GRID_ROWS - NA_ROWS)


def _na_kernel(q_ref, k_ref, v_ref, ck_ref, cv_ref, bias_ref, prev_ref, o_ref):
    r = pl.program_id(1)
    start = pl.multiple_of(_na_window_start(r) * GRID_W, GRID_W)
    n_loc = NA_ROWS * GRID_W
    scale = D_D ** -0.5
    heads = [slice(hd * D_D, (hd + 1) * D_D) for hd in range(H_D)]
    segs = []
    for hd, sl in enumerate(heads):
        q = _scaled(q_ref[:, sl], scale)
        segs.append([_dot_nt(q, k_ref[pl.ds(start, n_loc), sl]) + bias_ref[0, hd],
                     _dot_nt(q, ck_ref[0, :, sl].astype(BF16))])
    exps = [_exp_segments(s) for s in segs]
    for sl, (es, den) in zip(heads, exps):
        o = _pv(es, [v_ref[pl.ds(start, n_loc), sl], cv_ref[0, :, sl].astype(BF16)]) / den
        o_ref[:, sl] = o.astype(BF16)


def _neighbourhood(q, k, v, ck, cv, bias_win, prev_out):
    n_loc = NA_ROWS * GRID_W
    q0 = T_CTX // GRID_W
    k0 = T_CTX // LAT_S
    qspec = pl.BlockSpec((GRID_W, 512), lambda b, r: (q0 + b * GRID_ROWS + r, 0))
    kspec = pl.BlockSpec((LAT_S, 512), lambda b, r: (k0 + b, 0))
    cspec = pl.BlockSpec((1, PAST, 512), lambda b, r: (b, 0, 0))
    return pl.pallas_call(
        _na_kernel,
        grid=(N_LAT_B, GRID_ROWS),
        in_specs=[qspec, kspec, kspec, cspec, cspec,
                  pl.BlockSpec((1, H_D, GRID_W, n_loc),
                               lambda b, r: (_na_window_start(r) - r + NA_ROWS - 1, 0, 0, 0)),
                  pl.BlockSpec(memory_space=pl.ANY)],
        out_specs=qspec,
        out_shape=jax.ShapeDtypeStruct((T_ALL, 512), BF16),
        input_output_aliases={6: 0},
        compiler_params=_params(("arbitrary", "arbitrary")),
        name="attn_neighbourhood",
    )(q, k, v, ck, cv, bias_win, prev_out)


def _post_kernel(x_ref, mod_ref, o1_ref, o2_ref, wo1_ref, wo2_ref, g1_ref, b1_ref,
                 wq_ref, k1_ref, k2_ref, x1_ref, h2t_ref, s1_ref, s2_ref):
    mod = mod_ref[0, 0]
    y = _dot(o1_ref[...], wo1_ref[...]) + _dot(o2_ref[...], wo2_ref[...])
    x1 = _layer_norm(ALPHA * x_ref[...] + mod[2:3] * y, g1_ref[...], b1_ref[...])
    x1_ref[...] = x1
    h2f = x1 * (1.0 + mod[4:5]) + mod[3:4]
    h2 = h2f.astype(BF16)
    h2t_ref[...] = h2f.T.astype(BF16)
    q = _dot(h2, wq_ref[...]).astype(BF16)
    half = P_DKEY // 2
    k1 = k1_ref[...]
    k2 = k2_ref[...]
    for hd in range(P_HEADS):
        s1_ref[hd] = _dot_nt(k1, q[:, hd * P_DKEY:hd * P_DKEY + half])
        s2_ref[hd] = _dot_nt(k2, q[:, hd * P_DKEY + half:(hd + 1) * P_DKEY])


def _post(x, mods, l, o1, o2, wo1, wo2, g1, b1, wq, k1, k2):
    tok = lambda w: pl.BlockSpec((TM, w), lambda i: (i, 0))
    sspec = pl.BlockSpec((P_HEADS, P_KEYS, TM), lambda i: (0, 0, i))
    return pl.pallas_call(
        _post_kernel,
        grid=(N_TILES,),
        in_specs=[tok(D),
                  pl.BlockSpec((1, 1, 6, D), lambda i: (l, _mod_row(i), 0, 0)),
                  tok(512), tok(512), _full((512, D)), _full((512, D)), _full((1, D)), _full((1, D)),
                  _full((D, P_HEADS * P_DKEY)), _full((P_KEYS, P_DKEY // 2)), _full((P_KEYS, P_DKEY // 2))],
        out_specs=[tok(D), pl.BlockSpec((D, TM), lambda i: (0, i)), sspec, sspec],
        out_shape=[jax.ShapeDtypeStruct((T_ALL, D), F32),
                   jax.ShapeDtypeStruct((D, T_ALL), BF16),
                   jax.ShapeDtypeStruct((P_HEADS, P_KEYS, T_ALL), F32),
                   jax.ShapeDtypeStruct((P_HEADS, P_KEYS, T_ALL), F32)],
        compiler_params=_params(("arbitrary",)),
        name="post_mixer",
    )(x, mods, o1, o2, wo1, wo2, g1, b1, wq, k1, k2)


def _tables_kernel(u_ref, v_ref, ub_ref, vt_ref):
    ub_ref[0] = u_ref[0].astype(BF16)
    vt_ref[0] = v_ref[0].T.astype(BF16)


def _prep_tables(pk_u, pk_v):
    blk = pl.BlockSpec((1, PEER_EB, D), lambda l, e: (l, e, 0))
    return pl.pallas_call(
        _tables_kernel,
        grid=(DEPTH, P_EXPERTS // PEER_EB),
        in_specs=[blk, blk],
        out_specs=[blk, pl.BlockSpec((1, D, PEER_EB), lambda l, e: (l, 0, e))],
        out_shape=[jax.ShapeDtypeStruct((DEPTH, P_EXPERTS, D), BF16),
                   jax.ShapeDtypeStruct((DEPTH, D, P_EXPERTS), BF16)],
        compiler_params=_params(("arbitrary", "arbitrary")),
        name="expert_tables",
    )(pk_u, pk_v)


def _sorting_network(n):
    pairs = []
    p = 1
    while p < n:
        k = p
        while k >= 1:
            for j in range(k % p, n - k, 2 * k):
                for i in range(min(k, n - j - k)):
                    if (i + j) // (2 * p) == (i + j + k) // (2 * p):
                        pairs.append((i + j, i + j + k))
            k //= 2
        p *= 2
    return pairs


def _top_values(x, n):
    depth = x.shape[0] // 8
    groups = [x[8 * k:8 * k + 8, :] for k in range(depth)]
    size = 1
    while size < depth:
        size *= 2
    bottom = jnp.full_like(groups[0], NEG_INF)
    groups = groups + [bottom] * (size - depth)
    for i, j in _sorting_network(size):
        a, b = groups[i], groups[j]
        groups[i], groups[j] = jnp.maximum(a, b), jnp.minimum(a, b)
    groups = groups[:depth]
    vals = []
    for t in range(n):
        head = groups[0]
        m = jnp.max(head, axis=0, keepdims=True)
        vals.append(m)
        taken = head == m
        for k in range(min(n - 1 - t, depth)):
            groups[k] = jnp.where(taken, groups[k + 1] if k + 1 < depth else bottom, groups[k])
    return vals


PEER_PAIRS = [(a, b) for a in range(PEER_NSEL) for b in range(PEER_NSEL) if (a + 1) * (b + 1) <= PEER_NSEL]
PEER_NCAND = -(-len(PEER_PAIRS) // 8) * 8


def _peer_select(s1_ref, s2_ref, eth_scr, e1_scr, e2_scr, cand_scr):
    n_pad = PEER_NCAND - len(PEER_PAIRS)
    cand_scr[len(PEER_PAIRS):, :] = jnp.full((n_pad, cand_scr.shape[1]), NEG_INF, F32)

    def body(hd, carry):
        s1 = s1_ref[hd]
        s2 = s2_ref[hd]
        v1 = _top_values(s1, PEER_NSEL)
        v2 = _top_values(s2, PEER_NSEL)
        for k, (a, b) in enumerate(PEER_PAIRS):
            cand_scr[k:k + 1, :] = v1[a] + v2[b]
        cs = _top_values(cand_scr[...], PEER_NSEL)
        tau = 0.5 * (cs[P_TOPK - 1] + cs[P_TOPK])
        z = jnp.exp(cs[0] - cs[0])
        for j in range(1, P_TOPK):
            z = z + jnp.exp(cs[j] - cs[0])
        half_inv_z = 0.5 / z
        eth_scr[hd] = jnp.exp((tau - s1) - v2[0]) * half_inv_z
        e1_scr[hd] = jnp.exp(s1 - v1[0])
        e2_scr[hd] = jnp.exp(s2 - v2[0]) * half_inv_z
        return carry

    lax.fori_loop(0, P_HEADS, body, 0)


def _gelu2(a):
    return a * (1.0 + lax.erf(a * (2.0 ** -0.5)))


PEER_LW = 256
PEER_CH = 512


def _peer_kernel(h2t_ref, s1_ref, s2_ref, u_ref, vt_prev_ref, vt_last_ref, x1_ref, mod_ref, g2_ref, b2_ref,
                 o_ref, eth_scr, e1_scr, e2_scr, cand_scr, c_cur, c_prev, acc_scr):
    eb = pl.program_id(1)
    n_eb = pl.num_programs(1)
    rows_per_step = PEER_EB // P_KEYS

    @pl.when(eb == 0)
    def _():
        _peer_select(s1_ref, s2_ref, eth_scr, e1_scr, e2_scr, cand_scr)
        acc_scr[...] = jnp.zeros_like(acc_scr)
        c_prev[...] = jnp.zeros_like(c_prev)

    @pl.when(eb > 0)
    def _():
        c_prev[...] = c_cur[...]

    h2t = h2t_ref[...]
    n_ch = PEER_EB // PEER_CH
    rb = D // n_ch
    act_next = _dot(u_ref[0:PEER_CH, :], h2t)
    for k in range(n_ch):
        act = act_next
        if k + 1 < n_ch:
            act_next = _dot(u_ref[(k + 1) * PEER_CH:(k + 2) * PEER_CH, :], h2t)
        acc_scr[k * rb:(k + 1) * rb, :] += _dot(vt_prev_ref[k * rb:(k + 1) * rb, :], c_prev[...])
        for jj in range(PEER_CH // P_KEYS):
            j = k * (PEER_CH // P_KEYS) + jj
            i1 = eb * rows_per_step + j
            for lc in range(PEER_TT // PEER_LW):
                ls = slice(lc * PEER_LW, (lc + 1) * PEER_LW)
                w = jnp.zeros((P_KEYS, PEER_LW), F32)
                for hd in range(P_HEADS):
                    eth = eth_scr[hd, pl.ds(i1, 1), ls]
                    e1 = e1_scr[hd, pl.ds(i1, 1), ls]
                    e2 = e2_scr[hd, :, ls]
                    w = w + jnp.where(e2 >= eth, e2, 0.0) * e1
                a = act[jj * P_KEYS:(jj + 1) * P_KEYS, ls]
                c_cur[j * P_KEYS:(j + 1) * P_KEYS, ls] = (w * _gelu2(a)).astype(BF16)

    @pl.when(eb == n_eb - 1)
    def _():
        mod = mod_ref[0, 0]
        peer = (acc_scr[...] + _dot(vt_last_ref[...], c_cur[...])).T
        o_ref[...] = _layer_norm(ALPHA * x1_ref[...] + mod[5:6] * peer, g2_ref[...], b2_ref[...])


def _peer(h2t, s1, s2, u_bf, vt_bf, x1, mods, l, g2, b2):
    tiles_ctx = T_CTX // PEER_TT
    tiles_per_lat = LAT_S // PEER_TT
    n_eb = P_EXPERTS // PEER_EB
    mod_row = lambda i: jnp.where(i < tiles_ctx, 0, 1 + (i - tiles_ctx) // tiles_per_lat)
    tok = lambda w: pl.BlockSpec((PEER_TT, w), lambda i, e: (i, 0))
    sspec = pl.BlockSpec((P_HEADS, P_KEYS, PEER_TT), lambda i, e: (0, 0, i))
    sel = pltpu.VMEM((P_HEADS, P_KEYS, PEER_TT), F32)
    return pl.pallas_call(
        _peer_kernel,
        grid=(T_ALL // PEER_TT, n_eb),
        in_specs=[pl.BlockSpec((D, PEER_TT), lambda i, e: (0, i)), sspec, sspec,
                  pl.BlockSpec((None, PEER_EB, D), lambda i, e: (l, e, 0)),
                  pl.BlockSpec((None, D, PEER_EB), lambda i, e: (l, 0, jnp.maximum(e - 1, 0))),
                  pl.BlockSpec((None, D, PEER_EB), lambda i, e: (l, 0, jnp.where(e == n_eb - 1, n_eb - 1, 0))),
                  tok(D),
                  pl.BlockSpec((1, 1, 6, D), lambda i, e: (l, mod_row(i), 0, 0)),
                  pl.BlockSpec((1, D), lambda i, e: (0, 0)),
                  pl.BlockSpec((1, D), lambda i, e: (0, 0))],
        out_specs=tok(D),
        out_shape=jax.ShapeDtypeStruct((T_ALL, D), F32),
        scratch_shapes=[sel, sel, sel,
                        pltpu.VMEM((PEER_NCAND, PEER_TT), F32),
                        pltpu.VMEM((PEER_EB, PEER_TT), BF16),
                        pltpu.VMEM((PEER_EB, PEER_TT), BF16),
                        pltpu.VMEM((D, PEER_TT), F32)],
        compiler_params=_params(("arbitrary", "arbitrary")),
        name="peer",
    )(h2t, s1, s2, u_bf, vt_bf, vt_bf, x1, mods, g2, b2)


def _grid_angles(rot_dim):
    t = jnp.arange(LAT_S)
    row = (t // GRID_W).astype(F32)
    col = (t % GRID_W).astype(F32)
    n_freq = rot_dim // 4
    inv = THETA ** (-jnp.arange(n_freq, dtype=F32) / n_freq)
    return jnp.concatenate([row[:, None] * inv, col[:, None] * inv], axis=-1)


def _rope_tables(rot_dim, width):
    ang = _grid_angles(rot_dim)
    cos = jnp.cos(ang)
    sin = jnp.sin(ang)
    cos_t = jnp.tile(jnp.concatenate([cos, cos], axis=-1), (1, width // rot_dim))
    sin_t = jnp.tile(jnp.concatenate([-sin, sin], axis=-1), (1, width // rot_dim))
    ident_c = jnp.ones((1, TM, width), F32)
    ident_s = jnp.zeros((1, TM, width), F32)
    cos_t = jnp.concatenate([ident_c, cos_t.reshape(TILES_PER_LAT, TM, width)], axis=0)
    sin_t = jnp.concatenate([ident_s, sin_t.reshape(TILES_PER_LAT, TM, width)], axis=0)
    return cos_t, sin_t


def kernel(x_prompt, x_sample, cache_a_k, cache_a_v, cache_b_ckv, cache_b_kr, cache_c_k, cache_c_v, cache_d_k, cache_d_v, c, c_ctx, w_mod, b_mod, ln1_g, ln1_b, ln2_g, ln2_b, ev_w_in, ev_lam_q1, ev_lam_k1, ev_lam_q2, ev_lam_k2, ev_g_sub, ev_g_cq, ev_w_uq, ev_g_ckv, ev_w_ukv, ev_w_out, od_w_in, od_g_qn, od_g_kn, od_rpb, od_w_out, pk_w_q, pk_k1, pk_k2, pk_u, pk_v):
    x = jnp.concatenate([x_prompt.reshape(T_CTX, D), x_sample.reshape(T_LAT, D)], axis=0)
    cvecs = jnp.concatenate([c_ctx[None, :], c, jnp.zeros((8 - 1 - N_LAT_B, D), F32)], axis=0)
    mods = _modulation(cvecs, w_mod, b_mod).reshape(DEPTH, 8, 6, D)
    u_bf, vt_bf = _prep_tables(pk_u, pk_v)

    cos_a, sin_a = _rope_tables(D_A, 512)
    cos_b, sin_b = _rope_tables(B_ROPE, 256)
    rt_even = jnp.concatenate([cos_a, sin_a, cos_b, sin_b], axis=-1)
    rt_odd = jnp.concatenate(_rope_tables(D_C, 512), axis=-1)
    ones_bd = jnp.kron(jnp.eye(512 // D_C, dtype=F32), jnp.ones((D_C, D_C), F32)).astype(BF16)
    lat3 = lambda a, w: a.reshape(N_LAT_B, PAST, w)

    states = {}
    for l in range(DEPTH):
        j = l // 2
        if l % 2 == 0:
            lam_init = 0.8 - 0.6 * math.exp(-0.3 * l)
            w_in = jnp.pad(ev_w_in[j], ((0, 0), (0, EVEN_IN_PAD - ev_w_in.shape[-1]))).astype(BF16)
            wuq = ev_w_uq[j].reshape(B_QRANK, H_B, B_NOPE + B_ROPE)
            wuq = jnp.concatenate([wuq[:, :, :B_NOPE].reshape(B_QRANK, -1),
                                   wuq[:, :, B_NOPE:].reshape(B_QRANK, -1)], axis=1).astype(BF16)
            wukv = ev_w_ukv[j].reshape(B_KVRANK, H_B, B_NOPE + B_V)
            wukv = jnp.concatenate([wukv[:, :, :B_NOPE].reshape(B_KVRANK, -1),
                                    wukv[:, :, B_NOPE:].reshape(B_KVRANK, -1)], axis=1).astype(BF16)
            (qa, bq, ka, va, kvb, kr,
             states["a_k"], states["a_v"], states["b_ckv"], states["b_kr"]) = _proj_call(
                _proj_even_kernel, "proj_even", x, mods, l, rt_even,
                [w_in, ev_g_cq[j][None, :], wuq, ev_g_ckv[j][None, :], wukv],
                (512, 768, 512, 512, 1024, B_ROPE), (512, 512, B_KVRANK, B_ROPE))
            lam_vecs = [v[j][None, :] for v in (ev_lam_q1, ev_lam_k1, ev_lam_q2, ev_lam_k2)]
            g_sub = ev_g_sub[j][None, :]
            proj = (qa, bq, ka, va, kvb, kr)
            outs = _attn_even(lam_vecs, g_sub, lam_init, *proj)
            cache_kvb = _matmul(cache_b_ckv[:, j].reshape(N_LAT_B * PAST, B_KVRANK), wukv)
            cache = (lat3(cache_a_k[:, j], 512), lat3(cache_a_v[:, j], 512), cache_kvb,
                     lat3(cache_b_kr[:, j], B_ROPE))
            o1, o2 = _attn_even(lam_vecs, g_sub, lam_init, *proj, cache=cache, prev_out=outs)
            w_out = ev_w_out[j]
        else:
            w_in = od_w_in[j].astype(BF16)
            g_qn = jnp.tile(od_g_qn[j], H_C)[None, :]
            g_kn = jnp.tile(od_g_kn[j], G_C)[None, :]
            (cq, ck, cv, dq, dk, dv,
             states["c_k"], states["c_v"], states["d_k"], states["d_v"]) = _proj_call(
                _proj_odd_kernel, "proj_odd", x, mods, l, rt_odd, [w_in, g_qn, g_kn, ones_bd],
                (512, 128, 128, 512, 512, 512), (128, 128, 512, 512))
            o1 = _gqa(cq, ck, cv)
            o1 = _gqa(cq, ck, cv, cache=(lat3(cache_c_k[:, j], 128), lat3(cache_c_v[:, j], 128)),
                      prev_out=[o1])
            o2 = _mha(dq, dk, dv)
            bias = _na_bias(od_rpb[j])
            offs = jnp.arange(NA_ROWS + 1)[:, None] + jnp.arange(NA_ROWS)[None, :]
            bias_win = jnp.transpose(bias[:, offs], (1, 0, 3, 2, 4)).reshape(
                NA_ROWS + 1, H_D, GRID_W, NA_ROWS * GRID_W)
            o2 = _neighbourhood(dq, dk, dv, lat3(cache_d_k[:, j], 512), lat3(cache_d_v[:, j], 512),
                                bias_win, o2)
            w_out = od_w_out[j]

        x1, h2t, s1, s2 = _post(x, mods, l, o1, o2, w_out[:512].astype(BF16), w_out[512:].astype(BF16),
                                ln1_g[l][None, :], ln1_b[l][None, :], pk_w_q[l].astype(BF16),
                                pk_k1[l].astype(BF16), pk_k2[l].astype(BF16))
        x = _peer(h2t, s1, s2, u_bf, vt_bf, x1, mods, l, ln2_g[l][None, :], ln2_b[l][None, :])

    y = x[:T_CTX].reshape(N_CTX_B, CTX_S, D)
    z = x[T_CTX:].reshape(N_LAT_B, LAT_S, D)
    n_even = (DEPTH + 1) // 2
    n_odd = DEPTH // 2
    return (y, z,
            states["a_k"].reshape(N_CTX_B, n_even, CTX_S, H_A, 2, D_A),
            states["a_v"].reshape(N_CTX_B, n_even, CTX_S, H_A, 2 * D_A),
            states["b_ckv"].reshape(N_CTX_B, n_even, CTX_S, B_KVRANK),
            states["b_kr"].reshape(N_CTX_B, n_even, CTX_S, B_ROPE),
            states["c_k"].reshape(N_CTX_B, n_odd, CTX_S, G_C, D_C),
            states["c_v"].reshape(N_CTX_B, n_odd, CTX_S, G_C, D_C),
            states["d_k"].reshape(N_CTX_B, n_odd, CTX_S, H_D, D_D),
            states["d_v"].reshape(N_CTX_B, n_odd, CTX_S, H_D, D_D))
```

```python
import functools
import math

import jax
import jax.numpy as jnp
from jax import lax
from jax.experimental import pallas as pl
from jax.experimental.pallas import tpu as pltpu

F32 = jnp.float32
BF16 = jnp.bfloat16

D = 1024
N_CTX_B = 32
CTX_S = 256
N_LAT_B = 4
LAT_S = 1024
PAST = 256
T_CTX = N_CTX_B * CTX_S
T_LAT = N_LAT_B * LAT_S
T_ALL = T_CTX + T_LAT
DEPTH = 2
GRID_W = 64
GRID_ROWS = LAT_S // GRID_W
THETA = 10000.0
EPS = 1e-6
ALPHA = (2 * DEPTH) ** 0.25

H_A, D_A = 4, 64
H_B, B_NOPE, B_ROPE, B_V, B_QRANK, B_KVRANK = 8, 64, 32, 64, 256, 128
H_C, G_C, D_C = 8, 2, 64
H_D, D_D = 8, 64
NA_ROWS, NA_COLS = 8, 16
P_HEADS, P_KEYS, P_DKEY, P_TOPK = 8, 128, 256, 16
P_EXPERTS = P_KEYS * P_KEYS

EVEN_IN_PAD = 2048
ODD_IN = 2304

TM = 512
N_TILES = T_ALL // TM
N_CTX_TILES = T_CTX // TM
TILES_PER_LAT = LAT_S // TM
N_LAT_TILES = T_LAT // TM
TQ = 256
PEER_TT = 512
PEER_EB = 1024
PEER_NSEL = P_TOPK + 1
VMEM_LIMIT = 56 * 1024 * 1024

NEG_INF = float("-inf")


def _dot(a, b):
    return jnp.dot(a, b, preferred_element_type=F32)


def _dot_nt(a, b):
    return lax.dot_general(a, b, (((1,), (1,)), ((), ())), preferred_element_type=F32)


def _layer_norm(x, g, b):
    mu = jnp.mean(x, axis=-1, keepdims=True)
    xc = x - mu
    var = jnp.mean(xc * xc, axis=-1, keepdims=True)
    return xc * lax.rsqrt(var + EPS) * g + b


def _rms_norm(x, g):
    return x * lax.rsqrt(jnp.mean(x * x, axis=-1, keepdims=True) + EPS) * g


def _chunk_rms_norm(x, g, ones_bd, chunk):
    x2 = x * x
    hi = x2.astype(BF16)
    lo = (x2 - hi.astype(F32)).astype(BF16)
    ms = (_dot(hi, ones_bd) + _dot(lo, ones_bd)) * (1.0 / chunk)
    return x * lax.rsqrt(ms + EPS) * g


def _rope(x, cos_t, sin_t, chunk):
    n = x.shape[-1]
    half = chunk // 2
    lane = lax.broadcasted_iota(jnp.int32, x.shape, x.ndim - 1)
    swapped = jnp.where((lane % chunk) < half,
                        pltpu.roll(x, n - half, x.ndim - 1),
                        pltpu.roll(x, half, x.ndim - 1))
    return x * cos_t + swapped * sin_t


def _params(sem):
    return pltpu.CompilerParams(dimension_semantics=sem, vmem_limit_bytes=VMEM_LIMIT)


def _full(shape):
    zeros = (0,) * len(shape)
    return pl.BlockSpec(shape, lambda *_: zeros)


def _mod_row(i):
    return jnp.where(i < N_CTX_TILES, 0, 1 + (i - N_CTX_TILES) // TILES_PER_LAT)


def _rope_blk(i):
    return jnp.where(i < N_CTX_TILES, 0, 1 + (i - N_CTX_TILES) % TILES_PER_LAT)


def _mod_kernel(c_ref, w_ref, b_ref, o_ref):
    c = c_ref[...]
    a = c * (1.0 / (1.0 + jnp.exp(-c)))
    o_ref[0] = _dot(a.astype(BF16), w_ref[0].astype(BF16)) + b_ref[0]


def _modulation(cvecs, w_mod, b_mod):
    nb = 6
    bn = 6 * D // nb
    return pl.pallas_call(
        _mod_kernel,
        grid=(DEPTH, nb),
        in_specs=[pl.BlockSpec((8, D), lambda l, j: (0, 0)),
                  pl.BlockSpec((1, D, bn), lambda l, j: (l, 0, j)),
                  pl.BlockSpec((1, 1, bn), lambda l, j: (l, 0, j))],
        out_specs=pl.BlockSpec((1, 8, bn), lambda l, j: (l, 0, j)),
        out_shape=jax.ShapeDtypeStruct((DEPTH, 8, 6 * D), F32),
        compiler_params=_params(("arbitrary", "arbitrary")),
        name="modulation",
    )(cvecs, w_mod, b_mod.reshape(DEPTH, 1, 6 * D))


def _tile(g):
    return (g + N_CTX_TILES) % N_TILES


def _state_blk(g):
    return jnp.maximum(g - N_LAT_TILES, 0)


def _proj_even_kernel(x_ref, mod_ref, rt_ref, win_ref, gcq_ref, wuq_ref, gckv_ref, wukv_ref,
                      qa_ref, bq_ref, ka_ref, va_ref, kvb_ref, kr_ref,
                      s_ak_ref, s_av_ref, s_ckv_ref, s_kr_ref):
    x = x_ref[...]
    mod = mod_ref[0, 0]
    h = x * (1.0 + mod[1:2]) + mod[0:1]
    p = _dot(h.astype(BF16), win_ref[...])
    rt = rt_ref[0]
    cos_a, sin_a = rt[:, 0:512], rt[:, 512:1024]
    cos_b, sin_b = rt[:, 1024:1280], rt[:, 1280:1536]
    qa_ref[...] = _rope(p[:, 0:512], cos_a, sin_a, D_A).astype(BF16)
    ka = _rope(p[:, 512:1024], cos_a, sin_a, D_A)
    ka_ref[...] = ka.astype(BF16)
    s_ak_ref[...] = ka
    va = p[:, 1024:1536]
    va_ref[...] = va.astype(BF16)
    s_av_ref[...] = va
    cq = _rms_norm(p[:, 1536:1792], gcq_ref[...])
    bq = _dot(cq.astype(BF16), wuq_ref[...])
    bq_ref[:, 0:512] = bq[:, 0:512].astype(BF16)
    bq_ref[:, 512:768] = _rope(bq[:, 512:768], cos_b, sin_b, B_ROPE).astype(BF16)
    ckv = _rms_norm(p[:, 1792:1920], gckv_ref[...])
    s_ckv_ref[...] = ckv
    kvb_ref[...] = _dot(ckv.astype(BF16), wukv_ref[...]).astype(BF16)
    kr = _rope(p[:, 1920:2048], cos_b[:, 0:128], sin_b[:, 0:128], B_ROPE)[:, 0:B_ROPE]
    kr_ref[...] = kr.astype(BF16)
    s_kr_ref[...] = kr


def _proj_call(kernel_fn, name, x, mods, l, rt, weights, widths, state_widths):
    tok = lambda w: pl.BlockSpec((TM, w), lambda g: (_tile(g), 0))
    st = lambda w: pl.BlockSpec((TM, w), lambda g: (_state_blk(g), 0))
    return pl.pallas_call(
        kernel_fn,
        grid=(N_TILES,),
        in_specs=[tok(D),
                  pl.BlockSpec((1, 1, 6, D), lambda g: (l, _mod_row(_tile(g)), 0, 0)),
                  pl.BlockSpec((1, TM, rt.shape[-1]), lambda g: (_rope_blk(_tile(g)), 0, 0))]
                 + [_full(w.shape) for w in weights],
        out_specs=[tok(w) for w in widths] + [st(w) for w in state_widths],
        out_shape=[jax.ShapeDtypeStruct((T_ALL, w), BF16) for w in widths]
                  + [jax.ShapeDtypeStruct((T_CTX, w), F32) for w in state_widths],
        compiler_params=_params(("arbitrary",)),
        name=name,
    )(x, mods, rt, *weights)


def _proj_odd_kernel(x_ref, mod_ref, rt_ref, win_ref, gq_ref, gk_ref, bd_ref,
                     cq_ref, ck_ref, cv_ref, dq_ref, dk_ref, dv_ref,
                     s_ck_ref, s_cv_ref, s_dk_ref, s_dv_ref):
    x = x_ref[...]
    mod = mod_ref[0, 0]
    h = x * (1.0 + mod[1:2]) + mod[0:1]
    p = _dot(h.astype(BF16), win_ref[...])
    rt = rt_ref[0]
    cos_c, sin_c = rt[:, 0:512], rt[:, 512:1024]
    bd = bd_ref[...]
    cq = _chunk_rms_norm(p[:, 0:512], gq_ref[...], bd, D_C)
    cq_ref[...] = _rope(cq, cos_c, sin_c, D_C).astype(BF16)
    ck = _chunk_rms_norm(p[:, 512:640], gk_ref[...], bd[0:128, 0:128], D_C)
    ck = _rope(ck, cos_c[:, 0:128], sin_c[:, 0:128], D_C)
    ck_ref[...] = ck.astype(BF16)
    s_ck_ref[...] = ck
    cv = p[:, 640:768]
    cv_ref[...] = cv.astype(BF16)
    s_cv_ref[...] = cv
    dq_ref[...] = p[:, 768:1280].astype(BF16)
    dk = p[:, 1280:1792]
    dk_ref[...] = dk.astype(BF16)
    s_dk_ref[...] = dk
    dv = p[:, 1792:2304]
    dv_ref[...] = dv.astype(BF16)
    s_dv_ref[...] = dv


def _matmul_kernel(x_ref, w_ref, o_ref):
    o_ref[...] = _dot(x_ref[...].astype(BF16), w_ref[...]).astype(BF16)


def _matmul(x, w):
    m, k = x.shape
    n = w.shape[1]
    return pl.pallas_call(
        _matmul_kernel,
        grid=(m // TM,),
        in_specs=[pl.BlockSpec((TM, k), lambda i: (i, 0)), _full((k, n))],
        out_specs=pl.BlockSpec((TM, n), lambda i: (i, 0)),
        out_shape=jax.ShapeDtypeStruct((m, n), BF16),
        compiler_params=_params(("arbitrary",)),
        name="cache_kv_up",
    )(x, w)


def _exp_segments(segs):
    m = jnp.max(segs[0], axis=-1, keepdims=True)
    for s in segs[1:]:
        m = jnp.maximum(m, jnp.max(s, axis=-1, keepdims=True))
    es = [jnp.exp(s - m) for s in segs]
    den = jnp.sum(es[0], axis=-1, keepdims=True)
    for e in es[1:]:
        den = den + jnp.sum(e, axis=-1, keepdims=True)
    return es, den


def _pv(ps, vs):
    o = _dot(ps[0].astype(BF16), vs[0])
    for p, v in zip(ps[1:], vs[1:]):
        o = o + _dot(p.astype(BF16), v)
    return o


def _attend_group(seg_lists, vs_lists):
    exps = [_exp_segments(segs) for segs in seg_lists]
    return [_pv(es, vs) / den for (es, den), vs in zip(exps, vs_lists)]


def _scaled(q, scale):
    return (q.astype(F32) * scale).astype(q.dtype)


def _attn_specs(row0, n_req, n_q, widths_q, widths_k, cache_shapes):
    n_own = n_q
    nqb = n_q // TQ
    q0 = row0 // TQ
    k0 = row0 // n_own
    qspec = lambda w: pl.BlockSpec((TQ, w), lambda b, j: (q0 + b * nqb + j, 0))
    kspec = lambda w: pl.BlockSpec((n_own, w), lambda b, j: (k0 + b, 0))
    cspec = lambda shp: pl.BlockSpec((1,) + tuple(shp[1:]), lambda b, j: (b,) + (0,) * (len(shp) - 1))
    specs = [qspec(w) for w in widths_q] + [kspec(w) for w in widths_k] + [cspec(s) for s in cache_shapes]
    return specs, qspec, (n_req, nqb)


def _attn_even_kernel(*refs, lam_init, has_cache):
    lq1_ref, lk1_ref, lq2_ref, lk2_ref, gsub_ref, qa_ref, bq_ref, ka_ref, va_ref, kvb_ref, kr_ref = refs[:11]
    if has_cache:
        cka_ref, cva_ref, ckvb_ref, ckr_ref = refs[11:15]
    oa_ref, ob_ref = refs[-2:]
    lam = (jnp.exp(jnp.sum(lq1_ref[...] * lk1_ref[...], axis=-1, keepdims=True))
           - jnp.exp(jnp.sum(lq2_ref[...] * lk2_ref[...], axis=-1, keepdims=True)) + lam_init)
    gsub = gsub_ref[...]
    grp_a, grp_b = (1, 2) if has_cache else (H_A, H_B)
    for h0 in range(0, H_A, grp_a):
        seg_lists, vs_lists = [], []
        for hd in range(h0, h0 + grp_a):
            vsl = slice(hd * 2 * D_A, (hd + 1) * 2 * D_A)
            vs = [va_ref[:, vsl]] + ([cva_ref[0, :, vsl].astype(BF16)] if has_cache else [])
            for m in range(2):
                sl = slice(hd * 2 * D_A + m * D_A, hd * 2 * D_A + (m + 1) * D_A)
                q = _scaled(qa_ref[:, sl], D_A ** -0.5)
                segs = [_dot_nt(q, ka_ref[:, sl])]
                if has_cache:
                    segs.append(_dot_nt(q, cka_ref[0, :, sl].astype(BF16)))
                seg_lists.append(segs)
                vs_lists.append(vs)
        os = _attend_group(seg_lists, vs_lists)
        for i, hd in enumerate(range(h0, h0 + grp_a)):
            vsl = slice(hd * 2 * D_A, (hd + 1) * 2 * D_A)
            o = os[2 * i] - lam * os[2 * i + 1]
            oa_ref[:, vsl] = (_rms_norm(o, gsub) * (1.0 - lam_init)).astype(BF16)
    kr = kr_ref[...]
    if has_cache:
        ckr = ckr_ref[0].astype(BF16)
    sc = (B_NOPE + B_ROPE) ** -0.5
    for h0 in range(0, H_B, grp_b):
        seg_lists, vs_lists = [], []
        for hd in range(h0, h0 + grp_b):
            nsl = slice(hd * B_NOPE, (hd + 1) * B_NOPE)
            vsl = slice(512 + hd * B_V, 512 + (hd + 1) * B_V)
            qn = bq_ref[:, nsl]
            qr = bq_ref[:, 512 + hd * B_ROPE:512 + (hd + 1) * B_ROPE]
            segs = [(_dot_nt(qn, kvb_ref[:, nsl]) + _dot_nt(qr, kr)) * sc]
            vs = [kvb_ref[:, vsl]]
            if has_cache:
                segs.append((_dot_nt(qn, ckvb_ref[0, :, nsl]) + _dot_nt(qr, ckr)) * sc)
                vs.append(ckvb_ref[0, :, vsl])
            seg_lists.append(segs)
            vs_lists.append(vs)
        for hd, o in zip(range(h0, h0 + grp_b), _attend_group(seg_lists, vs_lists)):
            ob_ref[:, hd * B_V:(hd + 1) * B_V] = o.astype(BF16)


def _attn_call(kernel_fn, name, params, arrays, widths_q, widths_k, latent, cache, prev_out, n_out):
    row0, n_req, n_q = (T_CTX, N_LAT_B, LAT_S) if latent else (0, N_CTX_B, CTX_S)
    cache = list(cache)
    cache_shapes = []
    for a in cache:
        cache_shapes.append(a.shape if a.ndim == 3 else (n_req,) + (a.shape[0] // n_req, a.shape[1]))
    cache = [a.reshape(shp) for a, shp in zip(cache, cache_shapes)]
    specs, qspec, grid = _attn_specs(row0, n_req, n_q, widths_q, widths_k, cache_shapes)
    in_specs = [_full(p.shape) for p in params] + specs
    operands = list(params) + list(arrays) + cache
    aliases = {}
    if prev_out is not None:
        for i, o in enumerate(prev_out):
            aliases[len(operands)] = i
            operands.append(o)
            in_specs.append(pl.BlockSpec(memory_space=pl.ANY))
    return pl.pallas_call(
        kernel_fn,
        grid=grid,
        in_specs=in_specs,
        out_specs=[qspec(512)] * n_out,
        out_shape=[jax.ShapeDtypeStruct((T_ALL, 512), BF16)] * n_out,
        input_output_aliases=aliases,
        compiler_params=_params(("arbitrary", "arbitrary")),
        name=name,
    )(*operands)


def _attn_even(lam_vecs, g_sub, lam_init, qa, bq, ka, va, kvb, kr, cache=(), prev_out=None):
    latent = prev_out is not None
    fn = functools.partial(_attn_even_kernel, lam_init=lam_init, has_cache=latent)
    return _attn_call(fn, "attn_even", list(lam_vecs) + [g_sub], [qa, bq, ka, va, kvb, kr],
                      (512, 768), (512, 512, 1024, B_ROPE), latent, cache, prev_out, 2)


def _gqa_kernel(*refs, has_cache):
    cq_ref, ck_ref, cv_ref = refs[:3]
    if has_cache:
        cck_ref, ccv_ref = refs[3:5]
    o_ref = refs[-1]
    grp = 2 if has_cache else H_C
    for h0 in range(0, H_C, grp):
        seg_lists, vs_lists = [], []
        for hd in range(h0, h0 + grp):
            g = hd // (H_C // G_C)
            gsl = slice(g * D_C, (g + 1) * D_C)
            q = _scaled(cq_ref[:, hd * D_C:(hd + 1) * D_C], D_C ** -0.5)
            segs = [_dot_nt(q, ck_ref[:, gsl])]
            vs = [cv_ref[:, gsl]]
            if has_cache:
                segs.append(_dot_nt(q, cck_ref[0, :, gsl].astype(BF16)))
                vs.append(ccv_ref[0, :, gsl].astype(BF16))
            seg_lists.append(segs)
            vs_lists.append(vs)
        for hd, o in zip(range(h0, h0 + grp), _attend_group(seg_lists, vs_lists)):
            o_ref[:, hd * D_C:(hd + 1) * D_C] = o.astype(BF16)


def _gqa(cq, ck, cv, cache=(), prev_out=None):
    latent = prev_out is not None
    fn = functools.partial(_gqa_kernel, has_cache=latent)
    return _attn_call(fn, "attn_gqa", [], [cq, ck, cv], (512,), (128, 128), latent, cache, prev_out, 1)[0]


def _mha_kernel(q_ref, k_ref, v_ref, o_ref):
    heads = [slice(hd * D_D, (hd + 1) * D_D) for hd in range(H_D)]
    seg_lists = [[_dot_nt(_scaled(q_ref[:, sl], D_D ** -0.5), k_ref[:, sl])] for sl in heads]
    for sl, o in zip(heads, _attend_group(seg_lists, [[v_ref[:, sl]] for sl in heads])):
        o_ref[:, sl] = o.astype(BF16)


def _mha(q, k, v):
    return _attn_call(_mha_kernel, "attn_mha", [], [q, k, v], (512,), (512, 512), False, (), None, 1)[0]


def _na_bias_kernel(rpb_ref, o_ref):
    hd = pl.program_id(0)
    qc = lax.broadcasted_iota(jnp.int32, (GRID_W, GRID_W), 0)
    kc = lax.broadcasted_iota(jnp.int32, (GRID_W, GRID_W), 1)
    dc = jnp.clip(kc - qc, -(NA_COLS - 1), NA_COLS - 1) + (NA_COLS - 1)
    c0 = jnp.clip(qc - NA_COLS // 2, 0, GRID_W - NA_COLS)
    n_dc = 2 * NA_COLS - 1
    n_dr = 2 * NA_ROWS - 1
    for dr in range(n_dr):
        acc = jnp.zeros((GRID_W, GRID_W), F32)
        for j in range(n_dc):
            acc = jnp.where(dc == j, rpb_ref[(hd * n_dr + dr) * n_dc + j], acc)
        o_ref[0, dr] = jnp.where(kc >= c0, jnp.where(kc < c0 + NA_COLS, acc, NEG_INF), NEG_INF)


def _na_bias(rpb):
    n_dr = 2 * NA_ROWS - 1
    return pl.pallas_call(
        _na_bias_kernel,
        grid=(H_D,),
        in_specs=[pl.BlockSpec(memory_space=pltpu.SMEM)],
        out_specs=pl.BlockSpec((1, n_dr, GRID_W, GRID_W), lambda h: (h, 0, 0, 0)),
        out_shape=jax.ShapeDtypeStruct((H_D, n_dr, GRID_W, GRID_W), F32),
        compiler_params=_params(("arbitrary",)),
        name="na_bias",
    )(rpb.reshape(-1))


def _na_window_start(r):
    return jnp.clip(r - NA_ROWS // 2, 0, GRID_ROWS - NA_ROWS)


def _na_kernel(q_ref, k_ref, v_ref, ck_ref, cv_ref, bias_ref, prev_ref, o_ref):
    r = pl.program_id(1)
    start = pl.multiple_of(_na_window_start(r) * GRID_W, GRID_W)
    n_loc = NA_ROWS * GRID_W
    scale = D_D ** -0.5
    heads = [slice(hd * D_D, (hd + 1) * D_D) for hd in range(H_D)]
    segs = []
    for hd, sl in enumerate(heads):
        q = _scaled(q_ref[:, sl], scale)
        segs.append([_dot_nt(q, k_ref[pl.ds(start, n_loc), sl]) + bias_ref[0, hd],
                     _dot_nt(q, ck_ref[0, :, sl].astype(BF16))])
    exps = [_exp_segments(s) for s in segs]
    for sl, (es, den) in zip(heads, exps):
        o = _pv(es, [v_ref[pl.ds(start, n_loc), sl], cv_ref[0, :, sl].astype(BF16)]) / den
        o_ref[:, sl] = o.astype(BF16)


def _neighbourhood(q, k, v, ck, cv, bias_win, prev_out):
    n_loc = NA_ROWS * GRID_W
    q0 = T_CTX // GRID_W
    k0 = T_CTX // LAT_S
    qspec = pl.BlockSpec((GRID_W, 512), lambda b, r: (q0 + b * GRID_ROWS + r, 0))
    kspec = pl.BlockSpec((LAT_S, 512), lambda b, r: (k0 + b, 0))
    cspec = pl.BlockSpec((1, PAST, 512), lambda b, r: (b, 0, 0))
    return pl.pallas_call(
        _na_kernel,
        grid=(N_LAT_B, GRID_ROWS),
        in_specs=[qspec, kspec, kspec, cspec, cspec,
                  pl.BlockSpec((1, H_D, GRID_W, n_loc),
                               lambda b, r: (_na_window_start(r) - r + NA_ROWS - 1, 0, 0, 0)),
                  pl.BlockSpec(memory_space=pl.ANY)],
        out_specs=qspec,
        out_shape=jax.ShapeDtypeStruct((T_ALL, 512), BF16),
        input_output_aliases={6: 0},
        compiler_params=_params(("arbitrary", "arbitrary")),
        name="attn_neighbourhood",
    )(q, k, v, ck, cv, bias_win, prev_out)


def _post_kernel(x_ref, mod_ref, o1_ref, o2_ref, wo1_ref, wo2_ref, g1_ref, b1_ref,
                 wq_ref, k1_ref, k2_ref, x1_ref, h2t_ref, s1_ref, s2_ref):
    mod = mod_ref[0, 0]
    y = _dot(o1_ref[...], wo1_ref[...]) + _dot(o2_ref[...], wo2_ref[...])
    x1 = _layer_norm(ALPHA * x_ref[...] + mod[2:3] * y, g1_ref[...], b1_ref[...])
    x1_ref[...] = x1
    h2f = x1 * (1.0 + mod[4:5]) + mod[3:4]
    h2 = h2f.astype(BF16)
    h2t_ref[...] = h2f.T.astype(BF16)
    q = _dot(h2, wq_ref[...]).astype(BF16)
    half = P_DKEY // 2
    k1 = k1_ref[...]
    k2 = k2_ref[...]
    for hd in range(P_HEADS):
        s1_ref[hd] = _dot_nt(k1, q[:, hd * P_DKEY:hd * P_DKEY + half])
        s2_ref[hd] = _dot_nt(k2, q[:, hd * P_DKEY + half:(hd + 1) * P_DKEY])


def _post(x, mods, l, o1, o2, wo1, wo2, g1, b1, wq, k1, k2):
    tok = lambda w: pl.BlockSpec((TM, w), lambda i: (i, 0))
    sspec = pl.BlockSpec((P_HEADS, P_KEYS, TM), lambda i: (0, 0, i))
    return pl.pallas_call(
        _post_kernel,
        grid=(N_TILES,),
        in_specs=[tok(D),
                  pl.BlockSpec((1, 1, 6, D), lambda i: (l, _mod_row(i), 0, 0)),
                  tok(512), tok(512), _full((512, D)), _full((512, D)), _full((1, D)), _full((1, D)),
                  _full((D, P_HEADS * P_DKEY)), _full((P_KEYS, P_DKEY // 2)), _full((P_KEYS, P_DKEY // 2))],
        out_specs=[tok(D), pl.BlockSpec((D, TM), lambda i: (0, i)), sspec, sspec],
        out_shape=[jax.ShapeDtypeStruct((T_ALL, D), F32),
                   jax.ShapeDtypeStruct((D, T_ALL), BF16),
                   jax.ShapeDtypeStruct((P_HEADS, P_KEYS, T_ALL), F32),
                   jax.ShapeDtypeStruct((P_HEADS, P_KEYS, T_ALL), F32)],
        compiler_params=_params(("arbitrary",)),
        name="post_mixer",
    )(x, mods, o1, o2, wo1, wo2, g1, b1, wq, k1, k2)


def _tables_kernel(u_ref, v_ref, ub_ref, vt_ref):
    ub_ref[0] = u_ref[0].astype(BF16)
    vt_ref[0] = v_ref[0].T.astype(BF16)


def _prep_tables(pk_u, pk_v):
    blk = pl.BlockSpec((1, PEER_EB, D), lambda l, e: (l, e, 0))
    return pl.pallas_call(
        _tables_kernel,
        grid=(DEPTH, P_EXPERTS // PEER_EB),
        in_specs=[blk, blk],
        out_specs=[blk, pl.BlockSpec((1, D, PEER_EB), lambda l, e: (l, 0, e))],
        out_shape=[jax.ShapeDtypeStruct((DEPTH, P_EXPERTS, D), BF16),
                   jax.ShapeDtypeStruct((DEPTH, D, P_EXPERTS), BF16)],
        compiler_params=_params(("arbitrary", "arbitrary")),
        name="expert_tables",
    )(pk_u, pk_v)


def _sorting_network(n):
    pairs = []
    p = 1
    while p < n:
        k = p
        while k >= 1:
            for j in range(k % p, n - k, 2 * k):
                for i in range(min(k, n - j - k)):
                    if (i + j) // (2 * p) == (i + j + k) // (2 * p):
                        pairs.append((i + j, i + j + k))
            k //= 2
        p *= 2
    return pairs


def _top_values(x, n):
    depth = x.shape[0] // 8
    groups = [x[8 * k:8 * k + 8, :] for k in range(depth)]
    size = 1
    while size < depth:
        size *= 2
    bottom = jnp.full_like(groups[0], NEG_INF)
    groups = groups + [bottom] * (size - depth)
    for i, j in _sorting_network(size):
        a, b = groups[i], groups[j]
        groups[i], groups[j] = jnp.maximum(a, b), jnp.minimum(a, b)
    groups = groups[:depth]
    vals = []
    for t in range(n):
        head = groups[0]
        m = jnp.max(head, axis=0, keepdims=True)
        vals.append(m)
        taken = head == m
        for k in range(min(n - 1 - t, depth)):
            groups[k] = jnp.where(taken, groups[k + 1] if k + 1 < depth else bottom, groups[k])
    return vals


PEER_PAIRS = [(a, b) for a in range(PEER_NSEL) for b in range(PEER_NSEL) if (a + 1) * (b + 1) <= PEER_NSEL]
PEER_NCAND = -(-len(PEER_PAIRS) // 8) * 8


def _peer_select(s1_ref, s2_ref, eth_scr, e1_scr, e2_scr, cand_scr):
    n_pad = PEER_NCAND - len(PEER_PAIRS)
    cand_scr[len(PEER_PAIRS):, :] = jnp.full((n_pad, cand_scr.shape[1]), NEG_INF, F32)

    def body(hd, carry):
        s1 = s1_ref[hd]
        s2 = s2_ref[hd]
        v1 = _top_values(s1, PEER_NSEL)
        v2 = _top_values(s2, PEER_NSEL)
        for k, (a, b) in enumerate(PEER_PAIRS):
            cand_scr[k:k + 1, :] = v1[a] + v2[b]
        cs = _top_values(cand_scr[...], PEER_NSEL)
        tau = 0.5 * (cs[P_TOPK - 1] + cs[P_TOPK])
        z = jnp.exp(cs[0] - cs[0])
        for j in range(1, P_TOPK):
            z = z + jnp.exp(cs[j] - cs[0])
        half_inv_z = 0.5 / z
        eth_scr[hd] = jnp.exp((tau - s1) - v2[0]) * half_inv_z
        e1_scr[hd] = jnp.exp(s1 - v1[0])
        e2_scr[hd] = jnp.exp(s2 - v2[0]) * half_inv_z
        return carry

    lax.fori_loop(0, P_HEADS, body, 0)


def _gelu2(a):
    return a * (1.0 + lax.erf(a * (2.0 ** -0.5)))


PEER_LW = 256
PEER_CH = 256


def _peer_kernel(h2t_ref, s1_ref, s2_ref, u_ref, vt_prev_ref, vt_last_ref, x1_ref, mod_ref, g2_ref, b2_ref,
                 o_ref, eth_scr, e1_scr, e2_scr, cand_scr, c_cur, c_prev, acc_scr):
    eb = pl.program_id(1)
    n_eb = pl.num_programs(1)
    rows_per_step = PEER_EB // P_KEYS

    @pl.when(eb == 0)
    def _():
        _peer_select(s1_ref, s2_ref, eth_scr, e1_scr, e2_scr, cand_scr)
        acc_scr[...] = jnp.zeros_like(acc_scr)
        c_prev[...] = jnp.zeros_like(c_prev)

    @pl.when(eb > 0)
    def _():
        c_prev[...] = c_cur[...]

    h2t = h2t_ref[...]
    n_ch = PEER_EB // PEER_CH
    rb = D // n_ch
    act_next = _dot(u_ref[0:PEER_CH, :], h2t)
    for k in range(n_ch):
        act = act_next
        if k + 1 < n_ch:
            act_next = _dot(u_ref[(k + 1) * PEER_CH:(k + 2) * PEER_CH, :], h2t)
        acc_scr[k * rb:(k + 1) * rb, :] += _dot(vt_prev_ref[k * rb:(k + 1) * rb, :], c_prev[...])
        for jj in range(PEER_CH // P_KEYS):
            j = k * (PEER_CH // P_KEYS) + jj
            i1 = eb * rows_per_step + j
            for lc in range(PEER_TT // PEER_LW):
                ls = slice(lc * PEER_LW, (lc + 1) * PEER_LW)
                w = jnp.zeros((P_KEYS, PEER_LW), F32)
                for hd in range(P_HEADS):
                    eth = eth_scr[hd, pl.ds(i1, 1), ls]
                    e1 = e1_scr[hd, pl.ds(i1, 1), ls]
                    e2 = e2_scr[hd, :, ls]
                    w = w + jnp.where(e2 >= eth, e2, 0.0) * e1
                a = act[jj * P_KEYS:(jj + 1) * P_KEYS, ls]
                c_cur[j * P_KEYS:(j + 1) * P_KEYS, ls] = (w * _gelu2(a)).astype(BF16)

    @pl.when(eb == n_eb - 1)
    def _():
        mod = mod_ref[0, 0]
        peer = (acc_scr[...] + _dot(vt_last_ref[...], c_cur[...])).T
        o_ref[...] = _layer_norm(ALPHA * x1_ref[...] + mod[5:6] * peer, g2_ref[...], b2_ref[...])


def _peer(h2t, s1, s2, u_bf, vt_bf, x1, mods, l, g2, b2):
    tiles_ctx = T_CTX // PEER_TT
    tiles_per_lat = LAT_S // PEER_TT
    n_eb = P_EXPERTS // PEER_EB
    mod_row = lambda i: jnp.where(i < tiles_ctx, 0, 1 + (i - tiles_ctx) // tiles_per_lat)
    tok = lambda w: pl.BlockSpec((PEER_TT, w), lambda i, e: (i, 0))
    sspec = pl.BlockSpec((P_HEADS, P_KEYS, PEER_TT), lambda i, e: (0, 0, i))
    sel = pltpu.VMEM((P_HEADS, P_KEYS, PEER_TT), F32)
    return pl.pallas_call(
        _peer_kernel,
        grid=(T_ALL // PEER_TT, n_eb),
        in_specs=[pl.BlockSpec((D, PEER_TT), lambda i, e: (0, i)), sspec, sspec,
                  pl.BlockSpec((None, PEER_EB, D), lambda i, e: (l, e, 0)),
                  pl.BlockSpec((None, D, PEER_EB), lambda i, e: (l, 0, jnp.maximum(e - 1, 0))),
                  pl.BlockSpec((None, D, PEER_EB), lambda i, e: (l, 0, jnp.where(e == n_eb - 1, n_eb - 1, 0))),
                  tok(D),
                  pl.BlockSpec((1, 1, 6, D), lambda i, e: (l, mod_row(i), 0, 0)),
                  pl.BlockSpec((1, D), lambda i, e: (0, 0)),
                  pl.BlockSpec((1, D), lambda i, e: (0, 0))],
        out_specs=tok(D),
        out_shape=jax.ShapeDtypeStruct((T_ALL, D), F32),
        scratch_shapes=[sel, sel, sel,
                        pltpu.VMEM((PEER_NCAND, PEER_TT), F32),
                        pltpu.VMEM((PEER_EB, PEER_TT), BF16),
                        pltpu.VMEM((PEER_EB, PEER_TT), BF16),
                        pltpu.VMEM((D, PEER_TT), F32)],
        compiler_params=_params(("arbitrary", "arbitrary")),
        name="peer",
    )(h2t, s1, s2, u_bf, vt_bf, vt_bf, x1, mods, g2, b2)


def _grid_angles(rot_dim):
    t = jnp.arange(LAT_S)
    row = (t // GRID_W).astype(F32)
    col = (t % GRID_W).astype(F32)
    n_freq = rot_dim // 4
    inv = THETA ** (-jnp.arange(n_freq, dtype=F32) / n_freq)
    return jnp.concatenate([row[:, None] * inv, col[:, None] * inv], axis=-1)


def _rope_tables(rot_dim, width):
    ang = _grid_angles(rot_dim)
    cos = jnp.cos(ang)
    sin = jnp.sin(ang)
    cos_t = jnp.tile(jnp.concatenate([cos, cos], axis=-1), (1, width // rot_dim))
    sin_t = jnp.tile(jnp.concatenate([-sin, sin], axis=-1), (1, width // rot_dim))
    ident_c = jnp.ones((1, TM, width), F32)
    ident_s = jnp.zeros((1, TM, width), F32)
    cos_t = jnp.concatenate([ident_c, cos_t.reshape(TILES_PER_LAT, TM, width)], axis=0)
    sin_t = jnp.concatenate([ident_s, sin_t.reshape(TILES_PER_LAT, TM, width)], axis=0)
    return cos_t, sin_t


def kernel(x_prompt, x_sample, cache_a_k, cache_a_v, cache_b_ckv, cache_b_kr, cache_c_k, cache_c_v, cache_d_k, cache_d_v, c, c_ctx, w_mod, b_mod, ln1_g, ln1_b, ln2_g, ln2_b, ev_w_in, ev_lam_q1, ev_lam_k1, ev_lam_q2, ev_lam_k2, ev_g_sub, ev_g_cq, ev_w_uq, ev_g_ckv, ev_w_ukv, ev_w_out, od_w_in, od_g_qn, od_g_kn, od_rpb, od_w_out, pk_w_q, pk_k1, pk_k2, pk_u, pk_v):
    x = jnp.concatenate([x_prompt.reshape(T_CTX, D), x_sample.reshape(T_LAT, D)], axis=0)
    cvecs = jnp.concatenate([c_ctx[None, :], c, jnp.zeros((8 - 1 - N_LAT_B, D), F32)], axis=0)
    mods = _modulation(cvecs, w_mod, b_mod).reshape(DEPTH, 8, 6, D)
    u_bf, vt_bf = _prep_tables(pk_u, pk_v)

    cos_a, sin_a = _rope_tables(D_A, 512)
    cos_b, sin_b = _rope_tables(B_ROPE, 256)
    rt_even = jnp.concatenate([cos_a, sin_a, cos_b, sin_b], axis=-1)
    rt_odd = jnp.concatenate(_rope_tables(D_C, 512), axis=-1)
    ones_bd = jnp.kron(jnp.eye(512 // D_C, dtype=F32), jnp.ones((D_C, D_C), F32)).astype(BF16)
    lat3 = lambda a, w: a.reshape(N_LAT_B, PAST, w)

    states = {}
    for l in range(DEPTH):
        j = l // 2
        if l % 2 == 0:
            lam_init = 0.8 - 0.6 * math.exp(-0.3 * l)
            w_in = jnp.pad(ev_w_in[j], ((0, 0), (0, EVEN_IN_PAD - ev_w_in.shape[-1]))).astype(BF16)
            wuq = ev_w_uq[j].reshape(B_QRANK, H_B, B_NOPE + B_ROPE)
            wuq = jnp.concatenate([wuq[:, :, :B_NOPE].reshape(B_QRANK, -1),
                                   wuq[:, :, B_NOPE:].reshape(B_QRANK, -1)], axis=1).astype(BF16)
            wukv = ev_w_ukv[j].reshape(B_KVRANK, H_B, B_NOPE + B_V)
            wukv = jnp.concatenate([wukv[:, :, :B_NOPE].reshape(B_KVRANK, -1),
                                    wukv[:, :, B_NOPE:].reshape(B_KVRANK, -1)], axis=1).astype(BF16)
            (qa, bq, ka, va, kvb, kr,
             states["a_k"], states["a_v"], states["b_ckv"], states["b_kr"]) = _proj_call(
                _proj_even_kernel, "proj_even", x, mods, l, rt_even,
                [w_in, ev_g_cq[j][None, :], wuq, ev_g_ckv[j][None, :], wukv],
                (512, 768, 512, 512, 1024, B_ROPE), (512, 512, B_KVRANK, B_ROPE))
            lam_vecs = [v[j][None, :] for v in (ev_lam_q1, ev_lam_k1, ev_lam_q2, ev_lam_k2)]
            g_sub = ev_g_sub[j][None, :]
            proj = (qa, bq, ka, va, kvb, kr)
            outs = _attn_even(lam_vecs, g_sub, lam_init, *proj)
            cache_kvb = _matmul(cache_b_ckv[:, j].reshape(N_LAT_B * PAST, B_KVRANK), wukv)
            cache = (lat3(cache_a_k[:, j], 512), lat3(cache_a_v[:, j], 512), cache_kvb,
                     lat3(cache_b_kr[:, j], B_ROPE))
            o1, o2 = _attn_even(lam_vecs, g_sub, lam_init, *proj, cache=cache, prev_out=outs)
            w_out = ev_w_out[j]
        else:
            w_in = od_w_in[j].astype(BF16)
            g_qn = jnp.tile(od_g_qn[j], H_C)[None, :]
            g_kn = jnp.tile(od_g_kn[j], G_C)[None, :]
            (cq, ck, cv, dq, dk, dv,
             states["c_k"], states["c_v"], states["d_k"], states["d_v"]) = _proj_call(
                _proj_odd_kernel, "proj_odd", x, mods, l, rt_odd, [w_in, g_qn, g_kn, ones_bd],
                (512, 128, 128, 512, 512, 512), (128, 128, 512, 512))
            o1 = _gqa(cq, ck, cv)
            o1 = _gqa(cq, ck, cv, cache=(lat3(cache_c_k[:, j], 128), lat3(cache_c_v[:, j], 128)),
                      prev_out=[o1])
            o2 = _mha(dq, dk, dv)
            bias = _na_bias(od_rpb[j])
            offs = jnp.arange(NA_ROWS + 1)[:, None] + jnp.arange(NA_ROWS)[None, :]
            bias_win = jnp.transpose(bias[:, offs], (1, 0, 3, 2, 4)).reshape(
                NA_ROWS + 1, H_D, GRID_W, NA_ROWS * GRID_W)
            o2 = _neighbourhood(dq, dk, dv, lat3(cache_d_k[:, j], 512), lat3(cache_d_v[:, j], 512),
                                bias_win, o2)
            w_out = od_w_out[j]

        x1, h2t, s1, s2 = _post(x, mods, l, o1, o2, w_out[:512].astype(BF16), w_out[512:].astype(BF16),
                                ln1_g[l][None, :], ln1_b[l][None, :], pk_w_q[l].astype(BF16),
                                pk_k1[l].astype(BF16), pk_k2[l].astype(BF16))
        x = _peer(h2t, s1, s2, u_bf, vt_bf, x1, mods, l, ln2_g[l][None, :], ln2_b[l][None, :])

    y = x[:T_CTX].reshape(N_CTX_B, CTX_S, D)
    z = x[T_CTX:].reshape(N_LAT_B, LAT_S, D)
    n_even = (DEPTH + 1) // 2
    n_odd = DEPTH // 2
    return (y, z,
            states["a_k"].reshape(N_CTX_B, n_even, CTX_S, H_A, 2, D_A),
            states["a_v"].reshape(N_CTX_B, n_even, CTX_S, H_A, 2 * D_A),
            states["b_ckv"].reshape(N_CTX_B, n_even, CTX_S, B_KVRANK),
            states["b_kr"].reshape(N_CTX_B, n_even, CTX_S, B_ROPE),
            states["c_k"].reshape(N_CTX_B, n_odd, CTX_S, G_C, D_C),
            states["c_v"].reshape(N_CTX_B, n_odd, CTX_S, G_C, D_C),
            states["d_k"].reshape(N_CTX_B, n_odd, CTX_S, H_D, D_D),
            states["d_v"].reshape(N_CTX_B, n_odd, CTX_S, H_D, D_D))
```

```python
import functools
import math

import jax
import jax.numpy as jnp
from jax import lax
from jax.experimental import pallas as pl
from jax.experimental.pallas import tpu as pltpu

F32 = jnp.float32
BF16 = jnp.bfloat16

D = 1024
N_CTX_B = 32
CTX_S = 256
N_LAT_B = 4
LAT_S = 1024
PAST = 256
T_CTX = N_CTX_B * CTX_S
T_LAT = N_LAT_B * LAT_S
T_ALL = T_CTX + T_LAT
DEPTH = 2
GRID_W = 64
GRID_ROWS = LAT_S // GRID_W
THETA = 10000.0
EPS = 1e-6
ALPHA = (2 * DEPTH) ** 0.25

H_A, D_A = 4, 64
H_B, B_NOPE, B_ROPE, B_V, B_QRANK, B_KVRANK = 8, 64, 32, 64, 256, 128
H_C, G_C, D_C = 8, 2, 64
H_D, D_D = 8, 64
NA_ROWS, NA_COLS = 8, 16
P_HEADS, P_KEYS, P_DKEY, P_TOPK = 8, 128, 256, 16
P_EXPERTS = P_KEYS * P_KEYS

EVEN_IN_PAD = 2048
ODD_IN = 2304

TM = 512
N_TILES = T_ALL // TM
N_CTX_TILES = T_CTX // TM
TILES_PER_LAT = LAT_S // TM
N_LAT_TILES = T_LAT // TM
TQ = 256
PEER_TT = 512
PEER_EB = 1024
PEER_NSEL = P_TOPK + 1
VMEM_LIMIT = 56 * 1024 * 1024

NEG_INF = float("-inf")


def _dot(a, b):
    return jnp.dot(a, b, preferred_element_type=F32)


def _dot_nt(a, b):
    return lax.dot_general(a, b, (((1,), (1,)), ((), ())), preferred_element_type=F32)


def _layer_norm(x, g, b):
    mu = jnp.mean(x, axis=-1, keepdims=True)
    xc = x - mu
    var = jnp.mean(xc * xc, axis=-1, keepdims=True)
    return xc * lax.rsqrt(var + EPS) * g + b


def _rms_norm(x, g):
    return x * lax.rsqrt(jnp.mean(x * x, axis=-1, keepdims=True) + EPS) * g


def _chunk_rms_norm(x, g, ones_bd, chunk):
    x2 = x * x
    hi = x2.astype(BF16)
    lo = (x2 - hi.astype(F32)).astype(BF16)
    ms = (_dot(hi, ones_bd) + _dot(lo, ones_bd)) * (1.0 / chunk)
    return x * lax.rsqrt(ms + EPS) * g


def _rope(x, cos_t, sin_t, chunk):
    n = x.shape[-1]
    half = chunk // 2
    lane = lax.broadcasted_iota(jnp.int32, x.shape, x.ndim - 1)
    swapped = jnp.where((lane % chunk) < half,
                        pltpu.roll(x, n - half, x.ndim - 1),
                        pltpu.roll(x, half, x.ndim - 1))
    return x * cos_t + swapped * sin_t


def _params(sem):
    return pltpu.CompilerParams(dimension_semantics=sem, vmem_limit_bytes=VMEM_LIMIT)


def _full(shape):
    zeros = (0,) * len(shape)
    return pl.BlockSpec(shape, lambda *_: zeros)


def _mod_row(i):
    return jnp.where(i < N_CTX_TILES, 0, 1 + (i - N_CTX_TILES) // TILES_PER_LAT)


def _rope_blk(i):
    return jnp.where(i < N_CTX_TILES, 0, 1 + (i - N_CTX_TILES) % TILES_PER_LAT)


def _mod_kernel(c_ref, w_ref, b_ref, o_ref):
    c = c_ref[...]
    a = c * (1.0 / (1.0 + jnp.exp(-c)))
    o_ref[0] = _dot(a.astype(BF16), w_ref[0].astype(BF16)) + b_ref[0]


def _modulation(cvecs, w_mod, b_mod):
    nb = 6
    bn = 6 * D // nb
    return pl.pallas_call(
        _mod_kernel,
        grid=(DEPTH, nb),
        in_specs=[pl.BlockSpec((8, D), lambda l, j: (0, 0)),
                  pl.BlockSpec((1, D, bn), lambda l, j: (l, 0, j)),
                  pl.BlockSpec((1, 1, bn), lambda l, j: (l, 0, j))],
        out_specs=pl.BlockSpec((1, 8, bn), lambda l, j: (l, 0, j)),
        out_shape=jax.ShapeDtypeStruct((DEPTH, 8, 6 * D), F32),
        compiler_params=_params(("arbitrary", "arbitrary")),
        name="modulation",
    )(cvecs, w_mod, b_mod.reshape(DEPTH, 1, 6 * D))


def _tile(g):
    return (g + N_CTX_TILES) % N_TILES


def _state_blk(g):
    return jnp.maximum(g - N_LAT_TILES, 0)


def _proj_even_kernel(x_ref, mod_ref, rt_ref, win_ref, gcq_ref, wuq_ref, gckv_ref, wukv_ref,
                      qa_ref, bq_ref, ka_ref, va_ref, kvb_ref, kr_ref,
                      s_ak_ref, s_av_ref, s_ckv_ref, s_kr_ref):
    x = x_ref[...]
    mod = mod_ref[0, 0]
    h = x * (1.0 + mod[1:2]) + mod[0:1]
    p = _dot(h.astype(BF16), win_ref[...])
    rt = rt_ref[0]
    cos_a, sin_a = rt[:, 0:512], rt[:, 512:1024]
    cos_b, sin_b = rt[:, 1024:1280], rt[:, 1280:1536]
    qa_ref[...] = _rope(p[:, 0:512], cos_a, sin_a, D_A).astype(BF16)
    ka = _rope(p[:, 512:1024], cos_a, sin_a, D_A)
    ka_ref[...] = ka.astype(BF16)
    s_ak_ref[...] = ka
    va = p[:, 1024:1536]
    va_ref[...] = va.astype(BF16)
    s_av_ref[...] = va
    cq = _rms_norm(p[:, 1536:1792], gcq_ref[...])
    bq = _dot(cq.astype(BF16), wuq_ref[...])
    bq_ref[:, 0:512] = bq[:, 0:512].astype(BF16)
    bq_ref[:, 512:768] = _rope(bq[:, 512:768], cos_b, sin_b, B_ROPE).astype(BF16)
    ckv = _rms_norm(p[:, 1792:1920], gckv_ref[...])
    s_ckv_ref[...] = ckv
    kvb_ref[...] = _dot(ckv.astype(BF16), wukv_ref[...]).astype(BF16)
    kr = _rope(p[:, 1920:2048], cos_b[:, 0:128], sin_b[:, 0:128], B_ROPE)[:, 0:B_ROPE]
    kr_ref[...] = kr.astype(BF16)
    s_kr_ref[...] = kr


def _proj_call(kernel_fn, name, x, mods, l, rt, weights, widths, state_widths):
    tok = lambda w: pl.BlockSpec((TM, w), lambda g: (_tile(g), 0))
    st = lambda w: pl.BlockSpec((TM, w), lambda g: (_state_blk(g), 0))
    return pl.pallas_call(
        kernel_fn,
        grid=(N_TILES,),
        in_specs=[tok(D),
                  pl.BlockSpec((1, 1, 6, D), lambda g: (l, _mod_row(_tile(g)), 0, 0)),
                  pl.BlockSpec((1, TM, rt.shape[-1]), lambda g: (_rope_blk(_tile(g)), 0, 0))]
                 + [_full(w.shape) for w in weights],
        out_specs=[tok(w) for w in widths] + [st(w) for w in state_widths],
        out_shape=[jax.ShapeDtypeStruct((T_ALL, w), BF16) for w in widths]
                  + [jax.ShapeDtypeStruct((T_CTX, w), F32) for w in state_widths],
        compiler_params=_params(("arbitrary",)),
        name=name,
    )(x, mods, rt, *weights)


def _proj_odd_kernel(x_ref, mod_ref, rt_ref, win_ref, gq_ref, gk_ref, bd_ref,
                     cq_ref, ck_ref, cv_ref, dq_ref, dk_ref, dv_ref,
                     s_ck_ref, s_cv_ref, s_dk_ref, s_dv_ref):
    x = x_ref[...]
    mod = mod_ref[0, 0]
    h = x * (1.0 + mod[1:2]) + mod[0:1]
    p = _dot(h.astype(BF16), win_ref[...])
    rt = rt_ref[0]
    cos_c, sin_c = rt[:, 0:512], rt[:, 512:1024]
    bd = bd_ref[...]
    cq = _chunk_rms_norm(p[:, 0:512], gq_ref[...], bd, D_C)
    cq_ref[...] = _rope(cq, cos_c, sin_c, D_C).astype(BF16)
    ck = _chunk_rms_norm(p[:, 512:640], gk_ref[...], bd[0:128, 0:128], D_C)
    ck = _rope(ck, cos_c[:, 0:128], sin_c[:, 0:128], D_C)
    ck_ref[...] = ck.astype(BF16)
    s_ck_ref[...] = ck
    cv = p[:, 640:768]
    cv_ref[...] = cv.astype(BF16)
    s_cv_ref[...] = cv
    dq_ref[...] = p[:, 768:1280].astype(BF16)
    dk = p[:, 1280:1792]
    dk_ref[...] = dk.astype(BF16)
    s_dk_ref[...] = dk
    dv = p[:, 1792:2304]
    dv_ref[...] = dv.astype(BF16)
    s_dv_ref[...] = dv


def _matmul_kernel(x_ref, w_ref, o_ref):
    o_ref[...] = _dot(x_ref[...].astype(BF16), w_ref[...]).astype(BF16)


def _matmul(x, w):
    m, k = x.shape
    n = w.shape[1]
    return pl.pallas_call(
        _matmul_kernel,
        grid=(m // TM,),
        in_specs=[pl.BlockSpec((TM, k), lambda i: (i, 0)), _full((k, n))],
        out_specs=pl.BlockSpec((TM, n), lambda i: (i, 0)),
        out_shape=jax.ShapeDtypeStruct((m, n), BF16),
        compiler_params=_params(("arbitrary",)),
        name="cache_kv_up",
    )(x, w)


def _exp_segments(segs):
    m = jnp.max(segs[0], axis=-1, keepdims=True)
    for s in segs[1:]:
        m = jnp.maximum(m, jnp.max(s, axis=-1, keepdims=True))
    es = [jnp.exp(s - m) for s in segs]
    den = jnp.sum(es[0], axis=-1, keepdims=True)
    for e in es[1:]:
        den = den + jnp.sum(e, axis=-1, keepdims=True)
    return es, den


def _pv(ps, vs):
    o = _dot(ps[0].astype(BF16), vs[0])
    for p, v in zip(ps[1:], vs[1:]):
        o = o + _dot(p.astype(BF16), v)
    return o


def _attend_group(seg_lists, vs_lists):
    exps = [_exp_segments(segs) for segs in seg_lists]
    return [_pv(es, vs) / den for (es, den), vs in zip(exps, vs_lists)]


def _scaled(q, scale):
    return (q.astype(F32) * scale).astype(q.dtype)


def _attn_specs(row0, n_req, n_q, widths_q, widths_k, cache_shapes):
    n_own = n_q
    nqb = n_q // TQ
    q0 = row0 // TQ
    k0 = row0 // n_own
    qspec = lambda w: pl.BlockSpec((TQ, w), lambda b, j: (q0 + b * nqb + j, 0))
    kspec = lambda w: pl.BlockSpec((n_own, w), lambda b, j: (k0 + b, 0))
    cspec = lambda shp: pl.BlockSpec((1,) + tuple(shp[1:]), lambda b, j: (b,) + (0,) * (len(shp) - 1))
    specs = [qspec(w) for w in widths_q] + [kspec(w) for w in widths_k] + [cspec(s) for s in cache_shapes]
    return specs, qspec, (n_req, nqb)


def _attn_even_kernel(*refs, lam_init, has_cache):
    lq1_ref, lk1_ref, lq2_ref, lk2_ref, gsub_ref, qa_ref, bq_ref, ka_ref, va_ref, kvb_ref, kr_ref = refs[:11]
    if has_cache:
        cka_ref, cva_ref, ckvb_ref, ckr_ref = refs[11:15]
    oa_ref, ob_ref = refs[-2:]
    lam = (jnp.exp(jnp.sum(lq1_ref[...] * lk1_ref[...], axis=-1, keepdims=True))
           - jnp.exp(jnp.sum(lq2_ref[...] * lk2_ref[...], axis=-1, keepdims=True)) + lam_init)
    gsub = gsub_ref[...]
    grp_a, grp_b = (2, 4) if has_cache else (H_A, H_B)
    for h0 in range(0, H_A, grp_a):
        seg_lists, vs_lists = [], []
        for hd in range(h0, h0 + grp_a):
            vsl = slice(hd * 2 * D_A, (hd + 1) * 2 * D_A)
            vs = [va_ref[:, vsl]] + ([cva_ref[0, :, vsl].astype(BF16)] if has_cache else [])
            for m in range(2):
                sl = slice(hd * 2 * D_A + m * D_A, hd * 2 * D_A + (m + 1) * D_A)
                q = _scaled(qa_ref[:, sl], D_A ** -0.5)
                segs = [_dot_nt(q, ka_ref[:, sl])]
                if has_cache:
                    segs.append(_dot_nt(q, cka_ref[0, :, sl].astype(BF16)))
                seg_lists.append(segs)
                vs_lists.append(vs)
        os = _attend_group(seg_lists, vs_lists)
        for i, hd in enumerate(range(h0, h0 + grp_a)):
            vsl = slice(hd * 2 * D_A, (hd + 1) * 2 * D_A)
            o = os[2 * i] - lam * os[2 * i + 1]
            oa_ref[:, vsl] = (_rms_norm(o, gsub) * (1.0 - lam_init)).astype(BF16)
    kr = kr_ref[...]
    if has_cache:
        ckr = ckr_ref[0].astype(BF16)
    sc = (B_NOPE + B_ROPE) ** -0.5
    for h0 in range(0, H_B, grp_b):
        seg_lists, vs_lists = [], []
        for hd in range(h0, h0 + grp_b):
            nsl = slice(hd * B_NOPE, (hd + 1) * B_NOPE)
            vsl = slice(512 + hd * B_V, 512 + (hd + 1) * B_V)
            qn = bq_ref[:, nsl]
            qr = bq_ref[:, 512 + hd * B_ROPE:512 + (hd + 1) * B_ROPE]
            segs = [(_dot_nt(qn, kvb_ref[:, nsl]) + _dot_nt(qr, kr)) * sc]
            vs = [kvb_ref[:, vsl]]
            if has_cache:
                segs.append((_dot_nt(qn, ckvb_ref[0, :, nsl]) + _dot_nt(qr, ckr)) * sc)
                vs.append(ckvb_ref[0, :, vsl])
            seg_lists.append(segs)
            vs_lists.append(vs)
        for hd, o in zip(range(h0, h0 + grp_b), _attend_group(seg_lists, vs_lists)):
            ob_ref[:, hd * B_V:(hd + 1) * B_V] = o.astype(BF16)


def _attn_call(kernel_fn, name, params, arrays, widths_q, widths_k, latent, cache, prev_out, n_out):
    row0, n_req, n_q = (T_CTX, N_LAT_B, LAT_S) if latent else (0, N_CTX_B, CTX_S)
    cache = list(cache)
    cache_shapes = []
    for a in cache:
        cache_shapes.append(a.shape if a.ndim == 3 else (n_req,) + (a.shape[0] // n_req, a.shape[1]))
    cache = [a.reshape(shp) for a, shp in zip(cache, cache_shapes)]
    specs, qspec, grid = _attn_specs(row0, n_req, n_q, widths_q, widths_k, cache_shapes)
    in_specs = [_full(p.shape) for p in params] + specs
    operands = list(params) + list(arrays) + cache
    aliases = {}
    if prev_out is not None:
        for i, o in enumerate(prev_out):
            aliases[len(operands)] = i
            operands.append(o)
            in_specs.append(pl.BlockSpec(memory_space=pl.ANY))
    return pl.pallas_call(
        kernel_fn,
        grid=grid,
        in_specs=in_specs,
        out_specs=[qspec(512)] * n_out,
        out_shape=[jax.ShapeDtypeStruct((T_ALL, 512), BF16)] * n_out,
        input_output_aliases=aliases,
        compiler_params=_params(("arbitrary", "arbitrary")),
        name=name,
    )(*operands)


def _attn_even(lam_vecs, g_sub, lam_init, qa, bq, ka, va, kvb, kr, cache=(), prev_out=None):
    latent = prev_out is not None
    fn = functools.partial(_attn_even_kernel, lam_init=lam_init, has_cache=latent)
    return _attn_call(fn, "attn_even", list(lam_vecs) + [g_sub], [qa, bq, ka, va, kvb, kr],
                      (512, 768), (512, 512, 1024, B_ROPE), latent, cache, prev_out, 2)


def _gqa_kernel(*refs, has_cache):
    cq_ref, ck_ref, cv_ref = refs[:3]
    if has_cache:
        cck_ref, ccv_ref = refs[3:5]
    o_ref = refs[-1]
    grp = 4 if has_cache else H_C
    for h0 in range(0, H_C, grp):
        seg_lists, vs_lists = [], []
        for hd in range(h0, h0 + grp):
            g = hd // (H_C // G_C)
            gsl = slice(g * D_C, (g + 1) * D_C)
            q = _scaled(cq_ref[:, hd * D_C:(hd + 1) * D_C], D_C ** -0.5)
            segs = [_dot_nt(q, ck_ref[:, gsl])]
            vs = [cv_ref[:, gsl]]
            if has_cache:
                segs.append(_dot_nt(q, cck_ref[0, :, gsl].astype(BF16)))
                vs.append(ccv_ref[0, :, gsl].astype(BF16))
            seg_lists.append(segs)
            vs_lists.append(vs)
        for hd, o in zip(range(h0, h0 + grp), _attend_group(seg_lists, vs_lists)):
            o_ref[:, hd * D_C:(hd + 1) * D_C] = o.astype(BF16)


def _gqa(cq, ck, cv, cache=(), prev_out=None):
    latent = prev_out is not None
    fn = functools.partial(_gqa_kernel, has_cache=latent)
    return _attn_call(fn, "attn_gqa", [], [cq, ck, cv], (512,), (128, 128), latent, cache, prev_out, 1)[0]


def _mha_kernel(q_ref, k_ref, v_ref, o_ref):
    heads = [slice(hd * D_D, (hd + 1) * D_D) for hd in range(H_D)]
    seg_lists = [[_dot_nt(_scaled(q_ref[:, sl], D_D ** -0.5), k_ref[:, sl])] for sl in heads]
    for sl, o in zip(heads, _attend_group(seg_lists, [[v_ref[:, sl]] for sl in heads])):
        o_ref[:, sl] = o.astype(BF16)


def _mha(q, k, v):
    return _attn_call(_mha_kernel, "attn_mha", [], [q, k, v], (512,), (512, 512), False, (), None, 1)[0]


def _na_bias_kernel(rpb_ref, o_ref):
    hd = pl.program_id(0)
    qc = lax.broadcasted_iota(jnp.int32, (GRID_W, GRID_W), 0)
    kc = lax.broadcasted_iota(jnp.int32, (GRID_W, GRID_W), 1)
    dc = jnp.clip(kc - qc, -(NA_COLS - 1), NA_COLS - 1) + (NA_COLS - 1)
    c0 = jnp.clip(qc - NA_COLS // 2, 0, GRID_W - NA_COLS)
    n_dc = 2 * NA_COLS - 1
    n_dr = 2 * NA_ROWS - 1
    for dr in range(n_dr):
        acc = jnp.zeros((GRID_W, GRID_W), F32)
        for j in range(n_dc):
            acc = jnp.where(dc == j, rpb_ref[(hd * n_dr + dr) * n_dc + j], acc)
        o_ref[0, dr] = jnp.where(kc >= c0, jnp.where(kc < c0 + NA_COLS, acc, NEG_INF), NEG_INF)


def _na_bias(rpb):
    n_dr = 2 * NA_ROWS - 1
    return pl.pallas_call(
        _na_bias_kernel,
        grid=(H_D,),
        in_specs=[pl.BlockSpec(memory_space=pltpu.SMEM)],
        out_specs=pl.BlockSpec((1, n_dr, GRID_W, GRID_W), lambda h: (h, 0, 0, 0)),
        out_shape=jax.ShapeDtypeStruct((H_D, n_dr, GRID_W, GRID_W), F32),
        compiler_params=_params(("arbitrary",)),
        name="na_bias",
    )(rpb.reshape(-1))


def _na_window_start(r):
    return jnp.clip(r - NA_ROWS // 2, 0, GRID_ROWS - NA_ROWS)


def _na_kernel(q_ref, k_ref, v_ref, ck_ref, cv_ref, bias_ref, prev_ref, o_ref):
    r = pl.program_id(1)
    start = pl.multiple_of(_na_window_start(r) * GRID_W, GRID_W)
    n_loc = NA_ROWS * GRID_W
    scale = D_D ** -0.5
    heads = [slice(hd * D_D, (hd + 1) * D_D) for hd in range(H_D)]
    segs = []
    for hd, sl in enumerate(heads):
        q = _scaled(q_ref[:, sl], scale)
        segs.append([_dot_nt(q, k_ref[pl.ds(start, n_loc), sl]) + bias_ref[0, hd],
                     _dot_nt(q, ck_ref[0, :, sl].astype(BF16))])
    exps = [_exp_segments(s) for s in segs]
    for sl, (es, den) in zip(heads, exps):
        o = _pv(es, [v_ref[pl.ds(start, n_loc), sl], cv_ref[0, :, sl].astype(BF16)]) / den
        o_ref[:, sl] = o.astype(BF16)


def _neighbourhood(q, k, v, ck, cv, bias_win, prev_out):
    n_loc = NA_ROWS * GRID_W
    q0 = T_CTX // GRID_W
    k0 = T_CTX // LAT_S
    qspec = pl.BlockSpec((GRID_W, 512), lambda b, r: (q0 + b * GRID_ROWS + r, 0))
    kspec = pl.BlockSpec((LAT_S, 512), lambda b, r: (k0 + b, 0))
    cspec = pl.BlockSpec((1, PAST, 512), lambda b, r: (b, 0, 0))
    return pl.pallas_call(
        _na_kernel,
        grid=(N_LAT_B, GRID_ROWS),
        in_specs=[qspec, kspec, kspec, cspec, cspec,
                  pl.BlockSpec((1, H_D, GRID_W, n_loc),
                               lambda b, r: (_na_window_start(r) - r + NA_ROWS - 1, 0, 0, 0)),
                  pl.BlockSpec(memory_space=pl.ANY)],
        out_specs=qspec,
        out_shape=jax.ShapeDtypeStruct((T_ALL, 512), BF16),
        input_output_aliases={6: 0},
        compiler_params=_params(("arbitrary", "arbitrary")),
        name="attn_neighbourhood",
    )(q, k, v, ck, cv, bias_win, prev_out)


def _post_kernel(x_ref, mod_ref, o1_ref, o2_ref, wo1_ref, wo2_ref, g1_ref, b1_ref,
                 wq_ref, k1_ref, k2_ref, x1_ref, h2t_ref, s1_ref, s2_ref):
    mod = mod_ref[0, 0]
    y = _dot(o1_ref[...], wo1_ref[...]) + _dot(o2_ref[...], wo2_ref[...])
    x1 = _layer_norm(ALPHA * x_ref[...] + mod[2:3] * y, g1_ref[...], b1_ref[...])
    x1_ref[...] = x1
    h2f = x1 * (1.0 + mod[4:5]) + mod[3:4]
    h2 = h2f.astype(BF16)
    h2t_ref[...] = h2f.T.astype(BF16)
    q = _dot(h2, wq_ref[...]).astype(BF16)
    half = P_DKEY // 2
    k1 = k1_ref[...]
    k2 = k2_ref[...]
    for hd in range(P_HEADS):
        s1_ref[hd] = _dot_nt(k1, q[:, hd * P_DKEY:hd * P_DKEY + half])
        s2_ref[hd] = _dot_nt(k2, q[:, hd * P_DKEY + half:(hd + 1) * P_DKEY])


def _post(x, mods, l, o1, o2, wo1, wo2, g1, b1, wq, k1, k2):
    tok = lambda w: pl.BlockSpec((TM, w), lambda i: (i, 0))
    sspec = pl.BlockSpec((P_HEADS, P_KEYS, TM), lambda i: (0, 0, i))
    return pl.pallas_call(
        _post_kernel,
        grid=(N_TILES,),
        in_specs=[tok(D),
                  pl.BlockSpec((1, 1, 6, D), lambda i: (l, _mod_row(i), 0, 0)),
                  tok(512), tok(512), _full((512, D)), _full((512, D)), _full((1, D)), _full((1, D)),
                  _full((D, P_HEADS * P_DKEY)), _full((P_KEYS, P_DKEY // 2)), _full((P_KEYS, P_DKEY // 2))],
        out_specs=[tok(D), pl.BlockSpec((D, TM), lambda i: (0, i)), sspec, sspec],
        out_shape=[jax.ShapeDtypeStruct((T_ALL, D), F32),
                   jax.ShapeDtypeStruct((D, T_ALL), BF16),
                   jax.ShapeDtypeStruct((P_HEADS, P_KEYS, T_ALL), F32),
                   jax.ShapeDtypeStruct((P_HEADS, P_KEYS, T_ALL), F32)],
        compiler_params=_params(("arbitrary",)),
        name="post_mixer",
    )(x, mods, o1, o2, wo1, wo2, g1, b1, wq, k1, k2)


def _tables_kernel(u_ref, v_ref, ub_ref, vt_ref):
    ub_ref[0] = u_ref[0].astype(BF16)
    vt_ref[0] = v_ref[0].T.astype(BF16)


def _prep_tables(pk_u, pk_v):
    blk = pl.BlockSpec((1, PEER_EB, D), lambda l, e: (l, e, 0))
    return pl.pallas_call(
        _tables_kernel,
        grid=(DEPTH, P_EXPERTS // PEER_EB),
        in_specs=[blk, blk],
        out_specs=[blk, pl.BlockSpec((1, D, PEER_EB), lambda l, e: (l, 0, e))],
        out_shape=[jax.ShapeDtypeStruct((DEPTH, P_EXPERTS, D), BF16),
                   jax.ShapeDtypeStruct((DEPTH, D, P_EXPERTS), BF16)],
        compiler_params=_params(("arbitrary", "arbitrary")),
        name="expert_tables",
    )(pk_u, pk_v)


def _sorting_network(n):
    pairs = []
    p = 1
    while p < n:
        k = p
        while k >= 1:
            for j in range(k % p, n - k, 2 * k):
                for i in range(min(k, n - j - k)):
                    if (i + j) // (2 * p) == (i + j + k) // (2 * p):
                        pairs.append((i + j, i + j + k))
            k //= 2
        p *= 2
    return pairs


def _top_values(x, n):
    depth = x.shape[0] // 8
    groups = [x[8 * k:8 * k + 8, :] for k in range(depth)]
    size = 1
    while size < depth:
        size *= 2
    bottom = jnp.full_like(groups[0], NEG_INF)
    groups = groups + [bottom] * (size - depth)
    for i, j in _sorting_network(size):
        a, b = groups[i], groups[j]
        groups[i], groups[j] = jnp.maximum(a, b), jnp.minimum(a, b)
    groups = groups[:depth]
    vals = []
    for t in range(n):
        head = groups[0]
        m = jnp.max(head, axis=0, keepdims=True)
        vals.append(m)
        taken = head == m
        for k in range(min(n - 1 - t, depth)):
            groups[k] = jnp.where(taken, groups[k + 1] if k + 1 < depth else bottom, groups[k])
    return vals


PEER_PAIRS = [(a, b) for a in range(PEER_NSEL) for b in range(PEER_NSEL) if (a + 1) * (b + 1) <= PEER_NSEL]
PEER_NCAND = -(-len(PEER_PAIRS) // 8) * 8


def _peer_select(s1_ref, s2_ref, eth_scr, e1_scr, e2_scr, cand_scr):
    n_pad = PEER_NCAND - len(PEER_PAIRS)
    cand_scr[len(PEER_PAIRS):, :] = jnp.full((n_pad, cand_scr.shape[1]), NEG_INF, F32)

    def body(hd, carry):
        s1 = s1_ref[hd]
        s2 = s2_ref[hd]
        v1 = _top_values(s1, PEER_NSEL)
        v2 = _top_values(s2, PEER_NSEL)
        for k, (a, b) in enumerate(PEER_PAIRS):
            cand_scr[k:k + 1, :] = v1[a] + v2[b]
        cs = _top_values(cand_scr[...], PEER_NSEL)
        tau = 0.5 * (cs[P_TOPK - 1] + cs[P_TOPK])
        z = jnp.exp(cs[0] - cs[0])
        for j in range(1, P_TOPK):
            z = z + jnp.exp(cs[j] - cs[0])
        half_inv_z = 0.5 / z
        eth_scr[hd] = jnp.exp((tau - s1) - v2[0]) * half_inv_z
        e1_scr[hd] = jnp.exp(s1 - v1[0])
        e2_scr[hd] = jnp.exp(s2 - v2[0]) * half_inv_z
        return carry

    lax.fori_loop(0, P_HEADS, body, 0)


def _gelu2(a):
    return a * (1.0 + lax.erf(a * (2.0 ** -0.5)))


PEER_LW = 256
PEER_CH = 256


def _peer_kernel(h2t_ref, s1_ref, s2_ref, u_ref, vt_prev_ref, vt_last_ref, x1_ref, mod_ref, g2_ref, b2_ref,
                 o_ref, eth_scr, e1_scr, e2_scr, cand_scr, c_cur, c_prev, acc_scr):
    eb = pl.program_id(1)
    n_eb = pl.num_programs(1)
    rows_per_step = PEER_EB // P_KEYS

    @pl.when(eb == 0)
    def _():
        _peer_select(s1_ref, s2_ref, eth_scr, e1_scr, e2_scr, cand_scr)
        acc_scr[...] = jnp.zeros_like(acc_scr)
        c_prev[...] = jnp.zeros_like(c_prev)

    @pl.when(eb > 0)
    def _():
        c_prev[...] = c_cur[...]

    h2t = h2t_ref[...]
    n_ch = PEER_EB // PEER_CH
    rb = D // n_ch
    act_next = _dot(u_ref[0:PEER_CH, :], h2t)
    for k in range(n_ch):
        act = act_next
        if k + 1 < n_ch:
            act_next = _dot(u_ref[(k + 1) * PEER_CH:(k + 2) * PEER_CH, :], h2t)
        acc_scr[k * rb:(k + 1) * rb, :] += _dot(vt_prev_ref[k * rb:(k + 1) * rb, :], c_prev[...])
        for jj in range(PEER_CH // P_KEYS):
            j = k * (PEER_CH // P_KEYS) + jj
            i1 = eb * rows_per_step + j
            for lc in range(PEER_TT // PEER_LW):
                ls = slice(lc * PEER_LW, (lc + 1) * PEER_LW)
                w = jnp.zeros((P_KEYS, PEER_LW), F32)
                for hd in range(P_HEADS):
                    eth = eth_scr[hd, pl.ds(i1, 1), ls]
                    e1 = e1_scr[hd, pl.ds(i1, 1), ls]
                    e2 = e2_scr[hd, :, ls]
                    w = w + jnp.where(e2 >= eth, e2, 0.0) * e1
                a = act[jj * P_KEYS:(jj + 1) * P_KEYS, ls]
                c_cur[j * P_KEYS:(j + 1) * P_KEYS, ls] = (w * _gelu2(a)).astype(BF16)

    @pl.when(eb == n_eb - 1)
    def _():
        mod = mod_ref[0, 0]
        peer = (acc_scr[...] + _dot(vt_last_ref[...], c_cur[...])).T
        o_ref[...] = _layer_norm(ALPHA * x1_ref[...] + mod[5:6] * peer, g2_ref[...], b2_ref[...])


def _peer(h2t, s1, s2, u_bf, vt_bf, x1, mods, l, g2, b2):
    tiles_ctx = T_CTX // PEER_TT
    tiles_per_lat = LAT_S // PEER_TT
    n_eb = P_EXPERTS // PEER_EB
    mod_row = lambda i: jnp.where(i < tiles_ctx, 0, 1 + (i - tiles_ctx) // tiles_per_lat)
    tok = lambda w: pl.BlockSpec((PEER_TT, w), lambda i, e: (i, 0))
    sspec = pl.BlockSpec((P_HEADS, P_KEYS, PEER_TT), lambda i, e: (0, 0, i))
    sel = pltpu.VMEM((P_HEADS, P_KEYS, PEER_TT), F32)
    return pl.pallas_call(
        _peer_kernel,
        grid=(T_ALL // PEER_TT, n_eb),
        in_specs=[pl.BlockSpec((D, PEER_TT), lambda i, e: (0, i)), sspec, sspec,
                  pl.BlockSpec((None, PEER_EB, D), lambda i, e: (l, e, 0)),
                  pl.BlockSpec((None, D, PEER_EB), lambda i, e: (l, 0, jnp.maximum(e - 1, 0))),
                  pl.BlockSpec((None, D, PEER_EB), lambda i, e: (l, 0, jnp.where(e == n_eb - 1, n_eb - 1, 0))),
                  tok(D),
                  pl.BlockSpec((1, 1, 6, D), lambda i, e: (l, mod_row(i), 0, 0)),
                  pl.BlockSpec((1, D), lambda i, e: (0, 0)),
                  pl.BlockSpec((1, D), lambda i, e: (0, 0))],
        out_specs=tok(D),
        out_shape=jax.ShapeDtypeStruct((T_ALL, D), F32),
        scratch_shapes=[sel, sel, sel,
                        pltpu.VMEM((PEER_NCAND, PEER_TT), F32),
                        pltpu.VMEM((PEER_EB, PEER_TT), BF16),
                        pltpu.VMEM((PEER_EB, PEER_TT), BF16),
                        pltpu.VMEM((D, PEER_TT), F32)],
        compiler_params=_params(("arbitrary", "arbitrary")),
        name="peer",
    )(h2t, s1, s2, u_bf, vt_bf, vt_bf, x1, mods, g2, b2)


def _grid_angles(rot_dim):
    t = jnp.arange(LAT_S)
    row = (t // GRID_W).astype(F32)
    col = (t % GRID_W).astype(F32)
    n_freq = rot_dim // 4
    inv = THETA ** (-jnp.arange(n_freq, dtype=F32) / n_freq)
    return jnp.concatenate([row[:, None] * inv, col[:, None] * inv], axis=-1)


def _rope_tables(rot_dim, width):
    ang = _grid_angles(rot_dim)
    cos = jnp.cos(ang)
    sin = jnp.sin(ang)
    cos_t = jnp.tile(jnp.concatenate([cos, cos], axis=-1), (1, width // rot_dim))
    sin_t = jnp.tile(jnp.concatenate([-sin, sin], axis=-1), (1, width // rot_dim))
    ident_c = jnp.ones((1, TM, width), F32)
    ident_s = jnp.zeros((1, TM, width), F32)
    cos_t = jnp.concatenate([ident_c, cos_t.reshape(TILES_PER_LAT, TM, width)], axis=0)
    sin_t = jnp.concatenate([ident_s, sin_t.reshape(TILES_PER_LAT, TM, width)], axis=0)
    return cos_t, sin_t


def kernel(x_prompt, x_sample, cache_a_k, cache_a_v, cache_b_ckv, cache_b_kr, cache_c_k, cache_c_v, cache_d_k, cache_d_v, c, c_ctx, w_mod, b_mod, ln1_g, ln1_b, ln2_g, ln2_b, ev_w_in, ev_lam_q1, ev_lam_k1, ev_lam_q2, ev_lam_k2, ev_g_sub, ev_g_cq, ev_w_uq, ev_g_ckv, ev_w_ukv, ev_w_out, od_w_in, od_g_qn, od_g_kn, od_rpb, od_w_out, pk_w_q, pk_k1, pk_k2, pk_u, pk_v):
    x = jnp.concatenate([x_prompt.reshape(T_CTX, D), x_sample.reshape(T_LAT, D)], axis=0)
    cvecs = jnp.concatenate([c_ctx[None, :], c, jnp.zeros((8 - 1 - N_LAT_B, D), F32)], axis=0)
    mods = _modulation(cvecs, w_mod, b_mod).reshape(DEPTH, 8, 6, D)
    u_bf, vt_bf = _prep_tables(pk_u, pk_v)

    cos_a, sin_a = _rope_tables(D_A, 512)
    cos_b, sin_b = _rope_tables(B_ROPE, 256)
    rt_even = jnp.concatenate([cos_a, sin_a, cos_b, sin_b], axis=-1)
    rt_odd = jnp.concatenate(_rope_tables(D_C, 512), axis=-1)
    ones_bd = jnp.kron(jnp.eye(512 // D_C, dtype=F32), jnp.ones((D_C, D_C), F32)).astype(BF16)
    lat3 = lambda a, w: a.reshape(N_LAT_B, PAST, w)

    states = {}
    for l in range(DEPTH):
        j = l // 2
        if l % 2 == 0:
            lam_init = 0.8 - 0.6 * math.exp(-0.3 * l)
            w_in = jnp.pad(ev_w_in[j], ((0, 0), (0, EVEN_IN_PAD - ev_w_in.shape[-1]))).astype(BF16)
            wuq = ev_w_uq[j].reshape(B_QRANK, H_B, B_NOPE + B_ROPE)
            wuq = jnp.concatenate([wuq[:, :, :B_NOPE].reshape(B_QRANK, -1),
                                   wuq[:, :, B_NOPE:].reshape(B_QRANK, -1)], axis=1).astype(BF16)
            wukv = ev_w_ukv[j].reshape(B_KVRANK, H_B, B_NOPE + B_V)
            wukv = jnp.concatenate([wukv[:, :, :B_NOPE].reshape(B_KVRANK, -1),
                                    wukv[:, :, B_NOPE:].reshape(B_KVRANK, -1)], axis=1).astype(BF16)
            (qa, bq, ka, va, kvb, kr,
             states["a_k"], states["a_v"], states["b_ckv"], states["b_kr"]) = _proj_call(
                _proj_even_kernel, "proj_even", x, mods, l, rt_even,
                [w_in, ev_g_cq[j][None, :], wuq, ev_g_ckv[j][None, :], wukv],
                (512, 768, 512, 512, 1024, B_ROPE), (512, 512, B_KVRANK, B_ROPE))
            lam_vecs = [v[j][None, :] for v in (ev_lam_q1, ev_lam_k1, ev_lam_q2, ev_lam_k2)]
            g_sub = ev_g_sub[j][None, :]
            proj = (qa, bq, ka, va, kvb, kr)
            outs = _attn_even(lam_vecs, g_sub, lam_init, *proj)
            cache_kvb = _matmul(cache_b_ckv[:, j].reshape(N_LAT_B * PAST, B_KVRANK), wukv)
            cache = (lat3(cache_a_k[:, j], 512), lat3(cache_a_v[:, j], 512), cache_kvb,
                     lat3(cache_b_kr[:, j], B_ROPE))
            o1, o2 = _attn_even(lam_vecs, g_sub, lam_init, *proj, cache=cache, prev_out=outs)
            w_out = ev_w_out[j]
        else:
            w_in = od_w_in[j].astype(BF16)
            g_qn = jnp.tile(od_g_qn[j], H_C)[None, :]
            g_kn = jnp.tile(od_g_kn[j], G_C)[None, :]
            (cq, ck, cv, dq, dk, dv,
             states["c_k"], states["c_v"], states["d_k"], states["d_v"]) = _proj_call(
                _proj_odd_kernel, "proj_odd", x, mods, l, rt_odd, [w_in, g_qn, g_kn, ones_bd],
                (512, 128, 128, 512, 512, 512), (128, 128, 512, 512))
            o1 = _gqa(cq, ck, cv)
            o1 = _gqa(cq, ck, cv, cache=(lat3(cache_c_k[:, j], 128), lat3(cache_c_v[:, j], 128)),
                      prev_out=[o1])
            o2 = _mha(dq, dk, dv)
            bias = _na_bias(od_rpb[j])
            offs = jnp.arange(NA_ROWS + 1)[:, None] + jnp.arange(NA_ROWS)[None, :]
            bias_win = jnp.transpose(bias[:, offs], (1, 0, 3, 2, 4)).reshape(
                NA_ROWS + 1, H_D, GRID_W, NA_ROWS * GRID_W)
            o2 = _neighbourhood(dq, dk, dv, lat3(cache_d_k[:, j], 512), lat3(cache_d_v[:, j], 512),
                                bias_win, o2)
            w_out = od_w_out[j]

        x1, h2t, s1, s2 = _post(x, mods, l, o1, o2, w_out[:512].astype(BF16), w_out[512:].astype(BF16),
                                ln1_g[l][None, :], ln1_b[l][None, :], pk_w_q[l].astype(BF16),
                                pk_k1[l].astype(BF16), pk_k2[l].astype(BF16))
        x = _peer(h2t, s1, s2, u_bf, vt_bf, x1, mods, l, ln2_g[l][None, :], ln2_b[l][None, :])

    y = x[:T_CTX].reshape(N_CTX_B, CTX_S, D)
    z = x[T_CTX:].reshape(N_LAT_B, LAT_S, D)
    n_even = (DEPTH + 1) // 2
    n_odd = DEPTH // 2
    return (y, z,
            states["a_k"].reshape(N_CTX_B, n_even, CTX_S, H_A, 2, D_A),
            states["a_v"].reshape(N_CTX_B, n_even, CTX_S, H_A, 2 * D_A),
            states["b_ckv"].reshape(N_CTX_B, n_even, CTX_S, B_KVRANK),
            states["b_kr"].reshape(N_CTX_B, n_even, CTX_S, B_ROPE),
            states["c_k"].reshape(N_CTX_B, n_odd, CTX_S, G_C, D_C),
            states["c_v"].reshape(N_CTX_B, n_odd, CTX_S, G_C, D_C),
            states["d_k"].reshape(N_CTX_B, n_odd, CTX_S, H_D, D_D),
            states["d_v"].reshape(N_CTX_B, n_odd, CTX_S, H_D, D_D))
```

```python
import functools
import math

import jax
import jax.numpy as jnp
from jax import lax
from jax.experimental import pallas as pl
from jax.experimental.pallas import tpu as pltpu

F32 = jnp.float32
BF16 = jnp.bfloat16

D = 1024
N_CTX_B = 32
CTX_S = 256
N_LAT_B = 4
LAT_S = 1024
PAST = 256
T_CTX = N_CTX_B * CTX_S
T_LAT = N_LAT_B * LAT_S
T_ALL = T_CTX + T_LAT
DEPTH = 2
GRID_W = 64
GRID_ROWS = LAT_S // GRID_W
THETA = 10000.0
EPS = 1e-6
ALPHA = (2 * DEPTH) ** 0.25

H_A, D_A = 4, 64
H_B, B_NOPE, B_ROPE, B_V, B_QRANK, B_KVRANK = 8, 64, 32, 64, 256, 128
H_C, G_C, D_C = 8, 2, 64
H_D, D_D = 8, 64
NA_ROWS, NA_COLS = 8, 16
P_HEADS, P_KEYS, P_DKEY, P_TOPK = 8, 128, 256, 16
P_EXPERTS = P_KEYS * P_KEYS

EVEN_IN_PAD = 2048
ODD_IN = 2304

TM = 512
N_TILES = T_ALL // TM
N_CTX_TILES = T_CTX // TM
TILES_PER_LAT = LAT_S // TM
N_LAT_TILES = T_LAT // TM
TQ = 256
PEER_TT = 512
PEER_EB = 1024
PEER_NSEL = P_TOPK + 1
VMEM_LIMIT = 56 * 1024 * 1024

NEG_INF = float("-inf")


def _dot(a, b):
    return jnp.dot(a, b, preferred_element_type=F32)


def _dot_nt(a, b):
    return lax.dot_general(a, b, (((1,), (1,)), ((), ())), preferred_element_type=F32)


def _layer_norm(x, g, b):
    mu = jnp.mean(x, axis=-1, keepdims=True)
    xc = x - mu
    var = jnp.mean(xc * xc, axis=-1, keepdims=True)
    return xc * lax.rsqrt(var + EPS) * g + b


def _rms_norm(x, g):
    return x * lax.rsqrt(jnp.mean(x * x, axis=-1, keepdims=True) + EPS) * g


def _chunk_rms_norm(x, g, ones_bd, chunk):
    x2 = x * x
    hi = x2.astype(BF16)
    lo = (x2 - hi.astype(F32)).astype(BF16)
    ms = (_dot(hi, ones_bd) + _dot(lo, ones_bd)) * (1.0 / chunk)
    return x * lax.rsqrt(ms + EPS) * g


def _rope(x, cos_t, sin_t, chunk):
    n = x.shape[-1]
    half = chunk // 2
    lane = lax.broadcasted_iota(jnp.int32, x.shape, x.ndim - 1)
    swapped = jnp.where((lane % chunk) < half,
                        pltpu.roll(x, n - half, x.ndim - 1),
                        pltpu.roll(x, half, x.ndim - 1))
    return x * cos_t + swapped * sin_t


def _params(sem):
    return pltpu.CompilerParams(dimension_semantics=sem, vmem_limit_bytes=VMEM_LIMIT)


def _full(shape):
    zeros = (0,) * len(shape)
    return pl.BlockSpec(shape, lambda *_: zeros)


def _mod_row(i):
    return jnp.where(i < N_CTX_TILES, 0, 1 + (i - N_CTX_TILES) // TILES_PER_LAT)


def _rope_blk(i):
    return jnp.where(i < N_CTX_TILES, 0, 1 + (i - N_CTX_TILES) % TILES_PER_LAT)


def _mod_kernel(c_ref, w_ref, b_ref, o_ref):
    c = c_ref[...]
    a = c * (1.0 / (1.0 + jnp.exp(-c)))
    o_ref[0] = _dot(a.astype(BF16), w_ref[0].astype(BF16)) + b_ref[0]


def _modulation(cvecs, w_mod, b_mod):
    nb = 6
    bn = 6 * D // nb
    return pl.pallas_call(
        _mod_kernel,
        grid=(DEPTH, nb),
        in_specs=[pl.BlockSpec((8, D), lambda l, j: (0, 0)),
                  pl.BlockSpec((1, D, bn), lambda l, j: (l, 0, j)),
                  pl.BlockSpec((1, 1, bn), lambda l, j: (l, 0, j))],
        out_specs=pl.BlockSpec((1, 8, bn), lambda l, j: (l, 0, j)),
        out_shape=jax.ShapeDtypeStruct((DEPTH, 8, 6 * D), F32),
        compiler_params=_params(("arbitrary", "arbitrary")),
        name="modulation",
    )(cvecs, w_mod, b_mod.reshape(DEPTH, 1, 6 * D))


def _tile(g):
    return (g + N_CTX_TILES) % N_TILES


def _state_blk(g):
    return jnp.maximum(g - N_LAT_TILES, 0)


def _proj_even_kernel(x_ref, mod_ref, rt_ref, win_ref, gcq_ref, wuq_ref, gckv_ref, wukv_ref,
                      qa_ref, bq_ref, ka_ref, va_ref, kvb_ref, kr_ref,
                      s_ak_ref, s_av_ref, s_ckv_ref, s_kr_ref):
    x = x_ref[...]
    mod = mod_ref[0, 0]
    h = x * (1.0 + mod[1:2]) + mod[0:1]
    p = _dot(h.astype(BF16), win_ref[...])
    rt = rt_ref[0]
    cos_a, sin_a = rt[:, 0:512], rt[:, 512:1024]
    cos_b, sin_b = rt[:, 1024:1280], rt[:, 1280:1536]
    qa_ref[...] = _rope(p[:, 0:512], cos_a, sin_a, D_A).astype(BF16)
    ka = _rope(p[:, 512:1024], cos_a, sin_a, D_A)
    ka_ref[...] = ka.astype(BF16)
    s_ak_ref[...] = ka
    va = p[:, 1024:1536]
    va_ref[...] = va.astype(BF16)
    s_av_ref[...] = va
    cq = _rms_norm(p[:, 1536:1792], gcq_ref[...])
    bq = _dot(cq.astype(BF16), wuq_ref[...])
    bq_ref[:, 0:512] = bq[:, 0:512].astype(BF16)
    bq_ref[:, 512:768] = _rope(bq[:, 512:768], cos_b, sin_b, B_ROPE).astype(BF16)
    ckv = _rms_norm(p[:, 1792:1920], gckv_ref[...])
    s_ckv_ref[...] = ckv
    kvb_ref[...] = _dot(ckv.astype(BF16), wukv_ref[...]).astype(BF16)
    kr = _rope(p[:, 1920:2048], cos_b[:, 0:128], sin_b[:, 0:128], B_ROPE)[:, 0:B_ROPE]
    kr_ref[...] = kr.astype(BF16)
    s_kr_ref[...] = kr


def _proj_call(kernel_fn, name, x, mods, l, rt, weights, widths, state_widths):
    tok = lambda w: pl.BlockSpec((TM, w), lambda g: (_tile(g), 0))
    st = lambda w: pl.BlockSpec((TM, w), lambda g: (_state_blk(g), 0))
    return pl.pallas_call(
        kernel_fn,
        grid=(N_TILES,),
        in_specs=[tok(D),
                  pl.BlockSpec((1, 1, 6, D), lambda g: (l, _mod_row(_tile(g)), 0, 0)),
                  pl.BlockSpec((1, TM, rt.shape[-1]), lambda g: (_rope_blk(_tile(g)), 0, 0))]
                 + [_full(w.shape) for w in weights],
        out_specs=[tok(w) for w in widths] + [st(w) for w in state_widths],
        out_shape=[jax.ShapeDtypeStruct((T_ALL, w), BF16) for w in widths]
                  + [jax.ShapeDtypeStruct((T_CTX, w), F32) for w in state_widths],
        compiler_params=_params(("arbitrary",)),
        name=name,
    )(x, mods, rt, *weights)


def _proj_odd_kernel(x_ref, mod_ref, rt_ref, win_ref, gq_ref, gk_ref, bd_ref,
                     cq_ref, ck_ref, cv_ref, dq_ref, dk_ref, dv_ref,
                     s_ck_ref, s_cv_ref, s_dk_ref, s_dv_ref):
    x = x_ref[...]
    mod = mod_ref[0, 0]
    h = x * (1.0 + mod[1:2]) + mod[0:1]
    p = _dot(h.astype(BF16), win_ref[...])
    rt = rt_ref[0]
    cos_c, sin_c = rt[:, 0:512], rt[:, 512:1024]
    bd = bd_ref[...]
    cq = _chunk_rms_norm(p[:, 0:512], gq_ref[...], bd, D_C)
    cq_ref[...] = _rope(cq, cos_c, sin_c, D_C).astype(BF16)
    ck = _chunk_rms_norm(p[:, 512:640], gk_ref[...], bd[0:128, 0:128], D_C)
    ck = _rope(ck, cos_c[:, 0:128], sin_c[:, 0:128], D_C)
    ck_ref[...] = ck.astype(BF16)
    s_ck_ref[...] = ck
    cv = p[:, 640:768]
    cv_ref[...] = cv.astype(BF16)
    s_cv_ref[...] = cv
    dq_ref[...] = p[:, 768:1280].astype(BF16)
    dk = p[:, 1280:1792]
    dk_ref[...] = dk.astype(BF16)
    s_dk_ref[...] = dk
    dv = p[:, 1792:2304]
    dv_ref[...] = dv.astype(BF16)
    s_dv_ref[...] = dv


def _matmul_kernel(x_ref, w_ref, o_ref):
    o_ref[...] = _dot(x_ref[...].astype(BF16), w_ref[...]).astype(BF16)


def _matmul(x, w):
    m, k = x.shape
    n = w.shape[1]
    return pl.pallas_call(
        _matmul_kernel,
        grid=(m // TM,),
        in_specs=[pl.BlockSpec((TM, k), lambda i: (i, 0)), _full((k, n))],
        out_specs=pl.BlockSpec((TM, n), lambda i: (i, 0)),
        out_shape=jax.ShapeDtypeStruct((m, n), BF16),
        compiler_params=_params(("arbitrary",)),
        name="cache_kv_up",
    )(x, w)


def _exp_segments(segs):
    m = jnp.max(segs[0], axis=-1, keepdims=True)
    for s in segs[1:]:
        m = jnp.maximum(m, jnp.max(s, axis=-1, keepdims=True))
    es = [jnp.exp(s - m) for s in segs]
    den = jnp.sum(es[0], axis=-1, keepdims=True)
    for e in es[1:]:
        den = den + jnp.sum(e, axis=-1, keepdims=True)
    return es, den


def _pv(ps, vs):
    o = _dot(ps[0].astype(BF16), vs[0])
    for p, v in zip(ps[1:], vs[1:]):
        o = o + _dot(p.astype(BF16), v)
    return o


def _attend_group(seg_lists, vs_lists):
    exps = [_exp_segments(segs) for segs in seg_lists]
    return [_pv(es, vs) / den for (es, den), vs in zip(exps, vs_lists)]


def _scaled(q, scale):
    return (q.astype(F32) * scale).astype(q.dtype)


def _attn_specs(row0, n_req, n_q, widths_q, widths_k, cache_shapes):
    n_own = n_q
    nqb = n_q // TQ
    q0 = row0 // TQ
    k0 = row0 // n_own
    qspec = lambda w: pl.BlockSpec((TQ, w), lambda b, j: (q0 + b * nqb + j, 0))
    kspec = lambda w: pl.BlockSpec((n_own, w), lambda b, j: (k0 + b, 0))
    cspec = lambda shp: pl.BlockSpec((1,) + tuple(shp[1:]), lambda b, j: (b,) + (0,) * (len(shp) - 1))
    specs = [qspec(w) for w in widths_q] + [kspec(w) for w in widths_k] + [cspec(s) for s in cache_shapes]
    return specs, qspec, (n_req, nqb)


def _attn_even_kernel(*refs, lam_init, has_cache):
    lq1_ref, lk1_ref, lq2_ref, lk2_ref, gsub_ref, qa_ref, bq_ref, ka_ref, va_ref, kvb_ref, kr_ref = refs[:11]
    if has_cache:
        cka_ref, cva_ref, ckvb_ref, ckr_ref = refs[11:15]
    oa_ref, ob_ref = refs[-2:]
    lam = (jnp.exp(jnp.sum(lq1_ref[...] * lk1_ref[...], axis=-1, keepdims=True))
           - jnp.exp(jnp.sum(lq2_ref[...] * lk2_ref[...], axis=-1, keepdims=True)) + lam_init)
    gsub = gsub_ref[...]
    grp_a, grp_b = (1, 2) if has_cache else (H_A, H_B)
    for h0 in range(0, H_A, grp_a):
        seg_lists, vs_lists = [], []
        for hd in range(h0, h0 + grp_a):
            vsl = slice(hd * 2 * D_A, (hd + 1) * 2 * D_A)
            vs = [va_ref[:, vsl]] + ([cva_ref[0, :, vsl].astype(BF16)] if has_cache else [])
            for m in range(2):
                sl = slice(hd * 2 * D_A + m * D_A, hd * 2 * D_A + (m + 1) * D_A)
                q = _scaled(qa_ref[:, sl], D_A ** -0.5)
                segs = [_dot_nt(q, ka_ref[:, sl])]
                if has_cache:
                    segs.append(_dot_nt(q, cka_ref[0, :, sl].astype(BF16)))
                seg_lists.append(segs)
                vs_lists.append(vs)
        os = _attend_group(seg_lists, vs_lists)
        for i, hd in enumerate(range(h0, h0 + grp_a)):
            vsl = slice(hd * 2 * D_A, (hd + 1) * 2 * D_A)
            o = os[2 * i] - lam * os[2 * i + 1]
            oa_ref[:, vsl] = (_rms_norm(o, gsub) * (1.0 - lam_init)).astype(BF16)
    kr = kr_ref[...]
    if has_cache:
        ckr = ckr_ref[0].astype(BF16)
    sc = (B_NOPE + B_ROPE) ** -0.5
    for h0 in range(0, H_B, grp_b):
        seg_lists, vs_lists = [], []
        for hd in range(h0, h0 + grp_b):
            nsl = slice(hd * B_NOPE, (hd + 1) * B_NOPE)
            vsl = slice(512 + hd * B_V, 512 + (hd + 1) * B_V)
            qn = bq_ref[:, nsl]
            qr = bq_ref[:, 512 + hd * B_ROPE:512 + (hd + 1) * B_ROPE]
            segs = [(_dot_nt(qn, kvb_ref[:, nsl]) + _dot_nt(qr, kr)) * sc]
            vs = [kvb_ref[:, vsl]]
            if has_cache:
                segs.append((_dot_nt(qn, ckvb_ref[0, :, nsl]) + _dot_nt(qr, ckr)) * sc)
                vs.append(ckvb_ref[0, :, vsl])
            seg_lists.append(segs)
            vs_lists.append(vs)
        for hd, o in zip(range(h0, h0 + grp_b), _attend_group(seg_lists, vs_lists)):
            ob_ref[:, hd * B_V:(hd + 1) * B_V] = o.astype(BF16)


def _attn_call(kernel_fn, name, params, arrays, widths_q, widths_k, latent, cache, prev_out, n_out):
    row0, n_req, n_q = (T_CTX, N_LAT_B, LAT_S) if latent else (0, N_CTX_B, CTX_S)
    cache = list(cache)
    cache_shapes = []
    for a in cache:
        cache_shapes.append(a.shape if a.ndim == 3 else (n_req,) + (a.shape[0] // n_req, a.shape[1]))
    cache = [a.reshape(shp) for a, shp in zip(cache, cache_shapes)]
    specs, qspec, grid = _attn_specs(row0, n_req, n_q, widths_q, widths_k, cache_shapes)
    in_specs = [_full(p.shape) for p in params] + specs
    operands = list(params) + list(arrays) + cache
    aliases = {}
    if prev_out is not None:
        for i, o in enumerate(prev_out):
            aliases[len(operands)] = i
            operands.append(o)
            in_specs.append(pl.BlockSpec(memory_space=pl.ANY))
    return pl.pallas_call(
        kernel_fn,
        grid=grid,
        in_specs=in_specs,
        out_specs=[qspec(512)] * n_out,
        out_shape=[jax.ShapeDtypeStruct((T_ALL, 512), BF16)] * n_out,
        input_output_aliases=aliases,
        compiler_params=_params(("arbitrary", "arbitrary")),
        name=name,
    )(*operands)


def _attn_even(lam_vecs, g_sub, lam_init, qa, bq, ka, va, kvb, kr, cache=(), prev_out=None):
    latent = prev_out is not None
    fn = functools.partial(_attn_even_kernel, lam_init=lam_init, has_cache=latent)
    return _attn_call(fn, "attn_even", list(lam_vecs) + [g_sub], [qa, bq, ka, va, kvb, kr],
                      (512, 768), (512, 512, 1024, B_ROPE), latent, cache, prev_out, 2)


def _gqa_kernel(*refs, has_cache):
    cq_ref, ck_ref, cv_ref = refs[:3]
    if has_cache:
        cck_ref, ccv_ref = refs[3:5]
    o_ref = refs[-1]
    grp = 2 if has_cache else H_C
    for h0 in range(0, H_C, grp):
        seg_lists, vs_lists = [], []
        for hd in range(h0, h0 + grp):
            g = hd // (H_C // G_C)
            gsl = slice(g * D_C, (g + 1) * D_C)
            q = _scaled(cq_ref[:, hd * D_C:(hd + 1) * D_C], D_C ** -0.5)
            segs = [_dot_nt(q, ck_ref[:, gsl])]
            vs = [cv_ref[:, gsl]]
            if has_cache:
                segs.append(_dot_nt(q, cck_ref[0, :, gsl].astype(BF16)))
                vs.append(ccv_ref[0, :, gsl].astype(BF16))
            seg_lists.append(segs)
            vs_lists.append(vs)
        for hd, o in zip(range(h0, h0 + grp), _attend_group(seg_lists, vs_lists)):
            o_ref[:, hd * D_C:(hd + 1) * D_C] = o.astype(BF16)


def _gqa(cq, ck, cv, cache=(), prev_out=None):
    latent = prev_out is not None
    fn = functools.partial(_gqa_kernel, has_cache=latent)
    return _attn_call(fn, "attn_gqa", [], [cq, ck, cv], (512,), (128, 128), latent, cache, prev_out, 1)[0]


def _mha_kernel(q_ref, k_ref, v_ref, o_ref):
    heads = [slice(hd * D_D, (hd + 1) * D_D) for hd in range(H_D)]
    seg_lists = [[_dot_nt(_scaled(q_ref[:, sl], D_D ** -0.5), k_ref[:, sl])] for sl in heads]
    for sl, o in zip(heads, _attend_group(seg_lists, [[v_ref[:, sl]] for sl in heads])):
        o_ref[:, sl] = o.astype(BF16)


def _mha(q, k, v):
    return _attn_call(_mha_kernel, "attn_mha", [], [q, k, v], (512,), (512, 512), False, (), None, 1)[0]


def _na_bias_kernel(rpb_ref, o_ref):
    hd = pl.program_id(0)
    qc = lax.broadcasted_iota(jnp.int32, (GRID_W, GRID_W), 0)
    kc = lax.broadcasted_iota(jnp.int32, (GRID_W, GRID_W), 1)
    dc = jnp.clip(kc - qc, -(NA_COLS - 1), NA_COLS - 1) + (NA_COLS - 1)
    c0 = jnp.clip(qc - NA_COLS // 2, 0, GRID_W - NA_COLS)
    n_dc = 2 * NA_COLS - 1
    n_dr = 2 * NA_ROWS - 1
    for dr in range(n_dr):
        acc = jnp.zeros((GRID_W, GRID_W), F32)
        for j in range(n_dc):
            acc = jnp.where(dc == j, rpb_ref[(hd * n_dr + dr) * n_dc + j], acc)
        o_ref[0, dr] = jnp.where(kc >= c0, jnp.where(kc < c0 + NA_COLS, acc, NEG_INF), NEG_INF)


def _na_bias(rpb):
    n_dr = 2 * NA_ROWS - 1
    return pl.pallas_call(
        _na_bias_kernel,
        grid=(H_D,),
        in_specs=[pl.BlockSpec(memory_space=pltpu.SMEM)],
        out_specs=pl.BlockSpec((1, n_dr, GRID_W, GRID_W), lambda h: (h, 0, 0, 0)),
        out_shape=jax.ShapeDtypeStruct((H_D, n_dr, GRID_W, GRID_W), F32),
        compiler_params=_params(("arbitrary",)),
        name="na_bias",
    )(rpb.reshape(-1))


def _na_window_start(r):
    return jnp.clip(r - NA_ROWS // 2, 0, GRID_ROWS - NA_ROWS)


def _na_kernel(q_ref, k_ref, v_ref, ck_ref, cv_ref, bias_ref, prev_ref, o_ref):
    r = pl.program_id(1)
    start = pl.multiple_of(_na_window_start(r) * GRID_W, GRID_W)
    n_loc = NA_ROWS * GRID_W
    scale = D_D ** -0.5
    heads = [slice(hd * D_D, (hd + 1) * D_D) for hd in range(H_D)]
    segs = []
    for hd, sl in enumerate(heads):
        q = _scaled(q_ref[:, sl], scale)
        segs.append([_dot_nt(q, k_ref[pl.ds(start, n_loc), sl]) + bias_ref[0, hd],
                     _dot_nt(q, ck_ref[0, :, sl].astype(BF16))])
    exps = [_exp_segments(s) for s in segs]
    for sl, (es, den) in zip(heads, exps):
        o = _pv(es, [v_ref[pl.ds(start, n_loc), sl], cv_ref[0, :, sl].astype(BF16)]) / den
        o_ref[:, sl] = o.astype(BF16)


def _neighbourhood(q, k, v, ck, cv, bias_win, prev_out):
    n_loc = NA_ROWS * GRID_W
    q0 = T_CTX // GRID_W
    k0 = T_CTX // LAT_S
    qspec = pl.BlockSpec((GRID_W, 512), lambda b, r: (q0 + b * GRID_ROWS + r, 0))
    kspec = pl.BlockSpec((LAT_S, 512), lambda b, r: (k0 + b, 0))
    cspec = pl.BlockSpec((1, PAST, 512), lambda b, r: (b, 0, 0))
    return pl.pallas_call(
        _na_kernel,
        grid=(N_LAT_B, GRID_ROWS),
        in_specs=[qspec, kspec, kspec, cspec, cspec,
                  pl.BlockSpec((1, H_D, GRID_W, n_loc),
                               lambda b, r: (_na_window_start(r) - r + NA_ROWS - 1, 0, 0, 0)),
                  pl.BlockSpec(memory_space=pl.ANY)],
        out_specs=qspec,
        out_shape=jax.ShapeDtypeStruct((T_ALL, 512), BF16),
        input_output_aliases={6: 0},
        compiler_params=_params(("arbitrary", "arbitrary")),
        name="attn_neighbourhood",
    )(q, k, v, ck, cv, bias_win, prev_out)


def _post_kernel(x_ref, mod_ref, o1_ref, o2_ref, wo1_ref, wo2_ref, g1_ref, b1_ref,
                 wq_ref, k1_ref, k2_ref, x1_ref, h2t_ref, s1_ref, s2_ref):
    mod = mod_ref[0, 0]
    y = _dot(o1_ref[...], wo1_ref[...]) + _dot(o2_ref[...], wo2_ref[...])
    x1 = _layer_norm(ALPHA * x_ref[...] + mod[2:3] * y, g1_ref[...], b1_ref[...])
    x1_ref[...] = x1
    h2f = x1 * (1.0 + mod[4:5]) + mod[3:4]
    h2 = h2f.astype(BF16)
    h2t_ref[...] = h2f.T.astype(BF16)
    q = _dot(h2, wq_ref[...]).astype(BF16)
    half = P_DKEY // 2
    k1 = k1_ref[...]
    k2 = k2_ref[...]
    for hd in range(P_HEADS):
        s1_ref[hd] = _dot_nt(k1, q[:, hd * P_DKEY:hd * P_DKEY + half])
        s2_ref[hd] = _dot_nt(k2, q[:, hd * P_DKEY + half:(hd + 1) * P_DKEY])


def _post(x, mods, l, o1, o2, wo1, wo2, g1, b1, wq, k1, k2):
    tok = lambda w: pl.BlockSpec((TM, w), lambda i: (i, 0))
    sspec = pl.BlockSpec((P_HEADS, P_KEYS, TM), lambda i: (0, 0, i))
    return pl.pallas_call(
        _post_kernel,
        grid=(N_TILES,),
        in_specs=[tok(D),
                  pl.BlockSpec((1, 1, 6, D), lambda i: (l, _mod_row(i), 0, 0)),
                  tok(512), tok(512), _full((512, D)), _full((512, D)), _full((1, D)), _full((1, D)),
                  _full((D, P_HEADS * P_DKEY)), _full((P_KEYS, P_DKEY // 2)), _full((P_KEYS, P_DKEY // 2))],
        out_specs=[tok(D), pl.BlockSpec((D, TM), lambda i: (0, i)), sspec, sspec],
        out_shape=[jax.ShapeDtypeStruct((T_ALL, D), F32),
                   jax.ShapeDtypeStruct((D, T_ALL), BF16),
                   jax.ShapeDtypeStruct((P_HEADS, P_KEYS, T_ALL), F32),
                   jax.ShapeDtypeStruct((P_HEADS, P_KEYS, T_ALL), F32)],
        compiler_params=_params(("arbitrary",)),
        name="post_mixer",
    )(x, mods, o1, o2, wo1, wo2, g1, b1, wq, k1, k2)


def _tables_kernel(u_ref, v_ref, ub_ref, vt_ref):
    ub_ref[0] = u_ref[0].astype(BF16)
    vt_ref[0] = v_ref[0].T.astype(BF16)


def _prep_tables(pk_u, pk_v):
    blk = pl.BlockSpec((1, PEER_EB, D), lambda l, e: (l, e, 0))
    return pl.pallas_call(
        _tables_kernel,
        grid=(DEPTH, P_EXPERTS // PEER_EB),
        in_specs=[blk, blk],
        out_specs=[blk, pl.BlockSpec((1, D, PEER_EB), lambda l, e: (l, 0, e))],
        out_shape=[jax.ShapeDtypeStruct((DEPTH, P_EXPERTS, D), BF16),
                   jax.ShapeDtypeStruct((DEPTH, D, P_EXPERTS), BF16)],
        compiler_params=_params(("arbitrary", "arbitrary")),
        name="expert_tables",
    )(pk_u, pk_v)


def _sorting_network(n):
    pairs = []
    p = 1
    while p < n:
        k = p
        while k >= 1:
            for j in range(k % p, n - k, 2 * k):
                for i in range(min(k, n - j - k)):
                    if (i + j) // (2 * p) == (i + j + k) // (2 * p):
                        pairs.append((i + j, i + j + k))
            k //= 2
        p *= 2
    return pairs


def _top_values(x, n):
    depth = x.shape[0] // 8
    groups = [x[8 * k:8 * k + 8, :] for k in range(depth)]
    size = 1
    while size < depth:
        size *= 2
    bottom = jnp.full_like(groups[0], NEG_INF)
    groups = groups + [bottom] * (size - depth)
    for i, j in _sorting_network(size):
        a, b = groups[i], groups[j]
        groups[i], groups[j] = jnp.maximum(a, b), jnp.minimum(a, b)
    groups = groups[:depth]
    vals = []
    for t in range(n):
        head = groups[0]
        m = jnp.max(head, axis=0, keepdims=True)
        vals.append(m)
        taken = head == m
        for k in range(min(n - 1 - t, depth)):
            groups[k] = jnp.where(taken, groups[k + 1] if k + 1 < depth else bottom, groups[k])
    return vals


PEER_PAIRS = [(a, b) for a in range(PEER_NSEL) for b in range(PEER_NSEL) if (a + 1) * (b + 1) <= PEER_NSEL]
PEER_NCAND = -(-len(PEER_PAIRS) // 8) * 8


def _peer_select(s1_ref, s2_ref, eth_scr, e1_scr, e2_scr, cand_scr):
    n_pad = PEER_NCAND - len(PEER_PAIRS)
    cand_scr[len(PEER_PAIRS):, :] = jnp.full((n_pad, cand_scr.shape[1]), NEG_INF, F32)

    def body(hd, carry):
        s1 = s1_ref[hd]
        s2 = s2_ref[hd]
        v1 = _top_values(s1, PEER_NSEL)
        v2 = _top_values(s2, PEER_NSEL)
        for k, (a, b) in enumerate(PEER_PAIRS):
            cand_scr[k:k + 1, :] = v1[a] + v2[b]
        cs = _top_values(cand_scr[...], PEER_NSEL)
        tau = 0.5 * (cs[P_TOPK - 1] + cs[P_TOPK])
        z = jnp.exp(cs[0] - cs[0])
        for j in range(1, P_TOPK):
            z = z + jnp.exp(cs[j] - cs[0])
        half_inv_z = 0.5 / z
        eth_scr[hd] = jnp.exp((tau - s1) - v2[0]) * half_inv_z
        e1_scr[hd] = jnp.exp(s1 - v1[0])
        e2_scr[hd] = jnp.exp(s2 - v2[0]) * half_inv_z
        return carry

    lax.fori_loop(0, P_HEADS, body, 0)


def _gelu2(a):
    return a * (1.0 + lax.erf(a * (2.0 ** -0.5)))


PEER_LW = 256
PEER_CH = 256


def _peer_kernel(h2t_ref, s1_ref, s2_ref, u_ref, vt_prev_ref, vt_last_ref, x1_ref, mod_ref, g2_ref, b2_ref,
                 *refs, split_out):
    o_refs = refs[:2] if split_out else refs[:1]
    eth_scr, e1_scr, e2_scr, cand_scr, c_cur, c_prev, acc_scr = refs[len(o_refs):]
    tile = pl.program_id(0)
    eb = pl.program_id(1)
    n_eb = pl.num_programs(1)
    rows_per_step = PEER_EB // P_KEYS

    @pl.when(eb == 0)
    def _():
        _peer_select(s1_ref, s2_ref, eth_scr, e1_scr, e2_scr, cand_scr)
        acc_scr[...] = jnp.zeros_like(acc_scr)
        c_prev[...] = jnp.zeros_like(c_prev)

    @pl.when(eb > 0)
    def _():
        c_prev[...] = c_cur[...]

    h2t = h2t_ref[...]
    n_ch = PEER_EB // PEER_CH
    rb = D // n_ch
    act_next = _dot(u_ref[0:PEER_CH, :], h2t)
    for k in range(n_ch):
        act = act_next
        if k + 1 < n_ch:
            act_next = _dot(u_ref[(k + 1) * PEER_CH:(k + 2) * PEER_CH, :], h2t)
        acc_scr[k * rb:(k + 1) * rb, :] += _dot(vt_prev_ref[k * rb:(k + 1) * rb, :], c_prev[...])
        for jj in range(PEER_CH // P_KEYS):
            j = k * (PEER_CH // P_KEYS) + jj
            i1 = eb * rows_per_step + j
            for lc in range(PEER_TT // PEER_LW):
                ls = slice(lc * PEER_LW, (lc + 1) * PEER_LW)
                w = jnp.zeros((P_KEYS, PEER_LW), F32)
                for hd in range(P_HEADS):
                    eth = eth_scr[hd, pl.ds(i1, 1), ls]
                    e1 = e1_scr[hd, pl.ds(i1, 1), ls]
                    e2 = e2_scr[hd, :, ls]
                    w = w + jnp.where(e2 >= eth, e2, 0.0) * e1
                a = act[jj * P_KEYS:(jj + 1) * P_KEYS, ls]
                c_cur[j * P_KEYS:(j + 1) * P_KEYS, ls] = (w * _gelu2(a)).astype(BF16)

    @pl.when(eb == n_eb - 1)
    def _():
        mod = mod_ref[0, 0]
        peer = (acc_scr[...] + _dot(vt_last_ref[...], c_cur[...])).T
        res = _layer_norm(ALPHA * x1_ref[...] + mod[5:6] * peer, g2_ref[...], b2_ref[...])
        if split_out:
            @pl.when(tile < T_CTX // PEER_TT)
            def _():
                o_refs[0][...] = res

            @pl.when(tile >= T_CTX // PEER_TT)
            def _():
                o_refs[1][...] = res
        else:
            o_refs[0][...] = res


def _peer(h2t, s1, s2, u_bf, vt_bf, x1, mods, l, g2, b2, split_out):
    tiles_ctx = T_CTX // PEER_TT
    tiles_per_lat = LAT_S // PEER_TT
    n_eb = P_EXPERTS // PEER_EB
    mod_row = lambda i: jnp.where(i < tiles_ctx, 0, 1 + (i - tiles_ctx) // tiles_per_lat)
    tok = lambda w: pl.BlockSpec((PEER_TT, w), lambda i, e: (i, 0))
    sspec = pl.BlockSpec((P_HEADS, P_KEYS, PEER_TT), lambda i, e: (0, 0, i))
    sel = pltpu.VMEM((P_HEADS, P_KEYS, PEER_TT), F32)
    if split_out:
        out_specs = [pl.BlockSpec((PEER_TT, D), lambda i, e: (jnp.minimum(i, tiles_ctx - 1), 0)),
                     pl.BlockSpec((PEER_TT, D), lambda i, e: (jnp.maximum(i - tiles_ctx, 0), 0))]
        out_shape = [jax.ShapeDtypeStruct((T_CTX, D), F32), jax.ShapeDtypeStruct((T_LAT, D), F32)]
    else:
        out_specs = tok(D)
        out_shape = jax.ShapeDtypeStruct((T_ALL, D), F32)
    return pl.pallas_call(
        functools.partial(_peer_kernel, split_out=split_out),
        grid=(T_ALL // PEER_TT, n_eb),
        in_specs=[pl.BlockSpec((D, PEER_TT), lambda i, e: (0, i)), sspec, sspec,
                  pl.BlockSpec((None, PEER_EB, D), lambda i, e: (l, e, 0)),
                  pl.BlockSpec((None, D, PEER_EB), lambda i, e: (l, 0, jnp.maximum(e - 1, 0))),
                  pl.BlockSpec((None, D, PEER_EB), lambda i, e: (l, 0, jnp.where(e == n_eb - 1, n_eb - 1, 0))),
                  tok(D),
                  pl.BlockSpec((1, 1, 6, D), lambda i, e: (l, mod_row(i), 0, 0)),
                  pl.BlockSpec((1, D), lambda i, e: (0, 0)),
                  pl.BlockSpec((1, D), lambda i, e: (0, 0))],
        out_specs=out_specs,
        out_shape=out_shape,
        scratch_shapes=[sel, sel, sel,
                        pltpu.VMEM((PEER_NCAND, PEER_TT), F32),
                        pltpu.VMEM((PEER_EB, PEER_TT), BF16),
                        pltpu.VMEM((PEER_EB, PEER_TT), BF16),
                        pltpu.VMEM((D, PEER_TT), F32)],
        compiler_params=_params(("arbitrary", "arbitrary")),
        name="peer",
    )(h2t, s1, s2, u_bf, vt_bf, vt_bf, x1, mods, g2, b2)


def _grid_angles(rot_dim):
    t = jnp.arange(LAT_S)
    row = (t // GRID_W).astype(F32)
    col = (t % GRID_W).astype(F32)
    n_freq = rot_dim // 4
    inv = THETA ** (-jnp.arange(n_freq, dtype=F32) / n_freq)
    return jnp.concatenate([row[:, None] * inv, col[:, None] * inv], axis=-1)


def _rope_tables(rot_dim, width):
    ang = _grid_angles(rot_dim)
    cos = jnp.cos(ang)
    sin = jnp.sin(ang)
    cos_t = jnp.tile(jnp.concatenate([cos, cos], axis=-1), (1, width // rot_dim))
    sin_t = jnp.tile(jnp.concatenate([-sin, sin], axis=-1), (1, width // rot_dim))
    ident_c = jnp.ones((1, TM, width), F32)
    ident_s = jnp.zeros((1, TM, width), F32)
    cos_t = jnp.concatenate([ident_c, cos_t.reshape(TILES_PER_LAT, TM, width)], axis=0)
    sin_t = jnp.concatenate([ident_s, sin_t.reshape(TILES_PER_LAT, TM, width)], axis=0)
    return cos_t, sin_t


def kernel(x_prompt, x_sample, cache_a_k, cache_a_v, cache_b_ckv, cache_b_kr, cache_c_k, cache_c_v, cache_d_k, cache_d_v, c, c_ctx, w_mod, b_mod, ln1_g, ln1_b, ln2_g, ln2_b, ev_w_in, ev_lam_q1, ev_lam_k1, ev_lam_q2, ev_lam_k2, ev_g_sub, ev_g_cq, ev_w_uq, ev_g_ckv, ev_w_ukv, ev_w_out, od_w_in, od_g_qn, od_g_kn, od_rpb, od_w_out, pk_w_q, pk_k1, pk_k2, pk_u, pk_v):
    x = jnp.concatenate([x_prompt.reshape(T_CTX, D), x_sample.reshape(T_LAT, D)], axis=0)
    cvecs = jnp.concatenate([c_ctx[None, :], c, jnp.zeros((8 - 1 - N_LAT_B, D), F32)], axis=0)
    mods = _modulation(cvecs, w_mod, b_mod).reshape(DEPTH, 8, 6, D)
    u_bf, vt_bf = _prep_tables(pk_u, pk_v)

    cos_a, sin_a = _rope_tables(D_A, 512)
    cos_b, sin_b = _rope_tables(B_ROPE, 256)
    rt_even = jnp.concatenate([cos_a, sin_a, cos_b, sin_b], axis=-1)
    rt_odd = jnp.concatenate(_rope_tables(D_C, 512), axis=-1)
    ones_bd = jnp.kron(jnp.eye(512 // D_C, dtype=F32), jnp.ones((D_C, D_C), F32)).astype(BF16)
    lat3 = lambda a, w: a.reshape(N_LAT_B, PAST, w)

    states = {}
    for l in range(DEPTH):
        j = l // 2
        if l % 2 == 0:
            lam_init = 0.8 - 0.6 * math.exp(-0.3 * l)
            w_in = jnp.pad(ev_w_in[j], ((0, 0), (0, EVEN_IN_PAD - ev_w_in.shape[-1]))).astype(BF16)
            wuq = ev_w_uq[j].reshape(B_QRANK, H_B, B_NOPE + B_ROPE)
            wuq = jnp.concatenate([wuq[:, :, :B_NOPE].reshape(B_QRANK, -1),
                                   wuq[:, :, B_NOPE:].reshape(B_QRANK, -1)], axis=1).astype(BF16)
            wukv = ev_w_ukv[j].reshape(B_KVRANK, H_B, B_NOPE + B_V)
            wukv = jnp.concatenate([wukv[:, :, :B_NOPE].reshape(B_KVRANK, -1),
                                    wukv[:, :, B_NOPE:].reshape(B_KVRANK, -1)], axis=1).astype(BF16)
            (qa, bq, ka, va, kvb, kr,
             states["a_k"], states["a_v"], states["b_ckv"], states["b_kr"]) = _proj_call(
                _proj_even_kernel, "proj_even", x, mods, l, rt_even,
                [w_in, ev_g_cq[j][None, :], wuq, ev_g_ckv[j][None, :], wukv],
                (512, 768, 512, 512, 1024, B_ROPE), (512, 512, B_KVRANK, B_ROPE))
            lam_vecs = [v[j][None, :] for v in (ev_lam_q1, ev_lam_k1, ev_lam_q2, ev_lam_k2)]
            g_sub = ev_g_sub[j][None, :]
            proj = (qa, bq, ka, va, kvb, kr)
            outs = _attn_even(lam_vecs, g_sub, lam_init, *proj)
            cache_kvb = _matmul(cache_b_ckv[:, j].reshape(N_LAT_B * PAST, B_KVRANK), wukv)
            cache = (lat3(cache_a_k[:, j], 512), lat3(cache_a_v[:, j], 512), cache_kvb,
                     lat3(cache_b_kr[:, j], B_ROPE))
            o1, o2 = _attn_even(lam_vecs, g_sub, lam_init, *proj, cache=cache, prev_out=outs)
            w_out = ev_w_out[j]
        else:
            w_in = od_w_in[j].astype(BF16)
            g_qn = jnp.tile(od_g_qn[j], H_C)[None, :]
            g_kn = jnp.tile(od_g_kn[j], G_C)[None, :]
            (cq, ck, cv, dq, dk, dv,
             states["c_k"], states["c_v"], states["d_k"], states["d_v"]) = _proj_call(
                _proj_odd_kernel, "proj_odd", x, mods, l, rt_odd, [w_in, g_qn, g_kn, ones_bd],
                (512, 128, 128, 512, 512, 512), (128, 128, 512, 512))
            o1 = _gqa(cq, ck, cv)
            o1 = _gqa(cq, ck, cv, cache=(lat3(cache_c_k[:, j], 128), lat3(cache_c_v[:, j], 128)),
                      prev_out=[o1])
            o2 = _mha(dq, dk, dv)
            bias = _na_bias(od_rpb[j])
            offs = jnp.arange(NA_ROWS + 1)[:, None] + jnp.arange(NA_ROWS)[None, :]
            bias_win = jnp.transpose(bias[:, offs], (1, 0, 3, 2, 4)).reshape(
                NA_ROWS + 1, H_D, GRID_W, NA_ROWS * GRID_W)
            o2 = _neighbourhood(dq, dk, dv, lat3(cache_d_k[:, j], 512), lat3(cache_d_v[:, j], 512),
                                bias_win, o2)
            w_out = od_w_out[j]

        x1, h2t, s1, s2 = _post(x, mods, l, o1, o2, w_out[:512].astype(BF16), w_out[512:].astype(BF16),
                                ln1_g[l][None, :], ln1_b[l][None, :], pk_w_q[l].astype(BF16),
                                pk_k1[l].astype(BF16), pk_k2[l].astype(BF16))
        x = _peer(h2t, s1, s2, u_bf, vt_bf, x1, mods, l, ln2_g[l][None, :], ln2_b[l][None, :],
                  split_out=(l == DEPTH - 1))

    y = x[0].reshape(N_CTX_B, CTX_S, D)
    z = x[1].reshape(N_LAT_B, LAT_S, D)
    n_even = (DEPTH + 1) // 2
    n_odd = DEPTH // 2
    return (y, z,
            states["a_k"].reshape(N_CTX_B, n_even, CTX_S, H_A, 2, D_A),
            states["a_v"].reshape(N_CTX_B, n_even, CTX_S, H_A, 2 * D_A),
            states["b_ckv"].reshape(N_CTX_B, n_even, CTX_S, B_KVRANK),
            states["b_kr"].reshape(N_CTX_B, n_even, CTX_S, B_ROPE),
            states["c_k"].reshape(N_CTX_B, n_odd, CTX_S, G_C, D_C),
            states["c_v"].reshape(N_CTX_B, n_odd, CTX_S, G_C, D_C),
            states["d_k"].reshape(N_CTX_B, n_odd, CTX_S, H_D, D_D),
            states["d_v"].reshape(N_CTX_B, n_odd, CTX_S, H_D, D_D))
```
